```python
import math
import jax
import jax.numpy as jnp
from jax import lax
import numpy as np

D_MODEL = 1024
BATCH = 4
SEQ = 4096
DEPTH = 2

GRID_W = 64
CTX_LEN = 256

A_HEADS = 8
A_HEAD_DIM = 64
A_W = A_HEADS * A_HEAD_DIM
A_DECAY_RANK = 64
A_ICL_RANK = 64
A_GATE_RANK = 128
A_IN = 3 * A_W + 2 * A_DECAY_RANK + 2 * A_ICL_RANK + A_GATE_RANK
A_DECAY_SCALE = math.exp(-0.5)
GN_EPS_RWKV = 64e-5
B_W = 512
B_BLOCKS = 8
B_BLOCK = B_W // B_BLOCKS
B_CONV = 4
B_C = 8.0
B_IN = 2 * B_W
C_HEADS = 4
C_HEAD_DIM = 128
C_W = C_HEADS * C_HEAD_DIM
C_IN = 4 * C_W
ROPE_BASE = 10000.0
D_HEADS = 4
D_HEAD_DIM = 128
D_W = D_HEADS * D_HEAD_DIM
D_IN = 4 * D_W + 4 * D_HEADS
CHUNK = 128
EVEN_IN = A_IN + B_IN
ODD_IN = C_IN + D_IN
MIX_W = A_W + B_W
N_EXPERTS = 32
TOP_K = 4
D_FF = 1024
SWIGLU_ALPHA = 1.702
SWIGLU_LIMIT = 7.0
DN_ALPHA = (2 * DEPTH) ** 0.25
DN_BETA = (8 * DEPTH) ** -0.25
LN_EPS = 1e-5
N_MOD = 6
N_EVEN = (DEPTH + 1) // 2
N_ODD = DEPTH // 2

kernel_name = 'hybrid_rwkv7_rglru_retention_mlstm_moe_dit'


def _flip(t):
    return t[:, ::-1]


def _ident(t):
    return t


def _heads(t, n):
    return t.reshape(t.shape[:-1] + (n, t.shape[-1] // n))


def layer_norm(x, g, b):
    xf = x.astype(jnp.float32)
    mu = jnp.mean(xf, -1, keepdims=True)
    var = jnp.mean(jnp.square(xf - mu), -1, keepdims=True)
    return ((xf - mu) * lax.rsqrt(var + LN_EPS) * g + b).astype(x.dtype)


def group_norm(x, g, b, n_groups, eps):
    xf = _heads(x.astype(jnp.float32), n_groups)
    mu = jnp.mean(xf, -1, keepdims=True)
    var = jnp.mean(jnp.square(xf - mu), -1, keepdims=True)
    return ((xf - mu) * lax.rsqrt(var + eps)).reshape(x.shape) * g + b


def grid_qshift(z):
    bz, t, w = z.shape
    rows = t // GRID_W
    g = z.reshape(bz, rows, GRID_W, w // 4, 4)
    left = jnp.pad(g[:, :, :-1, :, 0], ((0, 0), (0, 0), (1, 0), (0, 0)))
    right = jnp.pad(g[:, :, 1:, :, 1], ((0, 0), (0, 0), (0, 1), (0, 0)))
    up = jnp.pad(g[:, :-1, :, :, 2], ((0, 0), (1, 0), (0, 0), (0, 0)))
    down = jnp.pad(g[:, 1:, :, :, 3], ((0, 0), (0, 1), (0, 0), (0, 0)))
    return jnp.stack([left, right, up, down], -1).reshape(bz, t, w)


def bidir_shift(z):
    bz, t, w = z.shape
    g = z.reshape(bz, t, w // 2, 2)
    prev = jnp.pad(g[:, :-1, :, 0], ((0, 0), (1, 0), (0, 0)))
    nxt = jnp.pad(g[:, 1:, :, 1], ((0, 0), (0, 1), (0, 0)))
    return jnp.stack([prev, nxt], -1).reshape(bz, t, w)


def axial_rope(t):
    n_tok, dh = t.shape[1], t.shape[-1]
    idx = jnp.arange(n_tok)
    row = (idx // GRID_W).astype(jnp.float32)
    col = (idx % GRID_W).astype(jnp.float32)
    n_freq = dh // 4
    inv = ROPE_BASE ** (-jnp.arange(n_freq, dtype=jnp.float32) / n_freq)
    ang = jnp.concatenate([row[:, None] * inv, col[:, None] * inv], -1)[None, :, None, :]
    cos, sin = jnp.cos(ang), jnp.sin(ang)
    t1, t2 = t[..., : dh // 2], t[..., dh // 2:]
    return jnp.concatenate([t1 * cos - t2 * sin, t1 * sin + t2 * cos], -1)


def centred_depthwise_conv(u, w, bias):
    y = lax.conv_general_dilated(
        u, w[:, None, :].astype(u.dtype), window_strides=(1,),
        padding=[(B_CONV // 2, B_CONV - 1 - B_CONV // 2)],
        dimension_numbers=('NWC', 'WIO', 'NWC'), feature_group_count=u.shape[-1])
    return y + bias


def _lin_comb(left, right):
    al, bl = left
    ar, br = right
    return al * ar, ar * bl + br


def linear_recurrence(a, b, h0):
    b = b.at[:, 0].add(a[:, 0] * h0)
    _, h = lax.associative_scan(_lin_comb, (a, b), axis=1)
    return h, h[:, -1]


def rwkv7_scan(s0, r, w, k, v, kk, a):
    xs = tuple(jnp.moveaxis(t, 1, 0) for t in (r, w, k, v, kk, a))

    def step(s, inp):
        r_t, w_t, k_t, v_t, kk_t, a_t = inp
        s_kk = jnp.einsum('bhvk,bhk->bhv', s, kk_t)
        s = (s * w_t[:, :, None, :] - s_kk[..., None] * (kk_t * a_t)[:, :, None, :]
             + v_t[..., None] * k_t[:, :, None, :])
        return s, jnp.einsum('bhvk,bhk->bhv', s, r_t)

    s_end, out = lax.scan(step, s0, xs)
    return jnp.moveaxis(out, 0, 1), s_end


def rwkv7_group(pc, pl, mu, w0, w_b, a0, a_b, g_b, k_k, k_a, r_k, gn_g, gn_b):
    zc = pc + (bidir_shift(pc) - pc) * mu
    zl = pl + (grid_qshift(pl) - pl) * mu

    def prep(z):
        z = z.astype(jnp.float32)
        r, k, v = (z[..., j * A_W:(j + 1) * A_W] for j in range(3))
        o = 3 * A_W
        wd = z[..., o:o + 2 * A_DECAY_RANK].reshape(z.shape[:-1] + (2, A_DECAY_RANK))
        o += 2 * A_DECAY_RANK
        ad = z[..., o:o + 2 * A_ICL_RANK].reshape(z.shape[:-1] + (2, A_ICL_RANK))
        o += 2 * A_ICL_RANK
        g = jax.nn.sigmoid(z[..., o:o + A_GATE_RANK]) @ g_b
        kk = _heads(k * k_k, A_HEADS)
        kk = kk / jnp.maximum(jnp.sqrt(jnp.sum(kk * kk, -1, keepdims=True)), 1e-12)
        dirs = []
        for d in range(2):
            w = jnp.exp(-A_DECAY_SCALE * jax.nn.sigmoid(w0[d] + jnp.tanh(wd[..., d, :]) @ w_b[d]))
            a = jax.nn.sigmoid(a0[d] + ad[..., d, :] @ a_b[d])
            kd = k * (1.0 + (a - 1.0) * k_a)
            dirs.append(tuple(_heads(t, A_HEADS) for t in (r, w, kd, v)) + (kk, _heads(a, A_HEADS)))
        return dirs, g

    def finish(wkv, dirs, g):
        r, v = dirs[0][0], dirs[0][3]
        bonus = sum(jnp.sum(r * dd[2] * r_k, -1, keepdims=True) for dd in dirs) * v
        y = group_norm(wkv.reshape(g.shape), gn_g, gn_b, A_HEADS, GN_EPS_RWKV) + bonus.reshape(g.shape)
        return y * g

    dirs_c, g_c = prep(zc)
    dirs_l, g_l = prep(zl)
    s0 = jnp.zeros((pc.shape[0], A_HEADS, A_HEAD_DIM, A_HEAD_DIM), jnp.float32)
    wkv_c = 0.0
    wkv_l = 0.0
    for d, fl in enumerate((_ident, _flip)):
        out_c, s_c = rwkv7_scan(s0, *(fl(t) for t in dirs_c[d]))
        out_l, _ = rwkv7_scan(s_c, *(fl(t) for t in dirs_l[d]))
        wkv_c = wkv_c + fl(out_c)
        wkv_l = wkv_l + fl(out_l)
    return finish(wkv_c, dirs_c, g_c).astype(pc.dtype), finish(wkv_l, dirs_l, g_l).astype(pl.dtype)


def rglru_group(pc, pl, conv_w, conv_b, gate_w, gate_b, lam):
    def prep(p):
        u = centred_depthwise_conv(p[..., :B_W], conv_w, conv_b).astype(jnp.float32)
        gate = jax.nn.gelu(p[..., B_W:].astype(jnp.float32))
        pre = jnp.einsum('btgi,dsgio->btdsgo', _heads(u, B_BLOCKS), gate_w)
        pre = pre.reshape(u.shape[:2] + (2, 2, B_W)) + gate_b
        rec = jax.nn.sigmoid(pre[..., 0, :])
        inp = jax.nn.sigmoid(pre[..., 1, :])
        log_a = -B_C * rec * jax.nn.softplus(-lam)
        a = jnp.exp(log_a)
        bx = jnp.sqrt(-jnp.expm1(2.0 * log_a)) * inp * u[..., None, :]
        return a, bx, gate

    a_c, b_c, g_c = prep(pc)
    a_l, b_l, g_l = prep(pl)
    h0 = jnp.zeros((pc.shape[0], B_W), jnp.float32)
    h_c = 0.0
    h_l = 0.0
    for d, fl in enumerate((_ident, _flip)):
        hc_d, h_end = linear_recurrence(fl(a_c[..., d, :]), fl(b_c[..., d, :]), h0)
        hl_d, _ = linear_recurrence(fl(a_l[..., d, :]), fl(b_l[..., d, :]), h_end)
        h_c = h_c + fl(hc_d)
        h_l = h_l + fl(hl_d)
    return (h_c * g_c).astype(pc.dtype), (h_l * g_l).astype(pl.dtype)


def retention_chunked(q, k, v, log_gamma, r0):
    bq, n_tok, nh, dk = q.shape
    dv = v.shape[-1]
    n = n_tok // CHUNK
    q, k = (t.reshape(bq, n, CHUNK, nh, dk) for t in (q, k))
    v = v.reshape(bq, n, CHUNK, nh, dv)
    pos = jnp.arange(CHUNK, dtype=jnp.float32)
    lg = log_gamma.astype(jnp.float32)[:, None]
    diff = pos[:, None] - pos[None, :]
    decay = jnp.where(diff >= 0, jnp.exp(lg[:, :, None] * jnp.maximum(diff, 0.0)), 0.0)
    scores = jnp.einsum('bnihd,bnjhd->bnhij', q, k) * decay
    intra = jnp.einsum('bnhij,bnjhe->bnihe', scores, v)
    zeta = jnp.exp(lg * (CHUNK - 1.0 - pos))
    kv = jnp.einsum('bnjhd,hj,bnjhe->bnhde', k, zeta, v)
    chunk_decay = jnp.exp(lg[:, 0] * CHUNK)[:, None, None]

    def step(r, kv_i):
        return r * chunk_decay + kv_i, r

    r_last, r_prev = lax.scan(step, r0, jnp.moveaxis(kv, 1, 0))
    r_prev = jnp.moveaxis(r_prev, 0, 1)
    xi = jnp.exp(lg * (pos + 1.0)).T
    inter = jnp.einsum('bnihd,bnhde->bnihe', q, r_prev) * xi[:, :, None]
    return (intra + inter).reshape(bq, n_tok, nh, dv), r_last


def retention_group(pc, pl, log_gamma, gn_g, gn_b):
    def prep(p, rotate):
        p = p.astype(jnp.float32)
        q, k, v = (_heads(p[..., j * C_W:(j + 1) * C_W], C_HEADS) for j in range(3))
        k = k * C_HEAD_DIM ** -0.5
        if rotate:
            q, k = axial_rope(q), axial_rope(k)
        return q, k, v, p[..., 3 * C_W:]

    qc, kc, vc, gc = prep(pc, False)
    ql, kl, vl, gl = prep(pl, True)
    r0 = jnp.zeros((pc.shape[0], C_HEADS, C_HEAD_DIM, C_HEAD_DIM), jnp.float32)
    o_c = 0.0
    o_l = 0.0
    for d, fl in enumerate((_ident, _flip)):
        oc_d, r_c = retention_chunked(fl(qc), fl(kc), fl(vc), log_gamma[d], r0)
        ol_d, _ = retention_chunked(fl(ql), fl(kl), fl(vl), log_gamma[d], r_c)
        o_c = o_c + fl(oc_d)
        o_l = o_l + fl(ol_d)
    y_c = group_norm(o_c.reshape(gc.shape), gn_g, gn_b, C_HEADS, LN_EPS) * jax.nn.silu(gc)
    y_l = group_norm(o_l.reshape(gl.shape), gn_g, gn_b, C_HEADS, LN_EPS) * jax.nn.silu(gl)
    return y_c.astype(pc.dtype), y_l.astype(pl.dtype)


def mlstm_chunked(q, k, v, ig, lf, state):
    bq, n_tok, nh, dh = q.shape
    n = n_tok // CHUNK
    q, k, v = (t.reshape(bq, n, CHUNK, nh, dh) for t in (q, k, v))
    ig, lf = (t.reshape(bq, n, CHUNK, nh) for t in (ig, lf))
    b = jnp.cumsum(lf, axis=2)
    tri = jnp.tril(jnp.ones((CHUNK, CHUNK), bool))[None, None, :, :, None]
    d_log = jnp.where(tri, b[:, :, :, None, :] - b[:, :, None, :, :] + ig[:, :, None, :, :], -jnp.inf)
    b_end = b[:, :, -1]
    w_end = b_end[:, :, None, :] - b + ig
    m_end = jnp.max(w_end, axis=2)
    e_end = jnp.exp(w_end - m_end[:, :, None, :])
    c_chunk = jnp.einsum('bnjh,bnjhd,bnjhe->bnhed', e_end, k, v)
    n_chunk = jnp.einsum('bnjh,bnjhd->bnhd', e_end, k)

    def step(carry, inp):
        c_st, n_st, m_st = carry
        c_i, n_i, m_i, b_i = inp
        m_new = jnp.maximum(b_i + m_st, m_i)
        s_old = jnp.exp(b_i + m_st - m_new)
        s_new = jnp.exp(m_i - m_new)
        new = (c_st * s_old[..., None, None] + c_i * s_new[..., None, None],
               n_st * s_old[..., None] + n_i * s_new[..., None], m_new)
        return new, carry

    xs = tuple(jnp.moveaxis(t, 1, 0) for t in (c_chunk, n_chunk, m_end, b_end))
    final, prev = lax.scan(step, state, xs)
    c_prev, n_prev, m_prev = (jnp.moveaxis(t, 0, 1) for t in prev)
    g_inter = b + m_prev[:, :, None, :]
    m_t = jnp.maximum(g_inter, jnp.max(d_log, axis=3))
    s_intra = jnp.einsum('bnihd,bnjhd->bnijh', q, k) * jnp.exp(d_log - m_t[:, :, :, None, :])
    s_inter = jnp.exp(g_inter - m_t)
    num = (jnp.einsum('bnijh,bnjhe->bnihe', s_intra, v)
           + jnp.einsum('bnihd,bnhed->bnihe', q, c_prev) * s_inter[..., None])
    den = jnp.sum(s_intra, axis=3) + jnp.einsum('bnihd,bnhd->bnih', q, n_prev) * s_inter
    den = jnp.maximum(jnp.abs(den), jnp.exp(-m_t))
    h = num / den[..., None]
    return h.reshape(bq, n_tok, nh, dh), final


def mlstm_group(pc, pl, ibias, fbias, gn_g, gn_b):
    def prep(p):
        p = p.astype(jnp.float32)
        q, k, v = (_heads(p[..., j * D_W:(j + 1) * D_W], D_HEADS) for j in range(3))
        gt = p[..., 4 * D_W:].reshape(p.shape[:2] + (2, 2, D_HEADS))
        ig = gt[..., 0, :, :] + ibias
        lf = jax.nn.log_sigmoid(gt[..., 1, :, :] + fbias)
        return q, k * D_HEAD_DIM ** -0.5, v, p[..., 3 * D_W:4 * D_W], ig, lf

    qc, kc, vc, oc, igc, lfc = prep(pc)
    ql, kl, vl, ol, igl, lfl = prep(pl)
    bz = pc.shape[0]
    s0 = (jnp.zeros((bz, D_HEADS, D_HEAD_DIM, D_HEAD_DIM), jnp.float32),
          jnp.zeros((bz, D_HEADS, D_HEAD_DIM), jnp.float32),
          jnp.full((bz, D_HEADS), -jnp.inf, jnp.float32))
    h_c = 0.0
    h_l = 0.0
    for d, fl in enumerate((_ident, _flip)):
        hc_d, s_c = mlstm_chunked(fl(qc), fl(kc), fl(vc), fl(igc[..., d, :]), fl(lfc[..., d, :]), s0)
        hl_d, _ = mlstm_chunked(fl(ql), fl(kl), fl(vl), fl(igl[..., d, :]), fl(lfl[..., d, :]), s_c)
        h_c = h_c + fl(hc_d)
        h_l = h_l + fl(hl_d)
    y_c = group_norm(h_c.reshape(oc.shape), gn_g, gn_b, D_HEADS, LN_EPS) * jax.nn.sigmoid(oc)
    y_l = group_norm(h_l.reshape(ol.shape), gn_g, gn_b, D_HEADS, LN_EPS) * jax.nn.sigmoid(ol)
    return y_c.astype(pc.dtype), y_l.astype(pl.dtype)


def moe(h, router_w, router_b, w1, b1, w2, b2):
    logits = (h @ router_w + router_b).astype(jnp.float32)
    top_val, top_idx = lax.top_k(logits, TOP_K)
    top_w = jax.nn.softmax(top_val, axis=-1)
    gates = jnp.sum(jax.nn.one_hot(top_idx, N_EXPERTS, dtype=jnp.float32) * top_w[..., None], axis=1)
    gates = gates.astype(h.dtype)
    y = jnp.zeros_like(h)
    for e in range(N_EXPERTS):
        u = h @ w1[e] + b1[e]
        glu = jnp.minimum(u[:, :D_FF], SWIGLU_LIMIT)
        lin = jnp.clip(u[:, D_FF:], -SWIGLU_LIMIT, SWIGLU_LIMIT)
        act = (lin + 1.0) * glu * jax.nn.sigmoid(SWIGLU_ALPHA * glu)
        y = y + gates[:, e:e + 1] * (act @ w2[e] + b2[e])
    return y


def setup_inputs(seed: int = 0) -> dict:
    key = jax.random.key(seed)
    ks = iter(jax.random.split(key, 64))

    def nrm(shape, scale):
        return jax.random.normal(next(ks), shape, jnp.float32) * scale

    lam_u = jax.random.uniform(next(ks), (N_EVEN, 2, B_W), jnp.float32, 0.9, 0.999)
    lam_base = lam_u ** (1.0 / B_C)
    gammas = jnp.log(1.0 - 2.0 ** (-5.0 - jnp.arange(C_HEADS, dtype=jnp.float32)))
    return {
        'x': nrm((BATCH, SEQ, D_MODEL), 1.0),
        'c': nrm((BATCH, D_MODEL), 1.0),
        'ctx': nrm((BATCH, CTX_LEN, D_MODEL), 1.0),
        'c_ctx': nrm((D_MODEL,), 1.0),
        'w_mod': nrm((DEPTH, D_MODEL, N_MOD * D_MODEL), D_MODEL ** -0.5),
        'b_mod': nrm((DEPTH, N_MOD * D_MODEL), 0.02),
        'ln_g': 1.0 + nrm((DEPTH, 2, D_MODEL), 0.02),
        'ln_b': nrm((DEPTH, 2, D_MODEL), 0.02),
        'even_w_in': nrm((N_EVEN, D_MODEL, EVEN_IN), D_MODEL ** -0.5),
        'even_w_out': nrm((N_EVEN, MIX_W, D_MODEL), MIX_W ** -0.5 * DN_BETA),
        'a_mu': jax.random.uniform(next(ks), (N_EVEN, A_IN), jnp.float32),
        'a_w0': jnp.linspace(-5.0, 1.0, A_W) + nrm((N_EVEN, 2, A_W), 0.1),
        'a_wB': nrm((N_EVEN, 2, A_DECAY_RANK, A_W), 0.1),
        'a_a0': nrm((N_EVEN, 2, A_W), 0.1),
        'a_aB': nrm((N_EVEN, 2, A_ICL_RANK, A_W), 0.5 * A_ICL_RANK ** -0.5),
        'a_gB': nrm((N_EVEN, A_GATE_RANK, A_W), A_GATE_RANK ** -0.5),
        'a_kk': 0.85 + nrm((N_EVEN, A_W), 0.05),
        'a_ka': 1.0 + nrm((N_EVEN, A_W), 0.05),
        'a_rk': nrm((N_EVEN, A_HEADS, A_HEAD_DIM), 0.1),
        'a_gn_g': 1.0 + nrm((N_EVEN, A_W), 0.02),
        'a_gn_b': nrm((N_EVEN, A_W), 0.02),
        'b_conv_w': nrm((N_EVEN, B_CONV, B_W), 0.5),
        'b_conv_b': nrm((N_EVEN, B_W), 0.02),
        'b_gate_w': nrm((N_EVEN, 2, 2, B_BLOCKS, B_BLOCK, B_BLOCK), B_BLOCK ** -0.5),
        'b_gate_b': nrm((N_EVEN, 2, 2, B_W), 0.02),
        'b_lam': jnp.log(lam_base) - jnp.log1p(-lam_base),
        'odd_w_in': nrm((N_ODD, D_MODEL, ODD_IN), D_MODEL ** -0.5),
        'odd_w_out': nrm((N_ODD, MIX_W, D_MODEL), MIX_W ** -0.5 * DN_BETA),
        'c_log_gamma': gammas * (1.0 + nrm((N_ODD, 2, C_HEADS), 0.05)),
        'c_gn_g': 1.0 + nrm((N_ODD, C_W), 0.02),
        'c_gn_b': nrm((N_ODD, C_W), 0.02),
        'd_ibias': nrm((N_ODD, 2, D_HEADS), 0.1),
        'd_fbias': jnp.linspace(3.0, 6.0, D_HEADS) + nrm((N_ODD, 2, D_HEADS), 0.1),
        'd_gn_g': 1.0 + nrm((N_ODD, D_W), 0.02),
        'd_gn_b': nrm((N_ODD, D_W), 0.02),
        'router_w': nrm((DEPTH, D_MODEL, N_EXPERTS), D_MODEL ** -0.5),
        'router_b': nrm((DEPTH, N_EXPERTS), 0.01),
        'exp_w1': nrm((DEPTH, N_EXPERTS, D_MODEL, 2 * D_FF), D_MODEL ** -0.5),
        'exp_b1': nrm((DEPTH, N_EXPERTS, 2 * D_FF), 0.02),
        'exp_w2': nrm((DEPTH, N_EXPERTS, D_FF, D_MODEL), D_FF ** -0.5 * DN_BETA),
        'exp_b2': nrm((DEPTH, N_EXPERTS, D_MODEL), 0.02),
    }


def reference(x, c, ctx, c_ctx, w_mod, b_mod, ln_g, ln_b, even_w_in, even_w_out, a_mu, a_w0, a_wB,
              a_a0, a_aB, a_gB, a_kk, a_ka, a_rk, a_gn_g, a_gn_b, b_conv_w, b_conv_b, b_gate_w,
              b_gate_b, b_lam, odd_w_in, odd_w_out, c_log_gamma, c_gn_g, c_gn_b, d_ibias, d_fbias,
              d_gn_g, d_gn_b, router_w, router_b, exp_w1, exp_b1, exp_w2, exp_b2):
    xl, xc = x, ctx
    bz = x.shape[0]
    for layer in range(DEPTH):
        last = layer == DEPTH - 1
        mod_l = (jax.nn.silu(c) @ w_mod[layer] + b_mod[layer]).reshape(bz, N_MOD, 1, D_MODEL)
        mod_c = (jax.nn.silu(c_ctx) @ w_mod[layer] + b_mod[layer]).reshape(N_MOD, 1, D_MODEL)
        hl = xl * (1.0 + mod_l[:, 1]) + mod_l[:, 0]
        hc = xc * (1.0 + mod_c[1]) + mod_c[0]
        i = layer // 2
        if layer % 2 == 0:
            pc, pl = hc @ even_w_in[i], hl @ even_w_in[i]
            ya_c, ya_l = rwkv7_group(pc[..., :A_IN], pl[..., :A_IN], a_mu[i], a_w0[i], a_wB[i], a_a0[i],
                                     a_aB[i], a_gB[i], a_kk[i], a_ka[i], a_rk[i], a_gn_g[i], a_gn_b[i])
            yb_c, yb_l = rglru_group(pc[..., A_IN:], pl[..., A_IN:], b_conv_w[i], b_conv_b[i],
                                     b_gate_w[i], b_gate_b[i], b_lam[i])
            w_out = even_w_out[i]
        else:
            pc, pl = hc @ odd_w_in[i], hl @ odd_w_in[i]
            ya_c, ya_l = retention_group(pc[..., :C_IN], pl[..., :C_IN], c_log_gamma[i], c_gn_g[i], c_gn_b[i])
            yb_c, yb_l = mlstm_group(pc[..., C_IN:], pl[..., C_IN:], d_ibias[i], d_fbias[i], d_gn_g[i], d_gn_b[i])
            w_out = odd_w_out[i]
        y_l = jnp.concatenate([ya_l, yb_l], -1) @ w_out
        xl = layer_norm(DN_ALPHA * xl + mod_l[:, 2] * y_l, ln_g[layer, 0], ln_b[layer, 0])
        moe_args = (router_w[layer], router_b[layer], exp_w1[layer], exp_b1[layer], exp_w2[layer], exp_b2[layer])
        hl = xl * (1.0 + mod_l[:, 4]) + mod_l[:, 3]
        if last:
            ffn_l = moe(hl.reshape(-1, D_MODEL), *moe_args).reshape(hl.shape)
        else:
            y_c = jnp.concatenate([ya_c, yb_c], -1) @ w_out
            xc = layer_norm(DN_ALPHA * xc + mod_c[2] * y_c, ln_g[layer, 0], ln_b[layer, 0])
            hc = xc * (1.0 + mod_c[4]) + mod_c[3]
            n_ctx = hc.shape[0] * hc.shape[1]
            ffn = moe(jnp.concatenate([hc.reshape(-1, D_MODEL), hl.reshape(-1, D_MODEL)], 0), *moe_args)
            ffn_c = ffn[:n_ctx].reshape(hc.shape)
            ffn_l = ffn[n_ctx:].reshape(hl.shape)
            xc = layer_norm(DN_ALPHA * xc + mod_c[5] * ffn_c, ln_g[layer, 1], ln_b[layer, 1])
        xl = layer_norm(DN_ALPHA * xl + mod_l[:, 5] * ffn_l, ln_g[layer, 1], ln_b[layer, 1])
    return xl
```

```python
import functools
import math

import jax
import jax.numpy as jnp
from jax import lax
from jax.experimental import pallas as pl
from jax.experimental.pallas import tpu as pltpu

F32 = jnp.float32
BF16 = jnp.bfloat16

GRID_W = 64
A_HEADS = 8
A_HEAD_DIM = 64
A_W = 512
A_DECAY_RANK = 64
A_ICL_RANK = 64
A_GATE_RANK = 128
A_IN = 1920
A_DECAY_SCALE = math.exp(-0.5)
GN_EPS_RWKV = 64e-5
B_W = 512
B_BLOCKS = 8
B_C = 8.0
B_IN = 1024
C_HEADS = 4
C_HEAD_DIM = 128
C_W = 512
C_IN = 2048
ROPE_BASE = 10000.0
D_HEADS = 4
D_HEAD_DIM = 128
D_W = 512
D_IN = 2064
D_IN_PAD = 2176
N_EXPERTS = 32
TOP_K = 4
D_FF = 1024
SWIGLU_ALPHA = 1.702
SWIGLU_LIMIT = 7.0
LN_EPS = 1e-5
N_MOD = 6
LANES = 128

ROW_TILE = 256
RWKV_CHUNK = 64
ODD_CHUNK = 256
MOE_TILE = 1024
VMEM_LIMIT_BYTES = 56 * 1024 * 1024


def _params(*sem):
    return pltpu.CompilerParams(dimension_semantics=sem, vmem_limit_bytes=VMEM_LIMIT_BYTES)


def _dot(a, b):
    return jnp.dot(a.astype(BF16), b.astype(BF16), preferred_element_type=F32)


def _dot_nt(a, b):
    return lax.dot_general(a.astype(BF16), b.astype(BF16), (((1,), (1,)), ((), ())),
                           preferred_element_type=F32)


def _dot_tn(a, b):
    return lax.dot_general(a.astype(BF16), b.astype(BF16), (((0,), (0,)), ((), ())),
                           preferred_element_type=F32)


def _split(x):
    hi = x.astype(BF16)
    lo = (x - hi.astype(F32)).astype(BF16)
    return hi, lo


def _dot_x3(a, b):
    ah, al = _split(a)
    bh, bl = _split(b)
    out = jnp.dot(ah, bh, preferred_element_type=F32)
    out += jnp.dot(ah, bl, preferred_element_type=F32)
    out += jnp.dot(al, bh, preferred_element_type=F32)
    return out


def _dot_nt_x3(a, b):
    ah, al = _split(a)
    bh, bl = _split(b)
    dn = (((1,), (1,)), ((), ()))
    out = lax.dot_general(ah, bh, dn, preferred_element_type=F32)
    out += lax.dot_general(ah, bl, dn, preferred_element_type=F32)
    out += lax.dot_general(al, bh, dn, preferred_element_type=F32)
    return out


def _dot_exact_lhs(a, b):
    a = a.astype(BF16)
    b1 = b.astype(BF16)
    r1 = b - b1.astype(F32)
    b2 = r1.astype(BF16)
    b3 = (r1 - b2.astype(F32)).astype(BF16)
    out = jnp.dot(a, b1, preferred_element_type=F32)
    out += jnp.dot(a, b2, preferred_element_type=F32)
    out += jnp.dot(a, b3, preferred_element_type=F32)
    return out


def _sigmoid(x):
    return 1.0 / (1.0 + jnp.exp(-x))


def _softplus(x):
    return jnp.maximum(x, 0.0) + jnp.log(1.0 + jnp.exp(-jnp.abs(x)))


def _seg_order(i, d, n_ctx, n_tot):
    rev = jnp.where(i < n_ctx, n_ctx - 1 - i, n_ctx + n_tot - 1 - i)
    return jnp.where(d == 0, i, rev)


def _mod_kernel(c_ref, w_ref, b_ref, o_ref):
    c = c_ref[...]
    s = c * _sigmoid(c)
    o_ref[...] = _dot_x3(s, w_ref[...]) + b_ref[...]


def _mod_vectors(c8, w, b):
    d, n = w.shape
    tn = n // 4
    return pl.pallas_call(
        _mod_kernel,
        grid=(n // tn,),
        in_specs=[pl.BlockSpec((8, d), lambda j: (0, 0)),
                  pl.BlockSpec((d, tn), lambda j: (0, j)),
                  pl.BlockSpec((1, tn), lambda j: (0, j))],
        out_specs=pl.BlockSpec((8, tn), lambda j: (0, j)),
        out_shape=jax.ShapeDtypeStruct((8, n), F32),
        compiler_params=_params("parallel"),
        name="mod_vectors",
    )(c8, w, b.reshape(1, n))


def _inproj_kernel(x_ref, sh_ref, sc_ref, w_ref, oa_ref, ob_ref, *, n_a):
    h = x_ref[0] * (1.0 + sc_ref[0, 0]) + sh_ref[0, 0]
    p = jnp.dot(h.astype(BF16), w_ref[...], preferred_element_type=F32)
    oa_ref[0] = p[:, :n_a]
    ob_ref[0] = p[:, n_a:]


def _inproj(x, shift, scale, w_bf16, n_a, n_ctx_tiles):
    bz, t, d = x.shape
    n = w_bf16.shape[1]
    n_b = n - n_a
    seg = lambda i: (i >= n_ctx_tiles).astype(jnp.int32)
    return pl.pallas_call(
        functools.partial(_inproj_kernel, n_a=n_a),
        grid=(bz, t // ROW_TILE),
        in_specs=[pl.BlockSpec((1, ROW_TILE, d), lambda b, i: (b, i, 0)),
                  pl.BlockSpec((1, 1, 1, d), lambda b, i: (b, seg(i), 0, 0)),
                  pl.BlockSpec((1, 1, 1, d), lambda b, i: (b, seg(i), 0, 0)),
                  pl.BlockSpec((d, n), lambda b, i: (0, 0))],
        out_specs=[pl.BlockSpec((1, ROW_TILE, n_a), lambda b, i: (b, i, 0)),
                   pl.BlockSpec((1, ROW_TILE, n_b), lambda b, i: (b, i, 0))],
        out_shape=[jax.ShapeDtypeStruct((bz, t, n_a), F32),
                   jax.ShapeDtypeStruct((bz, t, n_b), F32)],
        compiler_params=_params("parallel", "parallel"),
        name="inproj",
    )(x, shift, scale, w_bf16)


def _rwkv_prep_kernel(p_ref, up_ref, dn_ref, mu_ref, w0_ref, wb_ref, a0_ref, ab_ref, gb_ref,
                      kkw_ref, kaw_ref, rkw_ref, ones_ref,
                      r_o, v_o, kk_o, g_o, bonus_o, lw_o, kd_o, a_o, *, n_ctx_tiles, n_tiles):
    i = pl.program_id(1)
    p = p_ref[0]
    tr, w = p.shape
    row = lax.broadcasted_iota(jnp.int32, (tr, w), 0)
    lane = lax.broadcasted_iota(jnp.int32, (tr, w), 1)
    prev = pltpu.roll(p, 1, 0)
    nxt = pltpu.roll(p, tr - 1, 0)
    sh_ctx = jnp.where(lane % 2 == 0, jnp.where(row == 0, 0.0, prev), jnp.where(row == tr - 1, 0.0, nxt))
    col = row % GRID_W
    left = jnp.where(col == 0, 0.0, prev)
    right = jnp.where(col == GRID_W - 1, 0.0, nxt)
    up_halo = jnp.where(i > n_ctx_tiles, up_ref[0], 0.0)
    dn_halo = jnp.where(i < n_tiles - 1, dn_ref[0], 0.0)
    up = jnp.concatenate([up_halo, p[:tr - GRID_W]], axis=0)
    down = jnp.concatenate([p[GRID_W:], dn_halo], axis=0)
    slot = lane % 4
    sh_lat = jnp.where(slot == 0, left, jnp.where(slot == 1, right, jnp.where(slot == 2, up, down)))
    sh = jnp.where(i < n_ctx_tiles, sh_ctx, sh_lat)
    z = p + (sh - p) * mu_ref[...]

    r = z[:, 0:A_W]
    k = z[:, A_W:2 * A_W]
    v = z[:, 2 * A_W:3 * A_W]
    o = 3 * A_W
    wd = z[:, o:o + 2 * A_DECAY_RANK]
    o += 2 * A_DECAY_RANK
    ad = z[:, o:o + 2 * A_ICL_RANK]
    o += 2 * A_ICL_RANK
    zg = z[:, o:o + A_GATE_RANK]

    ones_bd = ones_ref[...]

    def head_sum(x):
        hi, lo = _split(x)
        return (jnp.dot(hi, ones_bd, preferred_element_type=F32)
                + jnp.dot(lo, ones_bd, preferred_element_type=F32))

    g_o[0] = _dot(_sigmoid(zg), gb_ref[...])
    kk = k * kkw_ref[...]
    kk = kk / jnp.maximum(jnp.sqrt(head_sum(kk * kk)), 1e-12)
    r_o[0] = r
    v_o[0] = v
    kk_o[0] = kk
    rr = r * rkw_ref[...]
    bsum = jnp.zeros_like(r)
    for d in range(2):
        wdd = jnp.tanh(wd[:, d * A_DECAY_RANK:(d + 1) * A_DECAY_RANK])
        lw = -A_DECAY_SCALE * _sigmoid(w0_ref[d] + _dot(wdd, wb_ref[d]))
        a = _sigmoid(a0_ref[d] + _dot(ad[:, d * A_ICL_RANK:(d + 1) * A_ICL_RANK], ab_ref[d]))
        kd = k * (1.0 + (a - 1.0) * kaw_ref[...])
        lw_o[d, 0] = lw
        kd_o[d, 0] = kd
        a_o[d, 0] = a
        bsum = bsum + head_sum(rr * kd)
    bonus_o[0] = bsum * v


def _rwkv_prep(pa, mu, w0, wb, a0, ab, gb, kkw, kaw, rkw, n_ctx_tiles):
    bz, t, _ = pa.shape
    n_tiles = t // ROW_TILE
    hb = ROW_TILE // GRID_W
    n_hblk = t // GRID_W
    ones_bd = jnp.kron(jnp.eye(A_HEADS, dtype=F32), jnp.ones((A_HEAD_DIM, A_HEAD_DIM), F32)).astype(BF16)
    row = lambda a: a.reshape(1, -1)
    tok = pl.BlockSpec((1, ROW_TILE, A_W), lambda b, i: (b, i, 0))
    tok2 = pl.BlockSpec((2, 1, ROW_TILE, A_W), lambda b, i: (0, b, i, 0))
    full = lambda a: pl.BlockSpec(a.shape, lambda b, i: (0,) * a.ndim)
    args = (row(mu), w0.reshape(2, 1, A_W), wb.astype(BF16), a0.reshape(2, 1, A_W), ab.astype(BF16),
            gb.astype(BF16), row(kkw), row(kaw), row(rkw), ones_bd)
    s1 = jax.ShapeDtypeStruct((bz, t, A_W), F32)
    s2 = jax.ShapeDtypeStruct((2, bz, t, A_W), F32)
    return pl.pallas_call(
        functools.partial(_rwkv_prep_kernel, n_ctx_tiles=n_ctx_tiles, n_tiles=n_tiles),
        grid=(bz, n_tiles),
        in_specs=[pl.BlockSpec((1, ROW_TILE, A_IN), lambda b, i: (b, i, 0)),
                  pl.BlockSpec((1, GRID_W, A_IN), lambda b, i: (b, jnp.maximum(i * hb - 1, 0), 0)),
                  pl.BlockSpec((1, GRID_W, A_IN), lambda b, i: (b, jnp.minimum((i + 1) * hb, n_hblk - 1), 0))]
                 + [full(a) for a in args],
        out_specs=[tok, tok, tok, tok, tok, tok2, tok2, tok2],
        out_shape=[s1, s1, s1, s1, s1, s2, s2, s2],
        compiler_params=_params("parallel", "parallel"),
        name="rwkv_prep",
    )(pa, pa, pa, *args)


def _rwkv_scan_kernel(r_ref, lw_ref, kd_ref, v_ref, kk_ref, a_ref, o_ref, s_ref):
    d = pl.program_id(1)
    i = pl.program_id(2)

    @pl.when(i == 0)
    def _():
        s_ref[...] = jnp.zeros_like(s_ref)

    r = r_ref[0]
    lw = lw_ref[0, 0]
    kd = kd_ref[0, 0]
    v = v_ref[0]
    kk = kk_ref[0]
    b = kk * a_ref[0, 0]
    n = r.shape[0]
    ti = lax.broadcasted_iota(jnp.int32, (n, n), 0)
    tj = lax.broadcasted_iota(jnp.int32, (n, n), 1)
    lag = jnp.where(d == 0, ti - tj, tj - ti)
    before = lag > 0
    before_eq = lag >= 0
    eye = (ti == tj).astype(F32)

    c = _dot_exact_lhs(before_eq.astype(F32), lw)
    ctot = jnp.sum(lw, axis=0, keepdims=True)
    e_c = jnp.exp(c)
    e_nc = jnp.exp(-c)
    e_tot = jnp.exp(ctot - c)
    rt = r * e_c
    at = kk * jnp.exp(c - lw)
    kt = kd * e_nc
    bt = b * e_nc
    kh = kd * e_tot
    bh = b * e_tot
    e_ctot = jnp.exp(ctot)

    for h in range(A_HEADS):
        sl = slice(h * A_HEAD_DIM, (h + 1) * A_HEAD_DIM)
        ar = jnp.concatenate([at[:, sl], rt[:, sl]], axis=0)
        kb = jnp.concatenate([kt[:, sl], bt[:, sl]], axis=0)
        m = _dot_nt_x3(ar, kb)
        a_ak = jnp.where(before, m[:n, :n], 0.0)
        a_ab = jnp.where(before, m[:n, n:], 0.0)
        a_rk = jnp.where(before_eq, m[n:, :n], 0.0)
        a_rb = jnp.where(before_eq, m[n:, n:], 0.0)
        pw = -a_ab
        tinv = eye + pw
        for _ in range(int(math.log2(n)) - 1):
            pw = _dot_x3(pw, pw)
            tinv = tinv + _dot_x3(tinv, pw)
        sv = s_ref[h]
        a_s = _dot_nt_x3(ar, sv)
        vh = v[:, sl]
        u = _dot_x3(tinv, a_s[:n] + _dot_x3(a_ak, vh))
        out = a_s[n:] + _dot_x3(a_rk, vh) - _dot_x3(a_rb, u)
        s_new = sv * e_ctot[:, sl]
        ah, al = _split(vh)
        bh_hi, bh_lo = _split(kh[:, sl])
        dn = (((0,), (0,)), ((), ()))
        s_new += lax.dot_general(ah, bh_hi, dn, preferred_element_type=F32)
        s_new += lax.dot_general(ah, bh_lo, dn, preferred_element_type=F32)
        s_new += lax.dot_general(al, bh_hi, dn, preferred_element_type=F32)
        uh, ul = _split(u)
        ch, cl = _split(bh[:, sl])
        s_new -= lax.dot_general(uh, ch, dn, preferred_element_type=F32)
        s_new -= lax.dot_general(uh, cl, dn, preferred_element_type=F32)
        s_new -= lax.dot_general(ul, ch, dn, preferred_element_type=F32)
        s_ref[h] = s_new
        o_ref[0, 0, :, sl] = out


def _rwkv_scan(r, lw, kd, v, kk, a, ctx_len):
    bz, t, _ = r.shape
    n_ctx = ctx_len // RWKV_CHUNK
    n_chunks = t // RWKV_CHUNK
    n_lat = n_chunks - n_ctx
    blk = lambda i, d: _seg_order(i, d, n_ctx, n_chunks)
    tok = pl.BlockSpec((1, RWKV_CHUNK, A_W), lambda b, d, i: (b, blk(i, d), 0))
    tok2 = pl.BlockSpec((1, 1, RWKV_CHUNK, A_W), lambda b, d, i: (d, b, blk(i, d), 0))
    del n_lat
    return pl.pallas_call(
        _rwkv_scan_kernel,
        grid=(bz, 2, n_chunks),
        in_specs=[tok, tok2, tok2, tok, tok, tok2],
        out_specs=tok2,
        out_shape=jax.ShapeDtypeStruct((2, bz, t, A_W), F32),
        scratch_shapes=[pltpu.VMEM((A_HEADS, A_HEAD_DIM, A_HEAD_DIM), F32)],
        compiler_params=_params("parallel", "parallel", "arbitrary"),
        name="rwkv_scan",
    )(r, lw, kd, v, kk, a)


def _rglru_kernel(p_ref, pv_ref, nx_ref, cw_ref, cb_ref, gw_ref, gbias_ref, lam_ref, o_ref,
                  a_scr, b_scr, h_scr, carry_ref, *, n_ctx_tiles, n_tiles):
    d = pl.program_id(1)
    i = pl.program_id(2)
    blk = _seg_order(i, d, n_ctx_tiles, n_tiles)

    @pl.when(i == 0)
    def _():
        carry_ref[...] = jnp.zeros_like(carry_ref)

    p = p_ref[0]
    tr = p.shape[0]
    x = p[:, :B_W]
    gate_in = p[:, B_W:]
    has_prev = jnp.logical_and(blk != 0, blk != n_ctx_tiles)
    has_next = jnp.logical_and(blk != n_ctx_tiles - 1, blk != n_tiles - 1)
    pv = jnp.where(has_prev, pv_ref[0][:, :B_W], 0.0)
    nx = jnp.where(has_next, nx_ref[0][:, :B_W], 0.0)
    row = lax.broadcasted_iota(jnp.int32, (tr, B_W), 0)
    pv_m1 = pv[7:8]
    pv_m2 = pv[6:7]
    nx_p1 = nx[0:1]
    x_m1 = jnp.where(row == 0, pv_m1, pltpu.roll(x, 1, 0))
    x_m2 = jnp.where(row == 0, pv_m2, jnp.where(row == 1, pv_m1, pltpu.roll(x, 2, 0)))
    x_p1 = jnp.where(row == tr - 1, nx_p1, pltpu.roll(x, tr - 1, 0))
    cw = cw_ref[...]
    u = cw[0:1] * x_m2 + cw[1:2] * x_m1 + cw[2:3] * x + cw[3:4] * x_p1 + cb_ref[...]
    gate = 0.5 * gate_in * (1.0 + jnp.tanh(math.sqrt(2.0 / math.pi) * (gate_in + 0.044715 * gate_in ** 3)))
    pre = _dot(u, gw_ref[0]) + gbias_ref[0]
    rec = _sigmoid(pre[:, :B_W])
    inp = _sigmoid(pre[:, B_W:])
    log_a = -B_C * rec * _softplus(-lam_ref[0])
    a_scr[...] = jnp.exp(log_a)
    b_scr[...] = jnp.sqrt(1.0 - jnp.exp(2.0 * log_a)) * inp * u

    n_grp = tr // 8
    g8 = lax.broadcasted_iota(jnp.int32, (8, B_W), 0)

    def group_scan(a, b, fwd):
        for s in (1, 2, 4):
            if fwd:
                ident = g8 < s
                a_p = pltpu.roll(a, s, 0)
                b_p = pltpu.roll(b, s, 0)
            else:
                ident = g8 >= 8 - s
                a_p = pltpu.roll(a, 8 - s, 0)
                b_p = pltpu.roll(b, 8 - s, 0)
            a_p = jnp.where(ident, 1.0, a_p)
            b_p = jnp.where(ident, 0.0, b_p)
            b = a * b_p + b
            a = a * a_p
        return a, b

    def run(fwd):
        def body(j, carry):
            g = j if fwd else n_grp - 1 - j
            off = pl.multiple_of(g * 8, 8)
            a, b = group_scan(a_scr[pl.ds(off, 8), :], b_scr[pl.ds(off, 8), :], fwd)
            h = b + a * carry
            h_scr[pl.ds(off, 8), :] = h
            return h[7:8] if fwd else h[0:1]
        carry_ref[...] = lax.fori_loop(0, n_grp, body, carry_ref[...])

    @pl.when(d == 0)
    def _():
        run(True)

    @pl.when(d == 1)
    def _():
        run(False)

    o_ref[0, 0] = h_scr[...] * gate


def _rglru(pb, conv_w, conv_b, gate_w, gate_b, lam, n_ctx_tiles):
    bz, t, _ = pb.shape
    n_tiles = t // ROW_TILE
    r8 = ROW_TILE // 8
    n8 = t // 8
    eye = jnp.eye(B_BLOCKS, dtype=F32)
    gw = jnp.einsum('dsgio,gh->dgisho', gate_w, eye).reshape(2, B_W, 2 * B_W).astype(BF16)
    gbias = gate_b.reshape(2, 1, 2 * B_W)
    blk = lambda i, d: _seg_order(i, d, n_ctx_tiles, n_tiles)
    return pl.pallas_call(
        functools.partial(_rglru_kernel, n_ctx_tiles=n_ctx_tiles, n_tiles=n_tiles),
        grid=(bz, 2, n_tiles),
        in_specs=[pl.BlockSpec((1, ROW_TILE, B_IN), lambda b, d, i: (b, blk(i, d), 0)),
                  pl.BlockSpec((1, 8, B_IN), lambda b, d, i: (b, jnp.maximum(blk(i, d) * r8 - 1, 0), 0)),
                  pl.BlockSpec((1, 8, B_IN), lambda b, d, i: (b, jnp.minimum((blk(i, d) + 1) * r8, n8 - 1), 0)),
                  pl.BlockSpec((4, B_W), lambda b, d, i: (0, 0)),
                  pl.BlockSpec((1, B_W), lambda b, d, i: (0, 0)),
                  pl.BlockSpec((1, B_W, 2 * B_W), lambda b, d, i: (d, 0, 0)),
                  pl.BlockSpec((1, 1, 2 * B_W), lambda b, d, i: (d, 0, 0)),
                  pl.BlockSpec((1, 1, B_W), lambda b, d, i: (d, 0, 0))],
        out_specs=pl.BlockSpec((1, 1, ROW_TILE, B_W), lambda b, d, i: (d, b, blk(i, d), 0)),
        out_shape=jax.ShapeDtypeStruct((2, bz, t, B_W), F32),
        scratch_shapes=[pltpu.VMEM((ROW_TILE, B_W), F32), pltpu.VMEM((ROW_TILE, B_W), F32),
                        pltpu.VMEM((ROW_TILE, B_W), F32), pltpu.VMEM((1, B_W), F32)],
        compiler_params=_params("parallel", "parallel", "arbitrary"),
        name="rglru",
    )(pb, pb, pb, conv_w, conv_b.reshape(1, B_W), gw, gbias, lam.reshape(2, 1, B_W))


def _retention_kernel(lg_ref, p_ref, cos_ref, sin_ref, o_ref, st_ref):
    d = pl.program_id(1)
    i = pl.program_id(2)

    @pl.when(i == 0)
    def _():
        st_ref[...] = jnp.zeros_like(st_ref)

    n = p_ref.shape[1]
    cos2 = cos_ref[...]
    sin2 = sin_ref[...]
    ti = lax.broadcasted_iota(jnp.int32, (n, n), 0)
    tj = lax.broadcasted_iota(jnp.int32, (n, n), 1)
    fwd = d == 0
    diff = jnp.where(fwd, ti - tj, tj - ti).astype(F32)
    t1 = lax.broadcasted_iota(jnp.int32, (n, 1), 0)
    pos = jnp.where(fwd, t1, n - 1 - t1).astype(F32)
    for h in range(C_HEADS):
        lg = lg_ref[d, h] * jnp.ones((1, 1), F32)
        sl = slice(h * C_HEAD_DIM, (h + 1) * C_HEAD_DIM)
        q = p_ref[0, :, sl]
        k = p_ref[0, :, C_W + h * C_HEAD_DIM:C_W + (h + 1) * C_HEAD_DIM]
        v = p_ref[0, :, 2 * C_W + h * C_HEAD_DIM:2 * C_W + (h + 1) * C_HEAD_DIM]
        q = q * cos2 + pltpu.roll(q, C_HEAD_DIM // 2, 1) * sin2
        k = (k * cos2 + pltpu.roll(k, C_HEAD_DIM // 2, 1) * sin2) * (C_HEAD_DIM ** -0.5)
        decay = jnp.where(diff >= 0, jnp.exp(lg * jnp.maximum(diff, 0.0)), 0.0)
        scores = _dot_nt(q, k) * decay
        st = st_ref[h]
        out = _dot(scores, v) + _dot(q, st) * jnp.exp(lg * (pos + 1.0))
        zeta = jnp.exp(lg * (n - 1.0 - pos))
        st_ref[h] = st * jnp.exp(lg * n) + _dot_tn(k * zeta, v)
        o_ref[0, 0, :, sl] = out


def _retention(pc, log_gamma, cos2, sin2, ctx_len):
    bz, t, _ = pc.shape
    n_ctx = ctx_len // ODD_CHUNK
    n_chunks = t // ODD_CHUNK
    blk = lambda i, d: _seg_order(i, d, n_ctx, n_chunks)
    return pl.pallas_call(
        _retention_kernel,
        grid_spec=pltpu.PrefetchScalarGridSpec(
            num_scalar_prefetch=1,
            grid=(bz, 2, n_chunks),
            in_specs=[pl.BlockSpec((1, ODD_CHUNK, 3 * C_W), lambda b, d, i, lg: (b, blk(i, d), 0)),
                      pl.BlockSpec((ODD_CHUNK, C_HEAD_DIM), lambda b, d, i, lg: (blk(i, d), 0)),
                      pl.BlockSpec((ODD_CHUNK, C_HEAD_DIM), lambda b, d, i, lg: (blk(i, d), 0))],
            out_specs=pl.BlockSpec((1, 1, ODD_CHUNK, C_W), lambda b, d, i, lg: (d, b, blk(i, d), 0)),
            scratch_shapes=[pltpu.VMEM((C_HEADS, C_HEAD_DIM, C_HEAD_DIM), F32)]),
        out_shape=jax.ShapeDtypeStruct((2, bz, t, C_W), F32),
        compiler_params=_params("parallel", "parallel", "arbitrary"),
        name="retention",
    )(log_gamma, pc, cos2, sin2)


def _mlstm_kernel(p_ref, bias_ref, o_ref, c_ref, n_ref, m_ref):
    d = pl.program_id(1)
    i = pl.program_id(2)

    @pl.when(i == 0)
    def _():
        c_ref[...] = jnp.zeros_like(c_ref)
        n_ref[...] = jnp.zeros_like(n_ref)
        m_ref[...] = jnp.full_like(m_ref, -jnp.inf)

    n = p_ref.shape[1]
    ti = lax.broadcasted_iota(jnp.int32, (n, n), 0)
    tj = lax.broadcasted_iota(jnp.int32, (n, n), 1)
    fwd = d == 0
    before_eq = jnp.where(fwd, ti - tj, tj - ti) >= 0
    gts = p_ref[0, :, 4 * D_W:] + bias_ref[...]
    gts = jnp.where(fwd, gts, pltpu.roll(gts, LANES - D_HEADS, 1))
    lf = jnp.minimum(gts, 0.0) - jnp.log(1.0 + jnp.exp(-jnp.abs(gts)))
    bcum = _dot_exact_lhs(before_eq.astype(F32), lf)
    btot = jnp.sum(lf, axis=0, keepdims=True)
    gts_t = gts.T
    bcum_t = bcum.T
    for h in range(D_HEADS):
        sl = slice(h * D_HEAD_DIM, (h + 1) * D_HEAD_DIM)
        q = p_ref[0, :, sl]
        k = p_ref[0, :, D_W + h * D_HEAD_DIM:D_W + (h + 1) * D_HEAD_DIM] * (D_HEAD_DIM ** -0.5)
        v = p_ref[0, :, 2 * D_W + h * D_HEAD_DIM:2 * D_W + (h + 1) * D_HEAD_DIM]
        ig_col = gts[:, h:h + 1]
        ig_row = gts_t[h:h + 1, :]
        b_col = bcum[:, 2 * D_HEADS + h:2 * D_HEADS + h + 1]
        b_row = bcum_t[2 * D_HEADS + h:2 * D_HEADS + h + 1, :]
        b_end = btot[:, 2 * D_HEADS + h:2 * D_HEADS + h + 1]
        m_prev = m_ref[h][0:1, 0:1]
        d_log = jnp.where(before_eq, b_col - b_row + ig_row, -jnp.inf)
        g_inter = b_col + m_prev
        m_t = jnp.maximum(g_inter, jnp.max(d_log, axis=1, keepdims=True))
        s_intra = _dot_nt(q, k) * jnp.exp(d_log - m_t)
        s_inter = jnp.exp(g_inter - m_t)
        num = _dot(s_intra, v) + _dot(q, c_ref[h]) * s_inter
        den = jnp.sum(s_intra, axis=1, keepdims=True) + jnp.sum(q * n_ref[h], axis=1, keepdims=True) * s_inter
        den = jnp.maximum(jnp.abs(den), jnp.exp(-m_t))
        o_ref[0, 0, :, sl] = num / den
        w_end = b_end - b_col + ig_col
        m_i = jnp.max(w_end, axis=0, keepdims=True)
        e_end = jnp.exp(w_end - m_i)
        ke = k * e_end
        m_new = jnp.maximum(b_end + m_prev, m_i)
        s_old = jnp.exp(b_end + m_prev - m_new)
        s_new = jnp.exp(m_i - m_new)
        c_ref[h] = c_ref[h] * s_old + _dot_tn(ke, v) * s_new
        n_ref[h] = n_ref[h] * s_old + jnp.sum(ke, axis=0, keepdims=True) * s_new
        m_ref[h] = jnp.broadcast_to(m_new, m_ref.shape[1:])


def _mlstm(pd, bias128, ctx_len):
    bz, t, w = pd.shape
    n_ctx = ctx_len // ODD_CHUNK
    n_chunks = t // ODD_CHUNK
    blk = lambda i, d: _seg_order(i, d, n_ctx, n_chunks)
    return pl.pallas_call(
        _mlstm_kernel,
        grid=(bz, 2, n_chunks),
        in_specs=[pl.BlockSpec((1, ODD_CHUNK, w), lambda b, d, i: (b, blk(i, d), 0)),
                  pl.BlockSpec((1, LANES), lambda b, d, i: (0, 0))],
        out_specs=pl.BlockSpec((1, 1, ODD_CHUNK, D_W), lambda b, d, i: (d, b, blk(i, d), 0)),
        out_shape=jax.ShapeDtypeStruct((2, bz, t, D_W), F32),
        scratch_shapes=[pltpu.VMEM((D_HEADS, D_HEAD_DIM, D_HEAD_DIM), F32),
                        pltpu.VMEM((D_HEADS, 1, D_HEAD_DIM), F32),
                        pltpu.VMEM((D_HEADS, 8, LANES), F32)],
        compiler_params=_params("parallel", "parallel", "arbitrary"),
        name="mlstm",
    )(pd, bias128)


def _group_norm_lanes(x, n_groups, eps):
    gw = x.shape[1] // n_groups
    outs = []
    for g in range(n_groups):
        xg = x[:, g * gw:(g + 1) * gw]
        mu = jnp.mean(xg, axis=1, keepdims=True)
        xc = xg - mu
        var = jnp.mean(xc * xc, axis=1, keepdims=True)
        outs.append(xc * lax.rsqrt(var + eps))
    return jnp.concatenate(outs, axis=1)


def _post_mixer(ycat, x_ref, wout_ref, mods_ref, lng_ref, lnb_ref, rw_ref, rb_ref, x1_o, h2_o, gates_o, alpha):
    y = jnp.dot(ycat.astype(BF16), wout_ref[...], preferred_element_type=F32)
    mods = mods_ref[0, 0]
    x1 = alpha * x_ref[0] + mods[2:3] * y
    mu = jnp.mean(x1, axis=1, keepdims=True)
    xc = x1 - mu
    var = jnp.mean(xc * xc, axis=1, keepdims=True)
    x1 = xc * lax.rsqrt(var + LN_EPS) * lng_ref[...] + lnb_ref[...]
    x1_o[0] = x1
    h2 = x1 * (1.0 + mods[4:5]) + mods[3:4]
    h2_o[0] = h2.astype(BF16)
    logits = _dot_x3(h2, rw_ref[...]) + rb_ref[...]
    lane = lax.broadcasted_iota(jnp.int32, logits.shape, 1)
    logits = jnp.where(lane < N_EXPERTS, logits, -jnp.inf)
    work = logits
    sel = lane < 0
    top = None
    for kth in range(TOP_K):
        mk = jnp.max(work, axis=1, keepdims=True)
        idx = jnp.min(jnp.where(work == mk, lane, LANES), axis=1, keepdims=True)
        pick = lane == idx
        sel = jnp.logical_or(sel, pick)
        work = jnp.where(pick, -jnp.inf, work)
        if kth == 0:
            top = mk
    e = jnp.where(sel, jnp.exp(logits - top), 0.0)
    gates_o[0] = e / jnp.sum(e, axis=1, keepdims=True)


def _outproj_even_kernel(wkv_ref, bonus_ref, g_ref, gng_ref, gnb_ref, hb_ref,
                         x_ref, wout_ref, mods_ref, lng_ref, lnb_ref, rw_ref, rb_ref,
                         x1_o, h2_o, gates_o, *, alpha):
    wkv = wkv_ref[0, 0] + wkv_ref[1, 0]
    ya = (_group_norm_lanes(wkv, A_HEADS, GN_EPS_RWKV) * gng_ref[...] + gnb_ref[...] + bonus_ref[0]) * g_ref[0]
    yb = hb_ref[0, 0] + hb_ref[1, 0]
    ycat = jnp.concatenate([ya, yb], axis=1)
    _post_mixer(ycat, x_ref, wout_ref, mods_ref, lng_ref, lnb_ref, rw_ref, rb_ref, x1_o, h2_o, gates_o, alpha)


def _outproj_odd_kernel(oc_ref, pc_ref, cg_ref, cb_ref, od_ref, pd_ref, dg_ref, db_ref,
                        x_ref, wout_ref, mods_ref, lng_ref, lnb_ref, rw_ref, rb_ref,
                        x1_o, h2_o, gates_o, *, alpha):
    oc = oc_ref[0, 0] + oc_ref[1, 0]
    gc = pc_ref[0]
    ya = (_group_norm_lanes(oc, C_HEADS, LN_EPS) * cg_ref[...] + cb_ref[...]) * (gc * _sigmoid(gc))
    od = od_ref[0, 0] + od_ref[1, 0]
    yb = (_group_norm_lanes(od, D_HEADS, LN_EPS) * dg_ref[...] + db_ref[...]) * _sigmoid(pd_ref[0])
    ycat = jnp.concatenate([ya, yb], axis=1)
    _post_mixer(ycat, x_ref, wout_ref, mods_ref, lng_ref, lnb_ref, rw_ref, rb_ref, x1_o, h2_o, gates_o, alpha)


def _outproj(kind, mixer_args, mixer_specs, x, wout, mods, ln_g, ln_b, rw, rb, n_ctx_tiles, alpha):
    bz, t, d = x.shape
    seg = lambda i: (i >= n_ctx_tiles).astype(jnp.int32)
    row = lambda a: a.reshape(1, -1)
    vec = lambda n: pl.BlockSpec((1, n), lambda b, i: (0, 0))
    kern = _outproj_even_kernel if kind == 'even' else _outproj_odd_kernel
    tok = lambda n: pl.BlockSpec((1, ROW_TILE, n), lambda b, i: (b, i, 0))
    return pl.pallas_call(
        functools.partial(kern, alpha=alpha),
        grid=(bz, t // ROW_TILE),
        in_specs=list(mixer_specs) + [
            tok(d),
            pl.BlockSpec((d, d), lambda b, i: (0, 0)),
            pl.BlockSpec((1, 1, 8, d), lambda b, i: (b, seg(i), 0, 0)),
            vec(d), vec(d),
            pl.BlockSpec((d, LANES), lambda b, i: (0, 0)),
            vec(LANES)],
        out_specs=[tok(d), tok(d), tok(LANES)],
        out_shape=[jax.ShapeDtypeStruct((bz, t, d), F32),
                   jax.ShapeDtypeStruct((bz, t, d), BF16),
                   jax.ShapeDtypeStruct((bz, t, LANES), F32)],
        compiler_params=_params("parallel", "parallel"),
        name="outproj_" + kind,
    )(*mixer_args, x, wout, mods, row(ln_g), row(ln_b), rw, rb)


def _moe_kernel(h_ref, gates_ref, w1_ref, b1_ref, w2_ref, b2_ref, o_ref, *, f_chunk):
    e = pl.program_id(1)

    @pl.when(e == 0)
    def _():
        o_ref[...] = jnp.zeros_like(o_ref)

    h = h_ref[...]
    gates = gates_ref[...]
    lane = lax.broadcasted_iota(jnp.int32, gates.shape, 1)
    ge = jnp.sum(jnp.where(lane == e, gates, 0.0), axis=1, keepdims=True)
    acc = jnp.zeros(o_ref.shape, F32)
    for c in range(D_FF // f_chunk):
        lo = c * f_chunk
        glu = jnp.dot(h, w1_ref[0, :, lo:lo + f_chunk], preferred_element_type=F32) + b1_ref[0, :, lo:lo + f_chunk]
        lin = (jnp.dot(h, w1_ref[0, :, D_FF + lo:D_FF + lo + f_chunk], preferred_element_type=F32)
               + b1_ref[0, :, D_FF + lo:D_FF + lo + f_chunk])
        glu = jnp.minimum(glu, SWIGLU_LIMIT)
        lin = jnp.clip(lin, -SWIGLU_LIMIT, SWIGLU_LIMIT)
        act = (lin + 1.0) * glu * _sigmoid(SWIGLU_ALPHA * glu)
        acc += jnp.dot(act.astype(BF16), w2_ref[0, lo:lo + f_chunk, :], preferred_element_type=F32)
    o_ref[...] += ge * (acc + b2_ref[0])


def _moe(h2, gates, w1, b1, w2, b2):
    n, d = h2.shape
    n_e = w1.shape[0]
    return pl.pallas_call(
        functools.partial(_moe_kernel, f_chunk=512),
        grid=(n // MOE_TILE, n_e),
        in_specs=[pl.BlockSpec((MOE_TILE, d), lambda i, e: (i, 0)),
                  pl.BlockSpec((MOE_TILE, LANES), lambda i, e: (i, 0)),
                  pl.BlockSpec((1, d, 2 * D_FF), lambda i, e: (e, 0, 0)),
                  pl.BlockSpec((1, 1, 2 * D_FF), lambda i, e: (e, 0, 0)),
                  pl.BlockSpec((1, D_FF, d), lambda i, e: (e, 0, 0)),
                  pl.BlockSpec((1, 1, d), lambda i, e: (e, 0, 0))],
        out_specs=pl.BlockSpec((MOE_TILE, d), lambda i, e: (i, 0)),
        out_shape=jax.ShapeDtypeStruct((n, d), F32),
        compiler_params=_params("parallel", "arbitrary"),
        name="moe",
    )(h2, gates, w1, b1.reshape(n_e, 1, -1), w2, b2.reshape(n_e, 1, -1))


def _ffn_norm_kernel(x_ref, f_ref, mods_ref, lng_ref, lnb_ref, o_ref, *, alpha):
    mods = mods_ref[0, 0]
    x2 = alpha * x_ref[0] + mods[5:6] * f_ref[0]
    mu = jnp.mean(x2, axis=1, keepdims=True)
    xc = x2 - mu
    var = jnp.mean(xc * xc, axis=1, keepdims=True)
    o_ref[0] = xc * lax.rsqrt(var + LN_EPS) * lng_ref[...] + lnb_ref[...]


def _ffn_norm(x1, ffn, mods, ln_g, ln_b, n_ctx_tiles, alpha):
    bz, t, d = x1.shape
    seg = lambda i: (i >= n_ctx_tiles).astype(jnp.int32)
    tok = pl.BlockSpec((1, ROW_TILE, d), lambda b, i: (b, i, 0))
    vec = pl.BlockSpec((1, d), lambda b, i: (0, 0))
    return pl.pallas_call(
        functools.partial(_ffn_norm_kernel, alpha=alpha),
        grid=(bz, t // ROW_TILE),
        in_specs=[tok, tok, pl.BlockSpec((1, 1, 8, d), lambda b, i: (b, seg(i), 0, 0)), vec, vec],
        out_specs=tok,
        out_shape=jax.ShapeDtypeStruct((bz, t, d), F32),
        compiler_params=_params("parallel", "parallel"),
        name="ffn_norm",
    )(x1, ffn, mods, ln_g.reshape(1, d), ln_b.reshape(1, d))


def _rope_tables(ctx_len, seq):
    idx = jnp.arange(seq)
    rowp = (idx // GRID_W).astype(F32)
    colp = (idx % GRID_W).astype(F32)
    n_freq = C_HEAD_DIM // 4
    inv = ROPE_BASE ** (-jnp.arange(n_freq, dtype=F32) / n_freq)
    ang = jnp.concatenate([rowp[:, None] * inv, colp[:, None] * inv], -1)
    cos, sin = jnp.cos(ang), jnp.sin(ang)
    cos2 = jnp.concatenate([cos, cos], -1)
    sin2 = jnp.concatenate([-sin, sin], -1)
    cos2 = jnp.concatenate([jnp.ones((ctx_len, C_HEAD_DIM), F32), cos2], 0)
    sin2 = jnp.concatenate([jnp.zeros((ctx_len, C_HEAD_DIM), F32), sin2], 0)
    return cos2, sin2


def kernel(x, c, ctx, c_ctx, w_mod, b_mod, ln_g, ln_b, even_w_in, even_w_out, a_mu, a_w0, a_wB, a_a0, a_aB, a_gB, a_kk, a_ka, a_rk, a_gn_g, a_gn_b, b_conv_w, b_conv_b, b_gate_w, b_gate_b, b_lam, odd_w_in, odd_w_out, c_log_gamma, c_gn_g, c_gn_b, d_ibias, d_fbias, d_gn_g, d_gn_b, router_w, router_b, exp_w1, exp_b1, exp_w2, exp_b2):
    bz, seq, d = x.shape
    ctx_len = ctx.shape[1]
    depth = w_mod.shape[0]
    assert ctx_len == ROW_TILE and seq % ROW_TILE == 0
    t = ctx_len + seq
    n_ctx_tiles = ctx_len // ROW_TILE
    alpha = (2 * depth) ** 0.25
    xs = jnp.concatenate([ctx, x], axis=1)
    c8 = jnp.concatenate([c, c_ctx[None], jnp.zeros((8 - bz - 1, d), F32)], 0)
    cos2, sin2 = _rope_tables(ctx_len, seq)
    row = lambda a: a.reshape(1, -1)
    tok = lambda n: pl.BlockSpec((1, ROW_TILE, n), lambda b, i: (b, i, 0))
    tok2 = lambda n: pl.BlockSpec((2, 1, ROW_TILE, n), lambda b, i: (0, b, i, 0))
    vec = lambda n: pl.BlockSpec((1, n), lambda b, i: (0, 0))

    for layer in range(depth):
        m = _mod_vectors(c8, w_mod[layer], b_mod[layer]).reshape(8, N_MOD, d)
        m_lat = m[:bz]
        m_ctx = jnp.broadcast_to(m[bz][None], (bz, N_MOD, d))
        mods = jnp.stack([m_ctx, m_lat], axis=1)
        mods = jnp.concatenate([mods, jnp.zeros((bz, 2, 8 - N_MOD, d), F32)], axis=2)
        shift = mods[:, :, 0:1]
        scale = mods[:, :, 1:2]
        rw = jnp.pad(router_w[layer], ((0, 0), (0, LANES - N_EXPERTS)))
        rb = jnp.pad(router_b[layer], (0, LANES - N_EXPERTS)).reshape(1, LANES)
        li = layer // 2
        if layer % 2 == 0:
            pa, pb = _inproj(xs, shift, scale, even_w_in[li].astype(BF16), A_IN, n_ctx_tiles)
            r, v, kk, g, bonus, lw, kd, a = _rwkv_prep(pa, a_mu[li], a_w0[li], a_wB[li], a_a0[li], a_aB[li],
                                                       a_gB[li], a_kk[li], a_ka[li], a_rk[li].reshape(-1),
                                                       n_ctx_tiles)
            wkv = _rwkv_scan(r, lw, kd, v, kk, a, ctx_len)
            hb = _rglru(pb, b_conv_w[li], b_conv_b[li], b_gate_w[li], b_gate_b[li], b_lam[li], n_ctx_tiles)
            mixer_args = (wkv, bonus, g, row(a_gn_g[li]), row(a_gn_b[li]), hb)
            mixer_specs = (tok2(A_W), tok(A_W), tok(A_W), vec(A_W), vec(A_W), tok2(B_W))
            x1, h2, gates = _outproj('even', mixer_args, mixer_specs, xs, even_w_out[li].astype(BF16), mods,
                                     ln_g[layer, 0], ln_b[layer, 0], rw, rb, n_ctx_tiles, alpha)
        else:
            w_in = jnp.pad(odd_w_in[li], ((0, 0), (0, D_IN_PAD - D_IN))).astype(BF16)
            pc, pd = _inproj(xs, shift, scale, w_in, C_IN, n_ctx_tiles)
            oc = _retention(pc, c_log_gamma[li], cos2, sin2, ctx_len)
            bias128 = jnp.concatenate([d_ibias[li].reshape(-1), d_fbias[li].reshape(-1),
                                       jnp.zeros((LANES - 4 * D_HEADS,), F32)]).reshape(1, LANES)
            od = _mlstm(pd, bias128, ctx_len)
            gate_c = pl.BlockSpec((1, ROW_TILE, C_W), lambda b, i: (b, i, 3))
            gate_d = pl.BlockSpec((1, ROW_TILE, D_W), lambda b, i: (b, i, 3))
            mixer_args = (oc, pc, row(c_gn_g[li]), row(c_gn_b[li]), od, pd, row(d_gn_g[li]), row(d_gn_b[li]))
            mixer_specs = (tok2(C_W), gate_c, vec(C_W), vec(C_W), tok2(D_W), gate_d, vec(D_W), vec(D_W))
            x1, h2, gates = _outproj('odd', mixer_args, mixer_specs, xs, odd_w_out[li].astype(BF16), mods,
                                     ln_g[layer, 0], ln_b[layer, 0], rw, rb, n_ctx_tiles, alpha)
        ffn = _moe(h2.reshape(bz * t, d), gates.reshape(bz * t, LANES), exp_w1[layer].astype(BF16), exp_b1[layer],
                   exp_w2[layer].astype(BF16), exp_b2[layer]).reshape(bz, t, d)
        xs = _ffn_norm(x1, ffn, mods, ln_g[layer, 1], ln_b[layer, 1], n_ctx_tiles, alpha)
    return xs[:, ctx_len:]
```

```python
import functools
import math

import jax
import jax.numpy as jnp
from jax import lax
from jax.experimental import pallas as pl
from jax.experimental.pallas import tpu as pltpu

F32 = jnp.float32
BF16 = jnp.bfloat16

GRID_W = 64
A_HEADS = 8
A_HEAD_DIM = 64
A_W = 512
A_DECAY_RANK = 64
A_ICL_RANK = 64
A_GATE_RANK = 128
A_IN = 1920
A_DECAY_SCALE = math.exp(-0.5)
GN_EPS_RWKV = 64e-5
B_W = 512
B_BLOCKS = 8
B_C = 8.0
B_IN = 1024
C_HEADS = 4
C_HEAD_DIM = 128
C_W = 512
C_IN = 2048
ROPE_BASE = 10000.0
D_HEADS = 4
D_HEAD_DIM = 128
D_W = 512
D_IN = 2064
D_IN_PAD = 2176
N_EXPERTS = 32
TOP_K = 4
D_FF = 1024
SWIGLU_ALPHA = 1.702
SWIGLU_LIMIT = 7.0
LN_EPS = 1e-5
N_MOD = 6
LANES = 128

ROW_TILE = 256
RWKV_CHUNK = 64
ODD_CHUNK = 256
MOE_TILE = 1024
VMEM_LIMIT_BYTES = 56 * 1024 * 1024


def _params(*sem):
    return pltpu.CompilerParams(dimension_semantics=sem, vmem_limit_bytes=VMEM_LIMIT_BYTES)


def _dot(a, b):
    return jnp.dot(a.astype(BF16), b.astype(BF16), preferred_element_type=F32)


def _dot_nt(a, b):
    return lax.dot_general(a.astype(BF16), b.astype(BF16), (((1,), (1,)), ((), ())),
                           preferred_element_type=F32)


def _dot_tn(a, b):
    return lax.dot_general(a.astype(BF16), b.astype(BF16), (((0,), (0,)), ((), ())),
                           preferred_element_type=F32)


def _split(x):
    hi = x.astype(BF16)
    lo = (x - hi.astype(F32)).astype(BF16)
    return hi, lo


def _dot_x3(a, b):
    ah, al = _split(a)
    bh, bl = _split(b)
    out = jnp.dot(ah, bh, preferred_element_type=F32)
    out += jnp.dot(ah, bl, preferred_element_type=F32)
    out += jnp.dot(al, bh, preferred_element_type=F32)
    return out


def _dot_nt_x3(a, b):
    ah, al = _split(a)
    bh, bl = _split(b)
    dn = (((1,), (1,)), ((), ()))
    out = lax.dot_general(ah, bh, dn, preferred_element_type=F32)
    out += lax.dot_general(ah, bl, dn, preferred_element_type=F32)
    out += lax.dot_general(al, bh, dn, preferred_element_type=F32)
    return out


def _dot_exact_lhs(a, b):
    a = a.astype(BF16)
    b1 = b.astype(BF16)
    r1 = b - b1.astype(F32)
    b2 = r1.astype(BF16)
    b3 = (r1 - b2.astype(F32)).astype(BF16)
    out = jnp.dot(a, b1, preferred_element_type=F32)
    out += jnp.dot(a, b2, preferred_element_type=F32)
    out += jnp.dot(a, b3, preferred_element_type=F32)
    return out


def _sigmoid(x):
    return 1.0 / (1.0 + jnp.exp(-x))


def _softplus(x):
    return jnp.maximum(x, 0.0) + jnp.log(1.0 + jnp.exp(-jnp.abs(x)))


def _seg_order(i, d, n_ctx, n_tot):
    rev = jnp.where(i < n_ctx, n_ctx - 1 - i, n_ctx + n_tot - 1 - i)
    return jnp.where(d == 0, i, rev)


def _mod_kernel(c_ref, w_ref, b_ref, o_ref):
    c = c_ref[...]
    s = c * _sigmoid(c)
    o_ref[...] = _dot_x3(s, w_ref[...]) + b_ref[...]


def _mod_vectors(c8, w, b):
    d, n = w.shape
    tn = n // 4
    return pl.pallas_call(
        _mod_kernel,
        grid=(n // tn,),
        in_specs=[pl.BlockSpec((8, d), lambda j: (0, 0)),
                  pl.BlockSpec((d, tn), lambda j: (0, j)),
                  pl.BlockSpec((1, tn), lambda j: (0, j))],
        out_specs=pl.BlockSpec((8, tn), lambda j: (0, j)),
        out_shape=jax.ShapeDtypeStruct((8, n), F32),
        compiler_params=_params("parallel"),
        name="mod_vectors",
    )(c8, w, b.reshape(1, n))


def _inproj_kernel(x_ref, sh_ref, sc_ref, w_ref, oa_ref, ob_ref, *, n_a):
    h = x_ref[0] * (1.0 + sc_ref[0, 0]) + sh_ref[0, 0]
    p = jnp.dot(h.astype(BF16), w_ref[...], preferred_element_type=F32)
    oa_ref[0] = p[:, :n_a]
    ob_ref[0] = p[:, n_a:]


def _inproj(x, shift, scale, w_bf16, n_a, n_ctx_tiles):
    bz, t, d = x.shape
    n = w_bf16.shape[1]
    n_b = n - n_a
    seg = lambda i: (i >= n_ctx_tiles).astype(jnp.int32)
    return pl.pallas_call(
        functools.partial(_inproj_kernel, n_a=n_a),
        grid=(bz, t // ROW_TILE),
        in_specs=[pl.BlockSpec((1, ROW_TILE, d), lambda b, i: (b, i, 0)),
                  pl.BlockSpec((1, 1, 1, d), lambda b, i: (b, seg(i), 0, 0)),
                  pl.BlockSpec((1, 1, 1, d), lambda b, i: (b, seg(i), 0, 0)),
                  pl.BlockSpec((d, n), lambda b, i: (0, 0))],
        out_specs=[pl.BlockSpec((1, ROW_TILE, n_a), lambda b, i: (b, i, 0)),
                   pl.BlockSpec((1, ROW_TILE, n_b), lambda b, i: (b, i, 0))],
        out_shape=[jax.ShapeDtypeStruct((bz, t, n_a), F32),
                   jax.ShapeDtypeStruct((bz, t, n_b), F32)],
        compiler_params=_params("parallel", "parallel"),
        name="inproj",
    )(x, shift, scale, w_bf16)


def _rwkv_prep_kernel(p_ref, up_ref, dn_ref, mu_ref, w0_ref, wb_ref, a0_ref, ab_ref, gb_ref,
                      kkw_ref, kaw_ref, rkw_ref, ones_ref,
                      r_o, v_o, kk_o, g_o, bonus_o, lw_o, kd_o, a_o, *, n_ctx_tiles, n_tiles):
    i = pl.program_id(1)
    p = p_ref[0]
    tr, w = p.shape
    row = lax.broadcasted_iota(jnp.int32, (tr, w), 0)
    lane = lax.broadcasted_iota(jnp.int32, (tr, w), 1)
    prev = pltpu.roll(p, 1, 0)
    nxt = pltpu.roll(p, tr - 1, 0)
    sh_ctx = jnp.where(lane % 2 == 0, jnp.where(row == 0, 0.0, prev), jnp.where(row == tr - 1, 0.0, nxt))
    col = row % GRID_W
    left = jnp.where(col == 0, 0.0, prev)
    right = jnp.where(col == GRID_W - 1, 0.0, nxt)
    up_halo = jnp.where(i > n_ctx_tiles, up_ref[0], 0.0)
    dn_halo = jnp.where(i < n_tiles - 1, dn_ref[0], 0.0)
    up = jnp.concatenate([up_halo, p[:tr - GRID_W]], axis=0)
    down = jnp.concatenate([p[GRID_W:], dn_halo], axis=0)
    slot = lane % 4
    sh_lat = jnp.where(slot == 0, left, jnp.where(slot == 1, right, jnp.where(slot == 2, up, down)))
    sh = jnp.where(i < n_ctx_tiles, sh_ctx, sh_lat)
    z = p + (sh - p) * mu_ref[...]

    r = z[:, 0:A_W]
    k = z[:, A_W:2 * A_W]
    v = z[:, 2 * A_W:3 * A_W]
    o = 3 * A_W
    wd = z[:, o:o + 2 * A_DECAY_RANK]
    o += 2 * A_DECAY_RANK
    ad = z[:, o:o + 2 * A_ICL_RANK]
    o += 2 * A_ICL_RANK
    zg = z[:, o:o + A_GATE_RANK]

    ones_bd = ones_ref[...]

    def head_sum(x):
        hi, lo = _split(x)
        return (jnp.dot(hi, ones_bd, preferred_element_type=F32)
                + jnp.dot(lo, ones_bd, preferred_element_type=F32))

    g_o[0] = _dot(_sigmoid(zg), gb_ref[...])
    kk = k * kkw_ref[...]
    kk = kk / jnp.maximum(jnp.sqrt(head_sum(kk * kk)), 1e-12)
    r_o[0] = r
    v_o[0] = v
    kk_o[0] = kk
    rr = r * rkw_ref[...]
    bsum = jnp.zeros_like(r)
    for d in range(2):
        wdd = jnp.tanh(wd[:, d * A_DECAY_RANK:(d + 1) * A_DECAY_RANK])
        lw = -A_DECAY_SCALE * _sigmoid(w0_ref[d] + _dot(wdd, wb_ref[d]))
        a = _sigmoid(a0_ref[d] + _dot(ad[:, d * A_ICL_RANK:(d + 1) * A_ICL_RANK], ab_ref[d]))
        kd = k * (1.0 + (a - 1.0) * kaw_ref[...])
        lw_o[d, 0] = lw
        kd_o[d, 0] = kd
        a_o[d, 0] = a
        bsum = bsum + head_sum(rr * kd)
    bonus_o[0] = bsum * v


def _rwkv_prep(pa, mu, w0, wb, a0, ab, gb, kkw, kaw, rkw, n_ctx_tiles):
    bz, t, _ = pa.shape
    n_tiles = t // ROW_TILE
    hb = ROW_TILE // GRID_W
    n_hblk = t // GRID_W
    ones_bd = jnp.kron(jnp.eye(A_HEADS, dtype=F32), jnp.ones((A_HEAD_DIM, A_HEAD_DIM), F32)).astype(BF16)
    row = lambda a: a.reshape(1, -1)
    tok = pl.BlockSpec((1, ROW_TILE, A_W), lambda b, i: (b, i, 0))
    tok2 = pl.BlockSpec((2, 1, ROW_TILE, A_W), lambda b, i: (0, b, i, 0))
    full = lambda a: pl.BlockSpec(a.shape, lambda b, i: (0,) * a.ndim)
    args = (row(mu), w0.reshape(2, 1, A_W), wb.astype(BF16), a0.reshape(2, 1, A_W), ab.astype(BF16),
            gb.astype(BF16), row(kkw), row(kaw), row(rkw), ones_bd)
    s1 = jax.ShapeDtypeStruct((bz, t, A_W), F32)
    s2 = jax.ShapeDtypeStruct((2, bz, t, A_W), F32)
    return pl.pallas_call(
        functools.partial(_rwkv_prep_kernel, n_ctx_tiles=n_ctx_tiles, n_tiles=n_tiles),
        grid=(bz, n_tiles),
        in_specs=[pl.BlockSpec((1, ROW_TILE, A_IN), lambda b, i: (b, i, 0)),
                  pl.BlockSpec((1, GRID_W, A_IN), lambda b, i: (b, jnp.maximum(i * hb - 1, 0), 0)),
                  pl.BlockSpec((1, GRID_W, A_IN), lambda b, i: (b, jnp.minimum((i + 1) * hb, n_hblk - 1), 0))]
                 + [full(a) for a in args],
        out_specs=[tok, tok, tok, tok, tok, tok2, tok2, tok2],
        out_shape=[s1, s1, s1, s1, s1, s2, s2, s2],
        compiler_params=_params("parallel", "parallel"),
        name="rwkv_prep",
    )(pa, pa, pa, *args)


_BMM_NN = (((2,), (1,)), ((0,), (0,)))
_BMM_NT = (((2,), (2,)), ((0,), (0,)))
_BMM_TN = (((1,), (1,)), ((0,), (0,)))
RWKV_CHAINS = 2 * A_HEADS


def _bmm(a, b, dims):
    return lax.dot_general(a.astype(BF16), b.astype(BF16), dims, preferred_element_type=F32)


def _bmm_x3(a, b, dims):
    ah, al = _split(a)
    bh, bl = _split(b)
    rows = a.shape[1]
    p = lax.dot_general(jnp.concatenate([ah, al], axis=1), bh, dims, preferred_element_type=F32)
    return p[:, :rows] + p[:, rows:] + lax.dot_general(ah, bl, dims, preferred_element_type=F32)


def _split_heads(x):
    return jnp.stack([x[:, h * A_HEAD_DIM:(h + 1) * A_HEAD_DIM] for h in range(A_HEADS)], axis=0)


def _rwkv_scan_kernel(rf, lwf, kdf, vf, kkf, af, rb, lwb, kdb, vb, kkb, ab, of, ob, s_ref):
    i = pl.program_id(1)

    @pl.when(i == 0)
    def _():
        s_ref[...] = jnp.zeros_like(s_ref)

    n = rf.shape[1]
    ti = lax.broadcasted_iota(jnp.int32, (n, n), 0)
    tj = lax.broadcasted_iota(jnp.int32, (n, n), 1)

    def decayed(r_ref, lw_ref, kd_ref, v_ref, kk_ref, a_ref, fwd):
        lw = lw_ref[0, 0]
        kd = kd_ref[0, 0]
        kk = kk_ref[0]
        b = kk * a_ref[0, 0]
        incl = (tj <= ti) if fwd else (tj >= ti)
        c = _dot_exact_lhs(incl.astype(F32), lw)
        ctot = jnp.sum(lw, axis=0, keepdims=True)
        e_nc = jnp.exp(-c)
        e_tot = jnp.exp(ctot - c)
        return dict(rt=r_ref[0] * jnp.exp(c), at=kk * jnp.exp(c - lw), kt=kd * e_nc, bt=b * e_nc,
                    kh=kd * e_tot, bh=b * e_tot, v=v_ref[0], e=jnp.exp(ctot))

    pf = decayed(rf, lwf, kdf, vf, kkf, af, True)
    pb = decayed(rb, lwb, kdb, vb, kkb, ab, False)

    def chains(name):
        return jnp.concatenate([_split_heads(pf[name]), _split_heads(pb[name])], axis=0)

    rt, at, kt, bt, kh, bh, v = (chains(x) for x in ("rt", "at", "kt", "bt", "kh", "bh", "v"))
    ar = jnp.concatenate([at, rt], axis=1)
    kb = jnp.concatenate([kt, bt], axis=1)
    m = _bmm_x3(ar, kb, _BMM_NT)
    chain = lax.broadcasted_iota(jnp.int32, (RWKV_CHAINS, n, n), 0)
    lag = jnp.where(chain < A_HEADS, ti - tj, tj - ti)
    before = lag > 0
    before_eq = lag >= 0
    a_ak = jnp.where(before, m[:, :n, :n], 0.0)
    a_ab = jnp.where(before, m[:, :n, n:], 0.0)
    a_rk = jnp.where(before_eq, m[:, n:, :n], 0.0)
    a_rb = jnp.where(before_eq, m[:, n:, n:], 0.0)
    eye = (ti == tj).astype(F32)
    pw = -a_ab
    tinv = eye + pw
    for _ in range(int(math.log2(n)) - 1):
        pw = _bmm_x3(pw, pw, _BMM_NN)
        tinv = tinv + _bmm_x3(tinv, pw, _BMM_NN)
    sk = s_ref[...]
    a_s = _bmm(ar, sk, _BMM_NN)
    u = _bmm(tinv, a_s[:, :n] + _bmm(a_ak, v, _BMM_NN), _BMM_NN)
    vu = jnp.concatenate([v, -u], axis=1)
    out = a_s[:, n:] + _bmm(jnp.concatenate([a_rk, a_rb], axis=2), vu, _BMM_NN)
    e_rows = jnp.concatenate([_split_heads(pf["e"]), _split_heads(pb["e"])], axis=0)
    e_col = jnp.sum(eye * e_rows, axis=2, keepdims=True)
    s_ref[...] = sk * e_col + _bmm(jnp.concatenate([kh, bh], axis=1), vu, _BMM_TN)
    for h in range(A_HEADS):
        sl = slice(h * A_HEAD_DIM, (h + 1) * A_HEAD_DIM)
        of[0, :, sl] = out[h]
        ob[0, :, sl] = out[A_HEADS + h]


def _rwkv_scan(r, lw, kd, v, kk, a, ctx_len):
    bz, t, _ = r.shape
    n_ctx = ctx_len // RWKV_CHUNK
    n_chunks = t // RWKV_CHUNK
    rev = lambda i: _seg_order(i, 1, n_ctx, n_chunks)
    tok_f = pl.BlockSpec((1, RWKV_CHUNK, A_W), lambda b, i: (b, i, 0))
    tok_b = pl.BlockSpec((1, RWKV_CHUNK, A_W), lambda b, i: (b, rev(i), 0))
    dir_f = pl.BlockSpec((1, 1, RWKV_CHUNK, A_W), lambda b, i: (0, b, i, 0))
    dir_b = pl.BlockSpec((1, 1, RWKV_CHUNK, A_W), lambda b, i: (1, b, rev(i), 0))
    shp = jax.ShapeDtypeStruct((bz, t, A_W), F32)
    return pl.pallas_call(
        _rwkv_scan_kernel,
        grid=(bz, n_chunks),
        in_specs=[tok_f, dir_f, dir_f, tok_f, tok_f, dir_f, tok_b, dir_b, dir_b, tok_b, tok_b, dir_b],
        out_specs=[tok_f, tok_b],
        out_shape=[shp, shp],
        scratch_shapes=[pltpu.VMEM((RWKV_CHAINS, A_HEAD_DIM, A_HEAD_DIM), F32)],
        compiler_params=_params("parallel", "arbitrary"),
        name="rwkv_scan",
    )(r, lw, kd, v, kk, a, r, lw, kd, v, kk, a)


def _rglru_kernel(p_ref, pv_ref, nx_ref, cw_ref, cb_ref, gw_ref, gbias_ref, lam_ref, o_ref,
                  a_scr, b_scr, h_scr, carry_ref, *, n_ctx_tiles, n_tiles):
    d = pl.program_id(1)
    i = pl.program_id(2)
    blk = _seg_order(i, d, n_ctx_tiles, n_tiles)

    @pl.when(i == 0)
    def _():
        carry_ref[...] = jnp.zeros_like(carry_ref)

    p = p_ref[0]
    tr = p.shape[0]
    x = p[:, :B_W]
    gate_in = p[:, B_W:]
    has_prev = jnp.logical_and(blk != 0, blk != n_ctx_tiles)
    has_next = jnp.logical_and(blk != n_ctx_tiles - 1, blk != n_tiles - 1)
    pv = jnp.where(has_prev, pv_ref[0][:, :B_W], 0.0)
    nx = jnp.where(has_next, nx_ref[0][:, :B_W], 0.0)
    row = lax.broadcasted_iota(jnp.int32, (tr, B_W), 0)
    pv_m1 = pv[7:8]
    pv_m2 = pv[6:7]
    nx_p1 = nx[0:1]
    x_m1 = jnp.where(row == 0, pv_m1, pltpu.roll(x, 1, 0))
    x_m2 = jnp.where(row == 0, pv_m2, jnp.where(row == 1, pv_m1, pltpu.roll(x, 2, 0)))
    x_p1 = jnp.where(row == tr - 1, nx_p1, pltpu.roll(x, tr - 1, 0))
    cw = cw_ref[...]
    u = cw[0:1] * x_m2 + cw[1:2] * x_m1 + cw[2:3] * x + cw[3:4] * x_p1 + cb_ref[...]
    gate = 0.5 * gate_in * (1.0 + jnp.tanh(math.sqrt(2.0 / math.pi) * (gate_in + 0.044715 * gate_in ** 3)))
    pre = _dot(u, gw_ref[0]) + gbias_ref[0]
    rec = _sigmoid(pre[:, :B_W])
    inp = _sigmoid(pre[:, B_W:])
    log_a = -B_C * rec * _softplus(-lam_ref[0])
    a_scr[...] = jnp.exp(log_a)
    b_scr[...] = jnp.sqrt(1.0 - jnp.exp(2.0 * log_a)) * inp * u

    n_grp = tr // 8
    g8 = lax.broadcasted_iota(jnp.int32, (8, B_W), 0)

    def group_scan(a, b, fwd):
        for s in (1, 2, 4):
            if fwd:
                ident = g8 < s
                a_p = pltpu.roll(a, s, 0)
                b_p = pltpu.roll(b, s, 0)
            else:
                ident = g8 >= 8 - s
                a_p = pltpu.roll(a, 8 - s, 0)
                b_p = pltpu.roll(b, 8 - s, 0)
            a_p = jnp.where(ident, 1.0, a_p)
            b_p = jnp.where(ident, 0.0, b_p)
            b = a * b_p + b
            a = a * a_p
        return a, b

    def run(fwd):
        def body(j, carry):
            g = j if fwd else n_grp - 1 - j
            off = pl.multiple_of(g * 8, 8)
            a, b = group_scan(a_scr[pl.ds(off, 8), :], b_scr[pl.ds(off, 8), :], fwd)
            h = b + a * carry
            h_scr[pl.ds(off, 8), :] = h
            return h[7:8] if fwd else h[0:1]
        carry_ref[...] = lax.fori_loop(0, n_grp, body, carry_ref[...])

    @pl.when(d == 0)
    def _():
        run(True)

    @pl.when(d == 1)
    def _():
        run(False)

    o_ref[0, 0] = h_scr[...] * gate


def _rglru(pb, conv_w, conv_b, gate_w, gate_b, lam, n_ctx_tiles):
    bz, t, _ = pb.shape
    n_tiles = t // ROW_TILE
    r8 = ROW_TILE // 8
    n8 = t // 8
    eye = jnp.eye(B_BLOCKS, dtype=F32)
    gw = jnp.einsum('dsgio,gh->dgisho', gate_w, eye).reshape(2, B_W, 2 * B_W).astype(BF16)
    gbias = gate_b.reshape(2, 1, 2 * B_W)
    blk = lambda i, d: _seg_order(i, d, n_ctx_tiles, n_tiles)
    return pl.pallas_call(
        functools.partial(_rglru_kernel, n_ctx_tiles=n_ctx_tiles, n_tiles=n_tiles),
        grid=(bz, 2, n_tiles),
        in_specs=[pl.BlockSpec((1, ROW_TILE, B_IN), lambda b, d, i: (b, blk(i, d), 0)),
                  pl.BlockSpec((1, 8, B_IN), lambda b, d, i: (b, jnp.maximum(blk(i, d) * r8 - 1, 0), 0)),
                  pl.BlockSpec((1, 8, B_IN), lambda b, d, i: (b, jnp.minimum((blk(i, d) + 1) * r8, n8 - 1), 0)),
                  pl.BlockSpec((4, B_W), lambda b, d, i: (0, 0)),
                  pl.BlockSpec((1, B_W), lambda b, d, i: (0, 0)),
                  pl.BlockSpec((1, B_W, 2 * B_W), lambda b, d, i: (d, 0, 0)),
                  pl.BlockSpec((1, 1, 2 * B_W), lambda b, d, i: (d, 0, 0)),
                  pl.BlockSpec((1, 1, B_W), lambda b, d, i: (d, 0, 0))],
        out_specs=pl.BlockSpec((1, 1, ROW_TILE, B_W), lambda b, d, i: (d, b, blk(i, d), 0)),
        out_shape=jax.ShapeDtypeStruct((2, bz, t, B_W), F32),
        scratch_shapes=[pltpu.VMEM((ROW_TILE, B_W), F32), pltpu.VMEM((ROW_TILE, B_W), F32),
                        pltpu.VMEM((ROW_TILE, B_W), F32), pltpu.VMEM((1, B_W), F32)],
        compiler_params=_params("parallel", "parallel", "arbitrary"),
        name="rglru",
    )(pb, pb, pb, conv_w, conv_b.reshape(1, B_W), gw, gbias, lam.reshape(2, 1, B_W))


def _retention_kernel(lg_ref, p_ref, cos_ref, sin_ref, o_ref, st_ref):
    d = pl.program_id(1)
    i = pl.program_id(2)

    @pl.when(i == 0)
    def _():
        st_ref[...] = jnp.zeros_like(st_ref)

    n = p_ref.shape[1]
    cos2 = cos_ref[...]
    sin2 = sin_ref[...]
    ti = lax.broadcasted_iota(jnp.int32, (n, n), 0)
    tj = lax.broadcasted_iota(jnp.int32, (n, n), 1)
    fwd = d == 0
    diff = jnp.where(fwd, ti - tj, tj - ti).astype(F32)
    t1 = lax.broadcasted_iota(jnp.int32, (n, 1), 0)
    pos = jnp.where(fwd, t1, n - 1 - t1).astype(F32)
    for h in range(C_HEADS):
        lg = lg_ref[d, h] * jnp.ones((1, 1), F32)
        sl = slice(h * C_HEAD_DIM, (h + 1) * C_HEAD_DIM)
        q = p_ref[0, :, sl]
        k = p_ref[0, :, C_W + h * C_HEAD_DIM:C_W + (h + 1) * C_HEAD_DIM]
        v = p_ref[0, :, 2 * C_W + h * C_HEAD_DIM:2 * C_W + (h + 1) * C_HEAD_DIM]
        q = q * cos2 + pltpu.roll(q, C_HEAD_DIM // 2, 1) * sin2
        k = (k * cos2 + pltpu.roll(k, C_HEAD_DIM // 2, 1) * sin2) * (C_HEAD_DIM ** -0.5)
        decay = jnp.where(diff >= 0, jnp.exp(lg * jnp.maximum(diff, 0.0)), 0.0)
        scores = _dot_nt(q, k) * decay
        st = st_ref[h]
        out = _dot(scores, v) + _dot(q, st) * jnp.exp(lg * (pos + 1.0))
        zeta = jnp.exp(lg * (n - 1.0 - pos))
        st_ref[h] = st * jnp.exp(lg * n) + _dot_tn(k * zeta, v)
        o_ref[0, 0, :, sl] = out


def _retention(pc, log_gamma, cos2, sin2, ctx_len):
    bz, t, _ = pc.shape
    n_ctx = ctx_len // ODD_CHUNK
    n_chunks = t // ODD_CHUNK
    blk = lambda i, d: _seg_order(i, d, n_ctx, n_chunks)
    return pl.pallas_call(
        _retention_kernel,
        grid_spec=pltpu.PrefetchScalarGridSpec(
            num_scalar_prefetch=1,
            grid=(bz, 2, n_chunks),
            in_specs=[pl.BlockSpec((1, ODD_CHUNK, 3 * C_W), lambda b, d, i, lg: (b, blk(i, d), 0)),
                      pl.BlockSpec((ODD_CHUNK, C_HEAD_DIM), lambda b, d, i, lg: (blk(i, d), 0)),
                      pl.BlockSpec((ODD_CHUNK, C_HEAD_DIM), lambda b, d, i, lg: (blk(i, d), 0))],
            out_specs=pl.BlockSpec((1, 1, ODD_CHUNK, C_W), lambda b, d, i, lg: (d, b, blk(i, d), 0)),
            scratch_shapes=[pltpu.VMEM((C_HEADS, C_HEAD_DIM, C_HEAD_DIM), F32)]),
        out_shape=jax.ShapeDtypeStruct((2, bz, t, C_W), F32),
        compiler_params=_params("parallel", "parallel", "arbitrary"),
        name="retention",
    )(log_gamma, pc, cos2, sin2)


def _mlstm_kernel(p_ref, bias_ref, o_ref, c_ref, n_ref, m_ref):
    d = pl.program_id(1)
    i = pl.program_id(2)

    @pl.when(i == 0)
    def _():
        c_ref[...] = jnp.zeros_like(c_ref)
        n_ref[...] = jnp.zeros_like(n_ref)
        m_ref[...] = jnp.full_like(m_ref, -jnp.inf)

    n = p_ref.shape[1]
    ti = lax.broadcasted_iota(jnp.int32, (n, n), 0)
    tj = lax.broadcasted_iota(jnp.int32, (n, n), 1)
    fwd = d == 0
    before_eq = jnp.where(fwd, ti - tj, tj - ti) >= 0
    gts = p_ref[0, :, 4 * D_W:] + bias_ref[...]
    gts = jnp.where(fwd, gts, pltpu.roll(gts, LANES - D_HEADS, 1))
    lf = jnp.minimum(gts, 0.0) - jnp.log(1.0 + jnp.exp(-jnp.abs(gts)))
    bcum = _dot_exact_lhs(before_eq.astype(F32), lf)
    btot = jnp.sum(lf, axis=0, keepdims=True)
    gts_t = gts.T
    bcum_t = bcum.T
    for h in range(D_HEADS):
        sl = slice(h * D_HEAD_DIM, (h + 1) * D_HEAD_DIM)
        q = p_ref[0, :, sl]
        k = p_ref[0, :, D_W + h * D_HEAD_DIM:D_W + (h + 1) * D_HEAD_DIM] * (D_HEAD_DIM ** -0.5)
        v = p_ref[0, :, 2 * D_W + h * D_HEAD_DIM:2 * D_W + (h + 1) * D_HEAD_DIM]
        ig_col = gts[:, h:h + 1]
        ig_row = gts_t[h:h + 1, :]
        b_col = bcum[:, 2 * D_HEADS + h:2 * D_HEADS + h + 1]
        b_row = bcum_t[2 * D_HEADS + h:2 * D_HEADS + h + 1, :]
        b_end = btot[:, 2 * D_HEADS + h:2 * D_HEADS + h + 1]
        m_prev = m_ref[h][0:1, 0:1]
        d_log = jnp.where(before_eq, b_col - b_row + ig_row, -jnp.inf)
        g_inter = b_col + m_prev
        m_t = jnp.maximum(g_inter, jnp.max(d_log, axis=1, keepdims=True))
        s_intra = _dot_nt(q, k) * jnp.exp(d_log - m_t)
        s_inter = jnp.exp(g_inter - m_t)
        num = _dot(s_intra, v) + _dot(q, c_ref[h]) * s_inter
        den = jnp.sum(s_intra, axis=1, keepdims=True) + jnp.sum(q * n_ref[h], axis=1, keepdims=True) * s_inter
        den = jnp.maximum(jnp.abs(den), jnp.exp(-m_t))
        o_ref[0, 0, :, sl] = num / den
        w_end = b_end - b_col + ig_col
        m_i = jnp.max(w_end, axis=0, keepdims=True)
        e_end = jnp.exp(w_end - m_i)
        ke = k * e_end
        m_new = jnp.maximum(b_end + m_prev, m_i)
        s_old = jnp.exp(b_end + m_prev - m_new)
        s_new = jnp.exp(m_i - m_new)
        c_ref[h] = c_ref[h] * s_old + _dot_tn(ke, v) * s_new
        n_ref[h] = n_ref[h] * s_old + jnp.sum(ke, axis=0, keepdims=True) * s_new
        m_ref[h] = jnp.broadcast_to(m_new, m_ref.shape[1:])


def _mlstm(pd, bias128, ctx_len):
    bz, t, w = pd.shape
    n_ctx = ctx_len // ODD_CHUNK
    n_chunks = t // ODD_CHUNK
    blk = lambda i, d: _seg_order(i, d, n_ctx, n_chunks)
    return pl.pallas_call(
        _mlstm_kernel,
        grid=(bz, 2, n_chunks),
        in_specs=[pl.BlockSpec((1, ODD_CHUNK, w), lambda b, d, i: (b, blk(i, d), 0)),
                  pl.BlockSpec((1, LANES), lambda b, d, i: (0, 0))],
        out_specs=pl.BlockSpec((1, 1, ODD_CHUNK, D_W), lambda b, d, i: (d, b, blk(i, d), 0)),
        out_shape=jax.ShapeDtypeStruct((2, bz, t, D_W), F32),
        scratch_shapes=[pltpu.VMEM((D_HEADS, D_HEAD_DIM, D_HEAD_DIM), F32),
                        pltpu.VMEM((D_HEADS, 1, D_HEAD_DIM), F32),
                        pltpu.VMEM((D_HEADS, 8, LANES), F32)],
        compiler_params=_params("parallel", "parallel", "arbitrary"),
        name="mlstm",
    )(pd, bias128)


def _group_norm_lanes(x, n_groups, eps):
    gw = x.shape[1] // n_groups
    outs = []
    for g in range(n_groups):
        xg = x[:, g * gw:(g + 1) * gw]
        mu = jnp.mean(xg, axis=1, keepdims=True)
        xc = xg - mu
        var = jnp.mean(xc * xc, axis=1, keepdims=True)
        outs.append(xc * lax.rsqrt(var + eps))
    return jnp.concatenate(outs, axis=1)


def _post_mixer(ycat, x_ref, wout_ref, mods_ref, lng_ref, lnb_ref, rw_ref, rb_ref, x1_o, h2_o, gates_o, alpha):
    y = jnp.dot(ycat.astype(BF16), wout_ref[...], preferred_element_type=F32)
    mods = mods_ref[0, 0]
    x1 = alpha * x_ref[0] + mods[2:3] * y
    mu = jnp.mean(x1, axis=1, keepdims=True)
    xc = x1 - mu
    var = jnp.mean(xc * xc, axis=1, keepdims=True)
    x1 = xc * lax.rsqrt(var + LN_EPS) * lng_ref[...] + lnb_ref[...]
    x1_o[0] = x1
    h2 = x1 * (1.0 + mods[4:5]) + mods[3:4]
    h2_o[0] = h2.astype(BF16)
    logits = _dot_x3(h2, rw_ref[...]) + rb_ref[...]
    lane = lax.broadcasted_iota(jnp.int32, logits.shape, 1)
    logits = jnp.where(lane < N_EXPERTS, logits, -jnp.inf)
    work = logits
    sel = lane < 0
    top = None
    for kth in range(TOP_K):
        mk = jnp.max(work, axis=1, keepdims=True)
        idx = jnp.min(jnp.where(work == mk, lane, LANES), axis=1, keepdims=True)
        pick = lane == idx
        sel = jnp.logical_or(sel, pick)
        work = jnp.where(pick, -jnp.inf, work)
        if kth == 0:
            top = mk
    e = jnp.where(sel, jnp.exp(logits - top), 0.0)
    gates_o[0] = e / jnp.sum(e, axis=1, keepdims=True)


def _outproj_even_kernel(wkvf_ref, wkvb_ref, bonus_ref, g_ref, gng_ref, gnb_ref, hb_ref,
                         x_ref, wout_ref, mods_ref, lng_ref, lnb_ref, rw_ref, rb_ref,
                         x1_o, h2_o, gates_o, *, alpha):
    wkv = wkvf_ref[0] + wkvb_ref[0]
    ya = (_group_norm_lanes(wkv, A_HEADS, GN_EPS_RWKV) * gng_ref[...] + gnb_ref[...] + bonus_ref[0]) * g_ref[0]
    yb = hb_ref[0, 0] + hb_ref[1, 0]
    ycat = jnp.concatenate([ya, yb], axis=1)
    _post_mixer(ycat, x_ref, wout_ref, mods_ref, lng_ref, lnb_ref, rw_ref, rb_ref, x1_o, h2_o, gates_o, alpha)


def _outproj_odd_kernel(oc_ref, pc_ref, cg_ref, cb_ref, od_ref, pd_ref, dg_ref, db_ref,
                        x_ref, wout_ref, mods_ref, lng_ref, lnb_ref, rw_ref, rb_ref,
                        x1_o, h2_o, gates_o, *, alpha):
    oc = oc_ref[0, 0] + oc_ref[1, 0]
    gc = pc_ref[0]
    ya = (_group_norm_lanes(oc, C_HEADS, LN_EPS) * cg_ref[...] + cb_ref[...]) * (gc * _sigmoid(gc))
    od = od_ref[0, 0] + od_ref[1, 0]
    yb = (_group_norm_lanes(od, D_HEADS, LN_EPS) * dg_ref[...] + db_ref[...]) * _sigmoid(pd_ref[0])
    ycat = jnp.concatenate([ya, yb], axis=1)
    _post_mixer(ycat, x_ref, wout_ref, mods_ref, lng_ref, lnb_ref, rw_ref, rb_ref, x1_o, h2_o, gates_o, alpha)


def _outproj(kind, mixer_args, mixer_specs, x, wout, mods, ln_g, ln_b, rw, rb, n_ctx_tiles, alpha, first_tile):
    bz, t, d = x.shape
    t_out = t - first_tile * ROW_TILE
    seg = lambda i: (i + first_tile >= n_ctx_tiles).astype(jnp.int32)
    row = lambda a: a.reshape(1, -1)
    vec = lambda n: pl.BlockSpec((1, n), lambda b, i: (0, 0))
    kern = _outproj_even_kernel if kind == 'even' else _outproj_odd_kernel
    tok = lambda n: pl.BlockSpec((1, ROW_TILE, n), lambda b, i: (b, i, 0))
    return pl.pallas_call(
        functools.partial(kern, alpha=alpha),
        grid=(bz, t_out // ROW_TILE),
        in_specs=list(mixer_specs) + [
            pl.BlockSpec((1, ROW_TILE, d), lambda b, i: (b, i + first_tile, 0)),
            pl.BlockSpec((d, d), lambda b, i: (0, 0)),
            pl.BlockSpec((1, 1, 8, d), lambda b, i: (b, seg(i), 0, 0)),
            vec(d), vec(d),
            pl.BlockSpec((d, LANES), lambda b, i: (0, 0)),
            vec(LANES)],
        out_specs=[tok(d), tok(d), tok(LANES)],
        out_shape=[jax.ShapeDtypeStruct((bz, t_out, d), F32),
                   jax.ShapeDtypeStruct((bz, t_out, d), BF16),
                   jax.ShapeDtypeStruct((bz, t_out, LANES), F32)],
        compiler_params=_params("parallel", "parallel"),
        name="outproj_" + kind,
    )(*mixer_args, x, wout, mods, row(ln_g), row(ln_b), rw, rb)


def _moe_kernel(h_ref, gates_ref, w1_ref, b1_ref, w2_ref, b2_ref, o_ref, *, f_chunk):
    e = pl.program_id(1)

    @pl.when(e == 0)
    def _():
        o_ref[...] = jnp.zeros_like(o_ref)

    h = h_ref[...]
    gates = gates_ref[...]
    lane = lax.broadcasted_iota(jnp.int32, gates.shape, 1)
    ge = jnp.sum(jnp.where(lane == e, gates, 0.0), axis=1, keepdims=True)
    acc = jnp.zeros(o_ref.shape, F32)
    for c in range(D_FF // f_chunk):
        lo = c * f_chunk
        glu = jnp.dot(h, w1_ref[0, :, lo:lo + f_chunk], preferred_element_type=F32) + b1_ref[0, :, lo:lo + f_chunk]
        lin = (jnp.dot(h, w1_ref[0, :, D_FF + lo:D_FF + lo + f_chunk], preferred_element_type=F32)
               + b1_ref[0, :, D_FF + lo:D_FF + lo + f_chunk])
        glu = jnp.minimum(glu, SWIGLU_LIMIT)
        lin = jnp.clip(lin, -SWIGLU_LIMIT, SWIGLU_LIMIT)
        act = (lin + 1.0) * glu * _sigmoid(SWIGLU_ALPHA * glu)
        acc += jnp.dot(act.astype(BF16), w2_ref[0, lo:lo + f_chunk, :], preferred_element_type=F32)
    o_ref[...] += ge * (acc + b2_ref[0])


def _moe(h2, gates, w1, b1, w2, b2):
    n, d = h2.shape
    n_e = w1.shape[0]
    tile = MOE_TILE if n % MOE_TILE == 0 else ROW_TILE
    return pl.pallas_call(
        functools.partial(_moe_kernel, f_chunk=512),
        grid=(n // tile, n_e),
        in_specs=[pl.BlockSpec((tile, d), lambda i, e: (i, 0)),
                  pl.BlockSpec((tile, LANES), lambda i, e: (i, 0)),
                  pl.BlockSpec((1, d, 2 * D_FF), lambda i, e: (e, 0, 0)),
                  pl.BlockSpec((1, 1, 2 * D_FF), lambda i, e: (e, 0, 0)),
                  pl.BlockSpec((1, D_FF, d), lambda i, e: (e, 0, 0)),
                  pl.BlockSpec((1, 1, d), lambda i, e: (e, 0, 0))],
        out_specs=pl.BlockSpec((tile, d), lambda i, e: (i, 0)),
        out_shape=jax.ShapeDtypeStruct((n, d), F32),
        compiler_params=_params("parallel", "arbitrary"),
        name="moe",
    )(h2, gates, w1, b1.reshape(n_e, 1, -1), w2, b2.reshape(n_e, 1, -1))


def _ffn_norm_kernel(x_ref, f_ref, mods_ref, lng_ref, lnb_ref, o_ref, *, alpha):
    mods = mods_ref[0, 0]
    x2 = alpha * x_ref[0] + mods[5:6] * f_ref[0]
    mu = jnp.mean(x2, axis=1, keepdims=True)
    xc = x2 - mu
    var = jnp.mean(xc * xc, axis=1, keepdims=True)
    o_ref[0] = xc * lax.rsqrt(var + LN_EPS) * lng_ref[...] + lnb_ref[...]


def _ffn_norm(x1, ffn, mods, ln_g, ln_b, n_ctx_tiles, alpha, first_tile):
    bz, t, d = x1.shape
    seg = lambda i: (i + first_tile >= n_ctx_tiles).astype(jnp.int32)
    tok = pl.BlockSpec((1, ROW_TILE, d), lambda b, i: (b, i, 0))
    vec = pl.BlockSpec((1, d), lambda b, i: (0, 0))
    return pl.pallas_call(
        functools.partial(_ffn_norm_kernel, alpha=alpha),
        grid=(bz, t // ROW_TILE),
        in_specs=[tok, tok, pl.BlockSpec((1, 1, 8, d), lambda b, i: (b, seg(i), 0, 0)), vec, vec],
        out_specs=tok,
        out_shape=jax.ShapeDtypeStruct((bz, t, d), F32),
        compiler_params=_params("parallel", "parallel"),
        name="ffn_norm",
    )(x1, ffn, mods, ln_g.reshape(1, d), ln_b.reshape(1, d))


def _rope_tables(ctx_len, seq):
    idx = jnp.arange(seq)
    rowp = (idx // GRID_W).astype(F32)
    colp = (idx % GRID_W).astype(F32)
    n_freq = C_HEAD_DIM // 4
    inv = ROPE_BASE ** (-jnp.arange(n_freq, dtype=F32) / n_freq)
    ang = jnp.concatenate([rowp[:, None] * inv, colp[:, None] * inv], -1)
    cos, sin = jnp.cos(ang), jnp.sin(ang)
    cos2 = jnp.concatenate([cos, cos], -1)
    sin2 = jnp.concatenate([-sin, sin], -1)
    cos2 = jnp.concatenate([jnp.ones((ctx_len, C_HEAD_DIM), F32), cos2], 0)
    sin2 = jnp.concatenate([jnp.zeros((ctx_len, C_HEAD_DIM), F32), sin2], 0)
    return cos2, sin2


def kernel(x, c, ctx, c_ctx, w_mod, b_mod, ln_g, ln_b, even_w_in, even_w_out, a_mu, a_w0, a_wB, a_a0, a_aB, a_gB, a_kk, a_ka, a_rk, a_gn_g, a_gn_b, b_conv_w, b_conv_b, b_gate_w, b_gate_b, b_lam, odd_w_in, odd_w_out, c_log_gamma, c_gn_g, c_gn_b, d_ibias, d_fbias, d_gn_g, d_gn_b, router_w, router_b, exp_w1, exp_b1, exp_w2, exp_b2):
    bz, seq, d = x.shape
    ctx_len = ctx.shape[1]
    depth = w_mod.shape[0]
    assert ctx_len == ROW_TILE and seq % ROW_TILE == 0
    t = ctx_len + seq
    n_ctx_tiles = ctx_len // ROW_TILE
    alpha = (2 * depth) ** 0.25
    xs = jnp.concatenate([ctx, x], axis=1)
    c8 = jnp.concatenate([c, c_ctx[None], jnp.zeros((8 - bz - 1, d), F32)], 0)
    cos2, sin2 = _rope_tables(ctx_len, seq)
    row = lambda a: a.reshape(1, -1)
    vec = lambda n: pl.BlockSpec((1, n), lambda b, i: (0, 0))

    for layer in range(depth):
        first = n_ctx_tiles if layer == depth - 1 else 0
        tok = lambda n, first=first: pl.BlockSpec((1, ROW_TILE, n), lambda b, i: (b, i + first, 0))
        tok2 = lambda n, first=first: pl.BlockSpec((2, 1, ROW_TILE, n), lambda b, i: (0, b, i + first, 0))
        m = _mod_vectors(c8, w_mod[layer], b_mod[layer]).reshape(8, N_MOD, d)
        m_lat = m[:bz]
        m_ctx = jnp.broadcast_to(m[bz][None], (bz, N_MOD, d))
        mods = jnp.stack([m_ctx, m_lat], axis=1)
        mods = jnp.concatenate([mods, jnp.zeros((bz, 2, 8 - N_MOD, d), F32)], axis=2)
        shift = mods[:, :, 0:1]
        scale = mods[:, :, 1:2]
        rw = jnp.pad(router_w[layer], ((0, 0), (0, LANES - N_EXPERTS)))
        rb = jnp.pad(router_b[layer], (0, LANES - N_EXPERTS)).reshape(1, LANES)
        li = layer // 2
        if layer % 2 == 0:
            pa, pb = _inproj(xs, shift, scale, even_w_in[li].astype(BF16), A_IN, n_ctx_tiles)
            r, v, kk, g, bonus, lw, kd, a = _rwkv_prep(pa, a_mu[li], a_w0[li], a_wB[li], a_a0[li], a_aB[li],
                                                       a_gB[li], a_kk[li], a_ka[li], a_rk[li].reshape(-1),
                                                       n_ctx_tiles)
            wkv_f, wkv_b = _rwkv_scan(r, lw, kd, v, kk, a, ctx_len)
            hb = _rglru(pb, b_conv_w[li], b_conv_b[li], b_gate_w[li], b_gate_b[li], b_lam[li], n_ctx_tiles)
            mixer_args = (wkv_f, wkv_b, bonus, g, row(a_gn_g[li]), row(a_gn_b[li]), hb)
            mixer_specs = (tok(A_W), tok(A_W), tok(A_W), tok(A_W), vec(A_W), vec(A_W), tok2(B_W))
            x1, h2, gates = _outproj('even', mixer_args, mixer_specs, xs, even_w_out[li].astype(BF16), mods,
                                     ln_g[layer, 0], ln_b[layer, 0], rw, rb, n_ctx_tiles, alpha, first)
        else:
            w_in = jnp.pad(odd_w_in[li], ((0, 0), (0, D_IN_PAD - D_IN))).astype(BF16)
            pc, pd = _inproj(xs, shift, scale, w_in, C_IN, n_ctx_tiles)
            oc = _retention(pc, c_log_gamma[li], cos2, sin2, ctx_len)
            bias128 = jnp.concatenate([d_ibias[li].reshape(-1), d_fbias[li].reshape(-1),
                                       jnp.zeros((LANES - 4 * D_HEADS,), F32)]).reshape(1, LANES)
            od = _mlstm(pd, bias128, ctx_len)
            gate_c = pl.BlockSpec((1, ROW_TILE, C_W), lambda b, i, first=first: (b, i + first, 3))
            gate_d = pl.BlockSpec((1, ROW_TILE, D_W), lambda b, i, first=first: (b, i + first, 3))
            mixer_args = (oc, pc, row(c_gn_g[li]), row(c_gn_b[li]), od, pd, row(d_gn_g[li]), row(d_gn_b[li]))
            mixer_specs = (tok2(C_W), gate_c, vec(C_W), vec(C_W), tok2(D_W), gate_d, vec(D_W), vec(D_W))
            x1, h2, gates = _outproj('odd', mixer_args, mixer_specs, xs, odd_w_out[li].astype(BF16), mods,
                                     ln_g[layer, 0], ln_b[layer, 0], rw, rb, n_ctx_tiles, alpha, first)
        t_out = x1.shape[1]
        ffn = _moe(h2.reshape(bz * t_out, d), gates.reshape(bz * t_out, LANES), exp_w1[layer].astype(BF16),
                   exp_b1[layer], exp_w2[layer].astype(BF16), exp_b2[layer]).reshape(bz, t_out, d)
        xs = _ffn_norm(x1, ffn, mods, ln_g[layer, 1], ln_b[layer, 1], n_ctx_tiles, alpha, first)
    return xs[:, xs.shape[1] - seq:]
```

```python
import functools
import math

import jax
import jax.numpy as jnp
from jax import lax
from jax.experimental import pallas as pl
from jax.experimental.pallas import tpu as pltpu

F32 = jnp.float32
BF16 = jnp.bfloat16

GRID_W = 64
A_HEADS = 8
A_HEAD_DIM = 64
A_W = 512
A_DECAY_RANK = 64
A_ICL_RANK = 64
A_GATE_RANK = 128
A_IN = 1920
A_DECAY_SCALE = math.exp(-0.5)
GN_EPS_RWKV = 64e-5
B_W = 512
B_BLOCKS = 8
B_C = 8.0
B_IN = 1024
C_HEADS = 4
C_HEAD_DIM = 128
C_W = 512
C_IN = 2048
ROPE_BASE = 10000.0
D_HEADS = 4
D_HEAD_DIM = 128
D_W = 512
D_IN = 2064
D_IN_PAD = 2176
N_EXPERTS = 32
TOP_K = 4
D_FF = 1024
SWIGLU_ALPHA = 1.702
SWIGLU_LIMIT = 7.0
LN_EPS = 1e-5
N_MOD = 6
LANES = 128

ROW_TILE = 256
RWKV_CHUNK = 64
ODD_CHUNK = 256
EXPERT_TILE = 256
VMEM_LIMIT_BYTES = 56 * 1024 * 1024


def _params(*sem):
    return pltpu.CompilerParams(dimension_semantics=sem, vmem_limit_bytes=VMEM_LIMIT_BYTES)


def _dot(a, b):
    return jnp.dot(a.astype(BF16), b.astype(BF16), preferred_element_type=F32)


def _dot_nt(a, b):
    return lax.dot_general(a.astype(BF16), b.astype(BF16), (((1,), (1,)), ((), ())),
                           preferred_element_type=F32)


def _dot_tn(a, b):
    return lax.dot_general(a.astype(BF16), b.astype(BF16), (((0,), (0,)), ((), ())),
                           preferred_element_type=F32)


def _split(x):
    hi = x.astype(BF16)
    lo = (x - hi.astype(F32)).astype(BF16)
    return hi, lo


def _dot_x3(a, b):
    ah, al = _split(a)
    bh, bl = _split(b)
    out = jnp.dot(ah, bh, preferred_element_type=F32)
    out += jnp.dot(ah, bl, preferred_element_type=F32)
    out += jnp.dot(al, bh, preferred_element_type=F32)
    return out


def _dot_nt_x3(a, b):
    ah, al = _split(a)
    bh, bl = _split(b)
    dn = (((1,), (1,)), ((), ()))
    out = lax.dot_general(ah, bh, dn, preferred_element_type=F32)
    out += lax.dot_general(ah, bl, dn, preferred_element_type=F32)
    out += lax.dot_general(al, bh, dn, preferred_element_type=F32)
    return out


def _dot_exact_lhs(a, b):
    a = a.astype(BF16)
    b1 = b.astype(BF16)
    r1 = b - b1.astype(F32)
    b2 = r1.astype(BF16)
    b3 = (r1 - b2.astype(F32)).astype(BF16)
    out = jnp.dot(a, b1, preferred_element_type=F32)
    out += jnp.dot(a, b2, preferred_element_type=F32)
    out += jnp.dot(a, b3, preferred_element_type=F32)
    return out


def _sigmoid(x):
    return 1.0 / (1.0 + jnp.exp(-x))


def _softplus(x):
    return jnp.maximum(x, 0.0) + jnp.log(1.0 + jnp.exp(-jnp.abs(x)))


def _seg_order(i, d, n_ctx, n_tot):
    rev = jnp.where(i < n_ctx, n_ctx - 1 - i, n_ctx + n_tot - 1 - i)
    return jnp.where(d == 0, i, rev)


def _mod_kernel(c_ref, w_ref, b_ref, o_ref):
    c = c_ref[...]
    s = c * _sigmoid(c)
    o_ref[...] = _dot_x3(s, w_ref[...]) + b_ref[...]


def _mod_vectors(c8, w, b):
    d, n = w.shape
    tn = n // 4
    return pl.pallas_call(
        _mod_kernel,
        grid=(n // tn,),
        in_specs=[pl.BlockSpec((8, d), lambda j: (0, 0)),
                  pl.BlockSpec((d, tn), lambda j: (0, j)),
                  pl.BlockSpec((1, tn), lambda j: (0, j))],
        out_specs=pl.BlockSpec((8, tn), lambda j: (0, j)),
        out_shape=jax.ShapeDtypeStruct((8, n), F32),
        compiler_params=_params("parallel"),
        name="mod_vectors",
    )(c8, w, b.reshape(1, n))


def _inproj_kernel(x_ref, sh_ref, sc_ref, w_ref, oa_ref, ob_ref, *, n_a):
    h = x_ref[0] * (1.0 + sc_ref[0, 0]) + sh_ref[0, 0]
    p = jnp.dot(h.astype(BF16), w_ref[...], preferred_element_type=F32)
    oa_ref[0] = p[:, :n_a]
    ob_ref[0] = p[:, n_a:]


def _inproj(x, shift, scale, w_bf16, n_a, n_ctx_tiles):
    bz, t, d = x.shape
    n = w_bf16.shape[1]
    n_b = n - n_a
    seg = lambda i: (i >= n_ctx_tiles).astype(jnp.int32)
    return pl.pallas_call(
        functools.partial(_inproj_kernel, n_a=n_a),
        grid=(bz, t // ROW_TILE),
        in_specs=[pl.BlockSpec((1, ROW_TILE, d), lambda b, i: (b, i, 0)),
                  pl.BlockSpec((1, 1, 1, d), lambda b, i: (b, seg(i), 0, 0)),
                  pl.BlockSpec((1, 1, 1, d), lambda b, i: (b, seg(i), 0, 0)),
                  pl.BlockSpec((d, n), lambda b, i: (0, 0))],
        out_specs=[pl.BlockSpec((1, ROW_TILE, n_a), lambda b, i: (b, i, 0)),
                   pl.BlockSpec((1, ROW_TILE, n_b), lambda b, i: (b, i, 0))],
        out_shape=[jax.ShapeDtypeStruct((bz, t, n_a), F32),
                   jax.ShapeDtypeStruct((bz, t, n_b), F32)],
        compiler_params=_params("parallel", "parallel"),
        name="inproj",
    )(x, shift, scale, w_bf16)


def _rwkv_prep_kernel(p_ref, up_ref, dn_ref, mu_ref, w0_ref, wb_ref, a0_ref, ab_ref, gb_ref,
                      kkw_ref, kaw_ref, rkw_ref, ones_ref,
                      r_o, v_o, kk_o, g_o, bonus_o, lw_o, kd_o, a_o, *, n_ctx_tiles, n_tiles):
    i = pl.program_id(1)
    p = p_ref[0]
    tr, w = p.shape
    row = lax.broadcasted_iota(jnp.int32, (tr, w), 0)
    lane = lax.broadcasted_iota(jnp.int32, (tr, w), 1)
    prev = pltpu.roll(p, 1, 0)
    nxt = pltpu.roll(p, tr - 1, 0)
    sh_ctx = jnp.where(lane % 2 == 0, jnp.where(row == 0, 0.0, prev), jnp.where(row == tr - 1, 0.0, nxt))
    col = row % GRID_W
    left = jnp.where(col == 0, 0.0, prev)
    right = jnp.where(col == GRID_W - 1, 0.0, nxt)
    up_halo = jnp.where(i > n_ctx_tiles, up_ref[0], 0.0)
    dn_halo = jnp.where(i < n_tiles - 1, dn_ref[0], 0.0)
    up = jnp.concatenate([up_halo, p[:tr - GRID_W]], axis=0)
    down = jnp.concatenate([p[GRID_W:], dn_halo], axis=0)
    slot = lane % 4
    sh_lat = jnp.where(slot == 0, left, jnp.where(slot == 1, right, jnp.where(slot == 2, up, down)))
    sh = jnp.where(i < n_ctx_tiles, sh_ctx, sh_lat)
    z = p + (sh - p) * mu_ref[...]

    r = z[:, 0:A_W]
    k = z[:, A_W:2 * A_W]
    v = z[:, 2 * A_W:3 * A_W]
    o = 3 * A_W
    wd = z[:, o:o + 2 * A_DECAY_RANK]
    o += 2 * A_DECAY_RANK
    ad = z[:, o:o + 2 * A_ICL_RANK]
    o += 2 * A_ICL_RANK
    zg = z[:, o:o + A_GATE_RANK]

    ones_bd = ones_ref[...]

    def head_sum(x):
        hi, lo = _split(x)
        return (jnp.dot(hi, ones_bd, preferred_element_type=F32)
                + jnp.dot(lo, ones_bd, preferred_element_type=F32))

    g_o[0] = _dot(_sigmoid(zg), gb_ref[...])
    kk = k * kkw_ref[...]
    kk = kk / jnp.maximum(jnp.sqrt(head_sum(kk * kk)), 1e-12)
    r_o[0] = r
    v_o[0] = v
    kk_o[0] = kk
    rr = r * rkw_ref[...]
    bsum = jnp.zeros_like(r)
    for d in range(2):
        wdd = jnp.tanh(wd[:, d * A_DECAY_RANK:(d + 1) * A_DECAY_RANK])
        lw = -A_DECAY_SCALE * _sigmoid(w0_ref[d] + _dot(wdd, wb_ref[d]))
        a = _sigmoid(a0_ref[d] + _dot(ad[:, d * A_ICL_RANK:(d + 1) * A_ICL_RANK], ab_ref[d]))
        kd = k * (1.0 + (a - 1.0) * kaw_ref[...])
        lw_o[d, 0] = lw
        kd_o[d, 0] = kd
        a_o[d, 0] = a
        bsum = bsum + head_sum(rr * kd)
    bonus_o[0] = bsum * v


def _rwkv_prep(pa, mu, w0, wb, a0, ab, gb, kkw, kaw, rkw, n_ctx_tiles):
    bz, t, _ = pa.shape
    n_tiles = t // ROW_TILE
    hb = ROW_TILE // GRID_W
    n_hblk = t // GRID_W
    ones_bd = jnp.kron(jnp.eye(A_HEADS, dtype=F32), jnp.ones((A_HEAD_DIM, A_HEAD_DIM), F32)).astype(BF16)
    row = lambda a: a.reshape(1, -1)
    tok = pl.BlockSpec((1, ROW_TILE, A_W), lambda b, i: (b, i, 0))
    tok2 = pl.BlockSpec((2, 1, ROW_TILE, A_W), lambda b, i: (0, b, i, 0))
    full = lambda a: pl.BlockSpec(a.shape, lambda b, i: (0,) * a.ndim)
    args = (row(mu), w0.reshape(2, 1, A_W), wb.astype(BF16), a0.reshape(2, 1, A_W), ab.astype(BF16),
            gb.astype(BF16), row(kkw), row(kaw), row(rkw), ones_bd)
    s1 = jax.ShapeDtypeStruct((bz, t, A_W), F32)
    s2 = jax.ShapeDtypeStruct((2, bz, t, A_W), F32)
    return pl.pallas_call(
        functools.partial(_rwkv_prep_kernel, n_ctx_tiles=n_ctx_tiles, n_tiles=n_tiles),
        grid=(bz, n_tiles),
        in_specs=[pl.BlockSpec((1, ROW_TILE, A_IN), lambda b, i: (b, i, 0)),
                  pl.BlockSpec((1, GRID_W, A_IN), lambda b, i: (b, jnp.maximum(i * hb - 1, 0), 0)),
                  pl.BlockSpec((1, GRID_W, A_IN), lambda b, i: (b, jnp.minimum((i + 1) * hb, n_hblk - 1), 0))]
                 + [full(a) for a in args],
        out_specs=[tok, tok, tok, tok, tok, tok2, tok2, tok2],
        out_shape=[s1, s1, s1, s1, s1, s2, s2, s2],
        compiler_params=_params("parallel", "parallel"),
        name="rwkv_prep",
    )(pa, pa, pa, *args)


_BMM_NN = (((2,), (1,)), ((0,), (0,)))
_BMM_NT = (((2,), (2,)), ((0,), (0,)))
_BMM_TN = (((1,), (1,)), ((0,), (0,)))
RWKV_CHAINS = 2 * A_HEADS


def _bmm(a, b, dims):
    return lax.dot_general(a.astype(BF16), b.astype(BF16), dims, preferred_element_type=F32)


def _bmm_x3(a, b, dims):
    ah, al = _split(a)
    bh, bl = _split(b)
    rows = a.shape[1]
    p = lax.dot_general(jnp.concatenate([ah, al], axis=1), bh, dims, preferred_element_type=F32)
    return p[:, :rows] + p[:, rows:] + lax.dot_general(ah, bl, dims, preferred_element_type=F32)


def _split_heads(x):
    return jnp.stack([x[:, h * A_HEAD_DIM:(h + 1) * A_HEAD_DIM] for h in range(A_HEADS)], axis=0)


def _rwkv_scan_kernel(rf, lwf, kdf, vf, kkf, af, rb, lwb, kdb, vb, kkb, ab, of, ob, s_ref):
    i = pl.program_id(1)

    @pl.when(i == 0)
    def _():
        s_ref[...] = jnp.zeros_like(s_ref)

    n = rf.shape[1]
    ti = lax.broadcasted_iota(jnp.int32, (n, n), 0)
    tj = lax.broadcasted_iota(jnp.int32, (n, n), 1)

    def decayed(r_ref, lw_ref, kd_ref, v_ref, kk_ref, a_ref, fwd):
        lw = lw_ref[0, 0]
        kd = kd_ref[0, 0]
        kk = kk_ref[0]
        b = kk * a_ref[0, 0]
        incl = (tj <= ti) if fwd else (tj >= ti)
        c = _dot_exact_lhs(incl.astype(F32), lw)
        ctot = jnp.sum(lw, axis=0, keepdims=True)
        e_nc = jnp.exp(-c)
        e_tot = jnp.exp(ctot - c)
        return dict(rt=r_ref[0] * jnp.exp(c), at=kk * jnp.exp(c - lw), kt=kd * e_nc, bt=b * e_nc,
                    kh=kd * e_tot, bh=b * e_tot, v=v_ref[0], e=jnp.exp(ctot))

    pf = decayed(rf, lwf, kdf, vf, kkf, af, True)
    pb = decayed(rb, lwb, kdb, vb, kkb, ab, False)

    def chains(name):
        return jnp.concatenate([_split_heads(pf[name]), _split_heads(pb[name])], axis=0)

    rt, at, kt, bt, kh, bh, v = (chains(x) for x in ("rt", "at", "kt", "bt", "kh", "bh", "v"))
    ar = jnp.concatenate([at, rt], axis=1)
    kb = jnp.concatenate([kt, bt], axis=1)
    m = _bmm_x3(ar, kb, _BMM_NT)
    chain = lax.broadcasted_iota(jnp.int32, (RWKV_CHAINS, n, n), 0)
    lag = jnp.where(chain < A_HEADS, ti - tj, tj - ti)
    before = lag > 0
    before_eq = lag >= 0
    a_ak = jnp.where(before, m[:, :n, :n], 0.0)
    a_ab = jnp.where(before, m[:, :n, n:], 0.0)
    a_rk = jnp.where(before_eq, m[:, n:, :n], 0.0)
    a_rb = jnp.where(before_eq, m[:, n:, n:], 0.0)
    eye = (ti == tj).astype(F32)
    pw = -a_ab
    tinv = eye + pw
    for _ in range(int(math.log2(n)) - 1):
        pw = _bmm_x3(pw, pw, _BMM_NN)
        tinv = tinv + _bmm_x3(tinv, pw, _BMM_NN)
    sk = s_ref[...]
    a_s = _bmm(ar, sk, _BMM_NN)
    u = _bmm(tinv, a_s[:, :n] + _bmm(a_ak, v, _BMM_NN), _BMM_NN)
    vu = jnp.concatenate([v, -u], axis=1)
    out = a_s[:, n:] + _bmm(jnp.concatenate([a_rk, a_rb], axis=2), vu, _BMM_NN)
    e_rows = jnp.concatenate([_split_heads(pf["e"]), _split_heads(pb["e"])], axis=0)
    e_col = jnp.sum(eye * e_rows, axis=2, keepdims=True)
    s_ref[...] = sk * e_col + _bmm(jnp.concatenate([kh, bh], axis=1), vu, _BMM_TN)
    for h in range(A_HEADS):
        sl = slice(h * A_HEAD_DIM, (h + 1) * A_HEAD_DIM)
        of[0, :, sl] = out[h]
        ob[0, :, sl] = out[A_HEADS + h]


def _rwkv_scan(r, lw, kd, v, kk, a, ctx_len):
    bz, t, _ = r.shape
    n_ctx = ctx_len // RWKV_CHUNK
    n_chunks = t // RWKV_CHUNK
    rev = lambda i: _seg_order(i, 1, n_ctx, n_chunks)
    tok_f = pl.BlockSpec((1, RWKV_CHUNK, A_W), lambda b, i: (b, i, 0))
    tok_b = pl.BlockSpec((1, RWKV_CHUNK, A_W), lambda b, i: (b, rev(i), 0))
    dir_f = pl.BlockSpec((1, 1, RWKV_CHUNK, A_W), lambda b, i: (0, b, i, 0))
    dir_b = pl.BlockSpec((1, 1, RWKV_CHUNK, A_W), lambda b, i: (1, b, rev(i), 0))
    shp = jax.ShapeDtypeStruct((bz, t, A_W), F32)
    return pl.pallas_call(
        _rwkv_scan_kernel,
        grid=(bz, n_chunks),
        in_specs=[tok_f, dir_f, dir_f, tok_f, tok_f, dir_f, tok_b, dir_b, dir_b, tok_b, tok_b, dir_b],
        out_specs=[tok_f, tok_b],
        out_shape=[shp, shp],
        scratch_shapes=[pltpu.VMEM((RWKV_CHAINS, A_HEAD_DIM, A_HEAD_DIM), F32)],
        compiler_params=_params("parallel", "arbitrary"),
        name="rwkv_scan",
    )(r, lw, kd, v, kk, a, r, lw, kd, v, kk, a)


def _rglru_kernel(p_ref, pv_ref, nx_ref, cw_ref, cb_ref, gw_ref, gbias_ref, lam_ref, o_ref,
                  a_scr, b_scr, h_scr, carry_ref, *, n_ctx_tiles, n_tiles):
    d = pl.program_id(1)
    i = pl.program_id(2)
    blk = _seg_order(i, d, n_ctx_tiles, n_tiles)

    @pl.when(i == 0)
    def _():
        carry_ref[...] = jnp.zeros_like(carry_ref)

    p = p_ref[0]
    tr = p.shape[0]
    x = p[:, :B_W]
    gate_in = p[:, B_W:]
    has_prev = jnp.logical_and(blk != 0, blk != n_ctx_tiles)
    has_next = jnp.logical_and(blk != n_ctx_tiles - 1, blk != n_tiles - 1)
    pv = jnp.where(has_prev, pv_ref[0][:, :B_W], 0.0)
    nx = jnp.where(has_next, nx_ref[0][:, :B_W], 0.0)
    row = lax.broadcasted_iota(jnp.int32, (tr, B_W), 0)
    pv_m1 = pv[7:8]
    pv_m2 = pv[6:7]
    nx_p1 = nx[0:1]
    x_m1 = jnp.where(row == 0, pv_m1, pltpu.roll(x, 1, 0))
    x_m2 = jnp.where(row == 0, pv_m2, jnp.where(row == 1, pv_m1, pltpu.roll(x, 2, 0)))
    x_p1 = jnp.where(row == tr - 1, nx_p1, pltpu.roll(x, tr - 1, 0))
    cw = cw_ref[...]
    u = cw[0:1] * x_m2 + cw[1:2] * x_m1 + cw[2:3] * x + cw[3:4] * x_p1 + cb_ref[...]
    gate = 0.5 * gate_in * (1.0 + jnp.tanh(math.sqrt(2.0 / math.pi) * (gate_in + 0.044715 * gate_in ** 3)))
    pre = _dot(u, gw_ref[0]) + gbias_ref[0]
    rec = _sigmoid(pre[:, :B_W])
    inp = _sigmoid(pre[:, B_W:])
    log_a = -B_C * rec * _softplus(-lam_ref[0])
    a_scr[...] = jnp.exp(log_a)
    b_scr[...] = jnp.sqrt(1.0 - jnp.exp(2.0 * log_a)) * inp * u

    n_grp = tr // 8
    g8 = lax.broadcasted_iota(jnp.int32, (8, B_W), 0)

    def group_scan(a, b, fwd):
        for s in (1, 2, 4):
            if fwd:
                ident = g8 < s
                a_p = pltpu.roll(a, s, 0)
                b_p = pltpu.roll(b, s, 0)
            else:
                ident = g8 >= 8 - s
                a_p = pltpu.roll(a, 8 - s, 0)
                b_p = pltpu.roll(b, 8 - s, 0)
            a_p = jnp.where(ident, 1.0, a_p)
            b_p = jnp.where(ident, 0.0, b_p)
            b = a * b_p + b
            a = a * a_p
        return a, b

    def run(fwd):
        def body(j, carry):
            g = j if fwd else n_grp - 1 - j
            off = pl.multiple_of(g * 8, 8)
            a, b = group_scan(a_scr[pl.ds(off, 8), :], b_scr[pl.ds(off, 8), :], fwd)
            h = b + a * carry
            h_scr[pl.ds(off, 8), :] = h
            return h[7:8] if fwd else h[0:1]
        carry_ref[...] = lax.fori_loop(0, n_grp, body, carry_ref[...])

    @pl.when(d == 0)
    def _():
        run(True)

    @pl.when(d == 1)
    def _():
        run(False)

    o_ref[0, 0] = h_scr[...] * gate


def _rglru(pb, conv_w, conv_b, gate_w, gate_b, lam, n_ctx_tiles):
    bz, t, _ = pb.shape
    n_tiles = t // ROW_TILE
    r8 = ROW_TILE // 8
    n8 = t // 8
    eye = jnp.eye(B_BLOCKS, dtype=F32)
    gw = jnp.einsum('dsgio,gh->dgisho', gate_w, eye).reshape(2, B_W, 2 * B_W).astype(BF16)
    gbias = gate_b.reshape(2, 1, 2 * B_W)
    blk = lambda i, d: _seg_order(i, d, n_ctx_tiles, n_tiles)
    return pl.pallas_call(
        functools.partial(_rglru_kernel, n_ctx_tiles=n_ctx_tiles, n_tiles=n_tiles),
        grid=(bz, 2, n_tiles),
        in_specs=[pl.BlockSpec((1, ROW_TILE, B_IN), lambda b, d, i: (b, blk(i, d), 0)),
                  pl.BlockSpec((1, 8, B_IN), lambda b, d, i: (b, jnp.maximum(blk(i, d) * r8 - 1, 0), 0)),
                  pl.BlockSpec((1, 8, B_IN), lambda b, d, i: (b, jnp.minimum((blk(i, d) + 1) * r8, n8 - 1), 0)),
                  pl.BlockSpec((4, B_W), lambda b, d, i: (0, 0)),
                  pl.BlockSpec((1, B_W), lambda b, d, i: (0, 0)),
                  pl.BlockSpec((1, B_W, 2 * B_W), lambda b, d, i: (d, 0, 0)),
                  pl.BlockSpec((1, 1, 2 * B_W), lambda b, d, i: (d, 0, 0)),
                  pl.BlockSpec((1, 1, B_W), lambda b, d, i: (d, 0, 0))],
        out_specs=pl.BlockSpec((1, 1, ROW_TILE, B_W), lambda b, d, i: (d, b, blk(i, d), 0)),
        out_shape=jax.ShapeDtypeStruct((2, bz, t, B_W), F32),
        scratch_shapes=[pltpu.VMEM((ROW_TILE, B_W), F32), pltpu.VMEM((ROW_TILE, B_W), F32),
                        pltpu.VMEM((ROW_TILE, B_W), F32), pltpu.VMEM((1, B_W), F32)],
        compiler_params=_params("parallel", "parallel", "arbitrary"),
        name="rglru",
    )(pb, pb, pb, conv_w, conv_b.reshape(1, B_W), gw, gbias, lam.reshape(2, 1, B_W))


def _retention_kernel(lg_ref, p_ref, cos_ref, sin_ref, o_ref, st_ref):
    d = pl.program_id(1)
    i = pl.program_id(2)

    @pl.when(i == 0)
    def _():
        st_ref[...] = jnp.zeros_like(st_ref)

    n = p_ref.shape[1]
    cos2 = cos_ref[...]
    sin2 = sin_ref[...]
    ti = lax.broadcasted_iota(jnp.int32, (n, n), 0)
    tj = lax.broadcasted_iota(jnp.int32, (n, n), 1)
    fwd = d == 0
    diff = jnp.where(fwd, ti - tj, tj - ti).astype(F32)
    t1 = lax.broadcasted_iota(jnp.int32, (n, 1), 0)
    pos = jnp.where(fwd, t1, n - 1 - t1).astype(F32)
    for h in range(C_HEADS):
        lg = lg_ref[d, h] * jnp.ones((1, 1), F32)
        sl = slice(h * C_HEAD_DIM, (h + 1) * C_HEAD_DIM)
        q = p_ref[0, :, sl]
        k = p_ref[0, :, C_W + h * C_HEAD_DIM:C_W + (h + 1) * C_HEAD_DIM]
        v = p_ref[0, :, 2 * C_W + h * C_HEAD_DIM:2 * C_W + (h + 1) * C_HEAD_DIM]
        q = q * cos2 + pltpu.roll(q, C_HEAD_DIM // 2, 1) * sin2
        k = (k * cos2 + pltpu.roll(k, C_HEAD_DIM // 2, 1) * sin2) * (C_HEAD_DIM ** -0.5)
        decay = jnp.where(diff >= 0, jnp.exp(lg * jnp.maximum(diff, 0.0)), 0.0)
        scores = _dot_nt(q, k) * decay
        st = st_ref[h]
        out = _dot(scores, v) + _dot(q, st) * jnp.exp(lg * (pos + 1.0))
        zeta = jnp.exp(lg * (n - 1.0 - pos))
        st_ref[h] = st * jnp.exp(lg * n) + _dot_tn(k * zeta, v)
        o_ref[0, 0, :, sl] = out


def _retention(pc, log_gamma, cos2, sin2, ctx_len):
    bz, t, _ = pc.shape
    n_ctx = ctx_len // ODD_CHUNK
    n_chunks = t // ODD_CHUNK
    blk = lambda i, d: _seg_order(i, d, n_ctx, n_chunks)
    return pl.pallas_call(
        _retention_kernel,
        grid_spec=pltpu.PrefetchScalarGridSpec(
            num_scalar_prefetch=1,
            grid=(bz, 2, n_chunks),
            in_specs=[pl.BlockSpec((1, ODD_CHUNK, 3 * C_W), lambda b, d, i, lg: (b, blk(i, d), 0)),
                      pl.BlockSpec((ODD_CHUNK, C_HEAD_DIM), lambda b, d, i, lg: (blk(i, d), 0)),
                      pl.BlockSpec((ODD_CHUNK, C_HEAD_DIM), lambda b, d, i, lg: (blk(i, d), 0))],
            out_specs=pl.BlockSpec((1, 1, ODD_CHUNK, C_W), lambda b, d, i, lg: (d, b, blk(i, d), 0)),
            scratch_shapes=[pltpu.VMEM((C_HEADS, C_HEAD_DIM, C_HEAD_DIM), F32)]),
        out_shape=jax.ShapeDtypeStruct((2, bz, t, C_W), F32),
        compiler_params=_params("parallel", "parallel", "arbitrary"),
        name="retention",
    )(log_gamma, pc, cos2, sin2)


def _mlstm_kernel(p_ref, bias_ref, o_ref, c_ref, n_ref, m_ref):
    d = pl.program_id(1)
    i = pl.program_id(2)

    @pl.when(i == 0)
    def _():
        c_ref[...] = jnp.zeros_like(c_ref)
        n_ref[...] = jnp.zeros_like(n_ref)
        m_ref[...] = jnp.full_like(m_ref, -jnp.inf)

    n = p_ref.shape[1]
    ti = lax.broadcasted_iota(jnp.int32, (n, n), 0)
    tj = lax.broadcasted_iota(jnp.int32, (n, n), 1)
    fwd = d == 0
    before_eq = jnp.where(fwd, ti - tj, tj - ti) >= 0
    gts = p_ref[0, :, 4 * D_W:] + bias_ref[...]
    gts = jnp.where(fwd, gts, pltpu.roll(gts, LANES - D_HEADS, 1))
    lf = jnp.minimum(gts, 0.0) - jnp.log(1.0 + jnp.exp(-jnp.abs(gts)))
    bcum = _dot_exact_lhs(before_eq.astype(F32), lf)
    btot = jnp.sum(lf, axis=0, keepdims=True)
    gts_t = gts.T
    bcum_t = bcum.T
    for h in range(D_HEADS):
        sl = slice(h * D_HEAD_DIM, (h + 1) * D_HEAD_DIM)
        q = p_ref[0, :, sl]
        k = p_ref[0, :, D_W + h * D_HEAD_DIM:D_W + (h + 1) * D_HEAD_DIM] * (D_HEAD_DIM ** -0.5)
        v = p_ref[0, :, 2 * D_W + h * D_HEAD_DIM:2 * D_W + (h + 1) * D_HEAD_DIM]
        ig_col = gts[:, h:h + 1]
        ig_row = gts_t[h:h + 1, :]
        b_col = bcum[:, 2 * D_HEADS + h:2 * D_HEADS + h + 1]
        b_row = bcum_t[2 * D_HEADS + h:2 * D_HEADS + h + 1, :]
        b_end = btot[:, 2 * D_HEADS + h:2 * D_HEADS + h + 1]
        m_prev = m_ref[h][0:1, 0:1]
        d_log = jnp.where(before_eq, b_col - b_row + ig_row, -jnp.inf)
        g_inter = b_col + m_prev
        m_t = jnp.maximum(g_inter, jnp.max(d_log, axis=1, keepdims=True))
        s_intra = _dot_nt(q, k) * jnp.exp(d_log - m_t)
        s_inter = jnp.exp(g_inter - m_t)
        num = _dot(s_intra, v) + _dot(q, c_ref[h]) * s_inter
        den = jnp.sum(s_intra, axis=1, keepdims=True) + jnp.sum(q * n_ref[h], axis=1, keepdims=True) * s_inter
        den = jnp.maximum(jnp.abs(den), jnp.exp(-m_t))
        o_ref[0, 0, :, sl] = num / den
        w_end = b_end - b_col + ig_col
        m_i = jnp.max(w_end, axis=0, keepdims=True)
        e_end = jnp.exp(w_end - m_i)
        ke = k * e_end
        m_new = jnp.maximum(b_end + m_prev, m_i)
        s_old = jnp.exp(b_end + m_prev - m_new)
        s_new = jnp.exp(m_i - m_new)
        c_ref[h] = c_ref[h] * s_old + _dot_tn(ke, v) * s_new
        n_ref[h] = n_ref[h] * s_old + jnp.sum(ke, axis=0, keepdims=True) * s_new
        m_ref[h] = jnp.broadcast_to(m_new, m_ref.shape[1:])


def _mlstm(pd, bias128, ctx_len):
    bz, t, w = pd.shape
    n_ctx = ctx_len // ODD_CHUNK
    n_chunks = t // ODD_CHUNK
    blk = lambda i, d: _seg_order(i, d, n_ctx, n_chunks)
    return pl.pallas_call(
        _mlstm_kernel,
        grid=(bz, 2, n_chunks),
        in_specs=[pl.BlockSpec((1, ODD_CHUNK, w), lambda b, d, i: (b, blk(i, d), 0)),
                  pl.BlockSpec((1, LANES), lambda b, d, i: (0, 0))],
        out_specs=pl.BlockSpec((1, 1, ODD_CHUNK, D_W), lambda b, d, i: (d, b, blk(i, d), 0)),
        out_shape=jax.ShapeDtypeStruct((2, bz, t, D_W), F32),
        scratch_shapes=[pltpu.VMEM((D_HEADS, D_HEAD_DIM, D_HEAD_DIM), F32),
                        pltpu.VMEM((D_HEADS, 1, D_HEAD_DIM), F32),
                        pltpu.VMEM((D_HEADS, 8, LANES), F32)],
        compiler_params=_params("parallel", "parallel", "arbitrary"),
        name="mlstm",
    )(pd, bias128)


def _group_norm_lanes(x, n_groups, eps):
    gw = x.shape[1] // n_groups
    outs = []
    for g in range(n_groups):
        xg = x[:, g * gw:(g + 1) * gw]
        mu = jnp.mean(xg, axis=1, keepdims=True)
        xc = xg - mu
        var = jnp.mean(xc * xc, axis=1, keepdims=True)
        outs.append(xc * lax.rsqrt(var + eps))
    return jnp.concatenate(outs, axis=1)


def _post_mixer(ycat, x_ref, wout_ref, mods_ref, lng_ref, lnb_ref, rw_ref, rb_ref, x1_o, h2_o, topi_o, topw_o, alpha):
    y = jnp.dot(ycat.astype(BF16), wout_ref[...], preferred_element_type=F32)
    mods = mods_ref[0, 0]
    x1 = alpha * x_ref[0] + mods[2:3] * y
    mu = jnp.mean(x1, axis=1, keepdims=True)
    xc = x1 - mu
    var = jnp.mean(xc * xc, axis=1, keepdims=True)
    x1 = xc * lax.rsqrt(var + LN_EPS) * lng_ref[...] + lnb_ref[...]
    x1_o[0] = x1
    h2 = x1 * (1.0 + mods[4:5]) + mods[3:4]
    h2_o[0] = h2
    logits = _dot_x3(h2, rw_ref[...]) + rb_ref[...]
    lane = lax.broadcasted_iota(jnp.int32, logits.shape, 1)
    work = jnp.where(lane < N_EXPERTS, logits, -jnp.inf)
    top_i = jnp.zeros(logits.shape, jnp.int32)
    top_v = jnp.full(logits.shape, -jnp.inf, F32)
    top = None
    for kth in range(TOP_K):
        mk = jnp.max(work, axis=1, keepdims=True)
        idx = jnp.min(jnp.where(work == mk, lane, LANES), axis=1, keepdims=True)
        work = jnp.where(lane == idx, -jnp.inf, work)
        top_i = jnp.where(lane == kth, idx, top_i)
        top_v = jnp.where(lane == kth, mk, top_v)
        if kth == 0:
            top = mk
    e = jnp.exp(top_v - top)
    topi_o[0] = top_i
    topw_o[0] = e / jnp.sum(e, axis=1, keepdims=True)


def _outproj_even_kernel(wkvf_ref, wkvb_ref, bonus_ref, g_ref, gng_ref, gnb_ref, hb_ref,
                         x_ref, wout_ref, mods_ref, lng_ref, lnb_ref, rw_ref, rb_ref,
                         x1_o, h2_o, topi_o, topw_o, *, alpha):
    wkv = wkvf_ref[0] + wkvb_ref[0]
    ya = (_group_norm_lanes(wkv, A_HEADS, GN_EPS_RWKV) * gng_ref[...] + gnb_ref[...] + bonus_ref[0]) * g_ref[0]
    yb = hb_ref[0, 0] + hb_ref[1, 0]
    ycat = jnp.concatenate([ya, yb], axis=1)
    _post_mixer(ycat, x_ref, wout_ref, mods_ref, lng_ref, lnb_ref, rw_ref, rb_ref, x1_o, h2_o, topi_o, topw_o, alpha)


def _outproj_odd_kernel(oc_ref, pc_ref, cg_ref, cb_ref, od_ref, pd_ref, dg_ref, db_ref,
                        x_ref, wout_ref, mods_ref, lng_ref, lnb_ref, rw_ref, rb_ref,
                        x1_o, h2_o, topi_o, topw_o, *, alpha):
    oc = oc_ref[0, 0] + oc_ref[1, 0]
    gc = pc_ref[0]
    ya = (_group_norm_lanes(oc, C_HEADS, LN_EPS) * cg_ref[...] + cb_ref[...]) * (gc * _sigmoid(gc))
    od = od_ref[0, 0] + od_ref[1, 0]
    yb = (_group_norm_lanes(od, D_HEADS, LN_EPS) * dg_ref[...] + db_ref[...]) * _sigmoid(pd_ref[0])
    ycat = jnp.concatenate([ya, yb], axis=1)
    _post_mixer(ycat, x_ref, wout_ref, mods_ref, lng_ref, lnb_ref, rw_ref, rb_ref, x1_o, h2_o, topi_o, topw_o, alpha)


def _outproj(kind, mixer_args, mixer_specs, x, wout, mods, ln_g, ln_b, rw, rb, n_ctx_tiles, alpha, first_tile):
    bz, t, d = x.shape
    t_out = t - first_tile * ROW_TILE
    seg = lambda i: (i + first_tile >= n_ctx_tiles).astype(jnp.int32)
    row = lambda a: a.reshape(1, -1)
    vec = lambda n: pl.BlockSpec((1, n), lambda b, i: (0, 0))
    kern = _outproj_even_kernel if kind == 'even' else _outproj_odd_kernel
    tok = lambda n: pl.BlockSpec((1, ROW_TILE, n), lambda b, i: (b, i, 0))
    return pl.pallas_call(
        functools.partial(kern, alpha=alpha),
        grid=(bz, t_out // ROW_TILE),
        in_specs=list(mixer_specs) + [
            pl.BlockSpec((1, ROW_TILE, d), lambda b, i: (b, i + first_tile, 0)),
            pl.BlockSpec((d, d), lambda b, i: (0, 0)),
            pl.BlockSpec((1, 1, 8, d), lambda b, i: (b, seg(i), 0, 0)),
            vec(d), vec(d),
            pl.BlockSpec((d, LANES), lambda b, i: (0, 0)),
            vec(LANES)],
        out_specs=[tok(d), tok(d), tok(LANES), tok(LANES)],
        out_shape=[jax.ShapeDtypeStruct((bz, t_out, d), F32),
                   jax.ShapeDtypeStruct((bz, t_out, d), F32),
                   jax.ShapeDtypeStruct((bz, t_out, LANES), jnp.int32),
                   jax.ShapeDtypeStruct((bz, t_out, LANES), F32)],
        compiler_params=_params("parallel", "parallel"),
        name="outproj_" + kind,
    )(*mixer_args, x, wout, mods, row(ln_g), row(ln_b), rw, rb)


def _route(top_i):
    e_flat = top_i[..., :TOP_K].reshape(-1)
    n_asg = e_flat.shape[0]
    onehot = (e_flat[:, None] == jnp.arange(N_EXPERTS, dtype=jnp.int32)).astype(jnp.int32)
    incl = jnp.cumsum(onehot, axis=0)
    rank = jnp.take_along_axis(incl, e_flat[:, None], axis=1)[:, 0] - 1
    cnt = incl[-1]
    start = jnp.cumsum(cnt) - cnt
    dest = start[e_flat] + rank
    first_t = start // EXPERT_TILE
    last_t = (start + cnt - 1) // EXPERT_TILE
    n_t = jnp.where(cnt > 0, last_t - first_t + 1, 0)
    w_end = jnp.cumsum(n_t)
    w_start = w_end - n_t
    n_work = w_end[-1]
    w = jnp.minimum(jnp.arange(n_asg // EXPERT_TILE + N_EXPERTS - 1, dtype=jnp.int32), n_work - 1)
    we = jnp.searchsorted(w_end, w, side='right').astype(jnp.int32)
    wt = first_t[we] + (w - w_start[we])
    lo = start[we]
    hi = lo + cnt[we]
    return dest.astype(jnp.int32), (wt.astype(jnp.int32), we, lo.astype(jnp.int32), hi.astype(jnp.int32),
                                     n_work.reshape(1).astype(jnp.int32))


def _dispatch_kernel(dest_ref, h_ref, xs_hbm, sem):
    i = pl.program_id(0)
    rows = h_ref.shape[0]
    base = i * rows * TOP_K

    def issue(t, carry):
        for k in range(TOP_K):
            row = dest_ref[base + t * TOP_K + k]
            pltpu.make_async_copy(h_ref.at[pl.ds(t, 1)], xs_hbm.at[pl.ds(row, 1)], sem).start()
        return carry

    lax.fori_loop(0, rows, issue, 0)
    for _ in range(TOP_K):
        pltpu.make_async_copy(h_ref, xs_hbm.at[pl.ds(0, rows)], sem).wait()


def _dispatch(h2, dest):
    n, d = h2.shape
    return pl.pallas_call(
        _dispatch_kernel,
        grid_spec=pltpu.PrefetchScalarGridSpec(
            num_scalar_prefetch=1,
            grid=(n // ROW_TILE,),
            in_specs=[pl.BlockSpec((ROW_TILE, d), lambda i, dest: (i, 0))],
            out_specs=pl.BlockSpec(memory_space=pl.ANY),
            scratch_shapes=[pltpu.SemaphoreType.DMA(())]),
        out_shape=jax.ShapeDtypeStruct((n * TOP_K, d), F32),
        compiler_params=_params("arbitrary"),
        name="moe_dispatch",
    )(dest, h2)


def _experts_kernel(wt_ref, we_ref, lo_ref, hi_ref, nw_ref, xs_ref, w1_ref, b1_ref, w2_ref, b2_ref, y_ref,
                    w1b, w2b, *, f_chunk):
    w = pl.program_id(0)
    valid = w < nw_ref[0]
    prev = jnp.maximum(w - 1, 0)
    new_expert = jnp.logical_or(w == 0, we_ref[w] != we_ref[prev])
    new_tile = jnp.logical_or(w == 0, wt_ref[w] != wt_ref[prev])

    @pl.when(jnp.logical_and(valid, new_expert))
    def _():
        w1b[...] = w1_ref[0].astype(BF16)
        w2b[...] = w2_ref[0].astype(BF16)

    @pl.when(valid)
    def _():
        x = xs_ref[...].astype(BF16)
        acc = jnp.zeros(y_ref.shape, F32)
        for c in range(D_FF // f_chunk):
            c0 = c * f_chunk
            glu = jnp.dot(x, w1b[:, c0:c0 + f_chunk], preferred_element_type=F32) + b1_ref[0, :, c0:c0 + f_chunk]
            lin = (jnp.dot(x, w1b[:, D_FF + c0:D_FF + c0 + f_chunk], preferred_element_type=F32)
                   + b1_ref[0, :, D_FF + c0:D_FF + c0 + f_chunk])
            glu = jnp.minimum(glu, SWIGLU_LIMIT)
            lin = jnp.clip(lin, -SWIGLU_LIMIT, SWIGLU_LIMIT)
            act = (lin + 1.0) * glu * _sigmoid(SWIGLU_ALPHA * glu)
            acc += jnp.dot(act.astype(BF16), w2b[c0:c0 + f_chunk, :], preferred_element_type=F32)
        y = acc + b2_ref[0]
        rows = wt_ref[w] * EXPERT_TILE + lax.broadcasted_iota(jnp.int32, (EXPERT_TILE, 1), 0)
        mine = jnp.logical_and(rows >= lo_ref[w], rows < hi_ref[w])

        @pl.when(new_tile)
        def _():
            y_ref[...] = jnp.where(mine, y, 0.0)

        @pl.when(jnp.logical_not(new_tile))
        def _():
            y_ref[...] = jnp.where(mine, y, y_ref[...])


def _experts(xs, work, w1, b1, w2, b2):
    n_asg, d = xs.shape
    n_e = w1.shape[0]
    n_work_max = work[0].shape[0]
    return pl.pallas_call(
        functools.partial(_experts_kernel, f_chunk=512),
        grid_spec=pltpu.PrefetchScalarGridSpec(
            num_scalar_prefetch=5,
            grid=(n_work_max,),
            in_specs=[pl.BlockSpec((EXPERT_TILE, d), lambda w, wt, we, lo, hi, nw: (wt[w], 0)),
                      pl.BlockSpec((1, d, 2 * D_FF), lambda w, wt, we, lo, hi, nw: (we[w], 0, 0)),
                      pl.BlockSpec((1, 1, 2 * D_FF), lambda w, wt, we, lo, hi, nw: (we[w], 0, 0)),
                      pl.BlockSpec((1, D_FF, d), lambda w, wt, we, lo, hi, nw: (we[w], 0, 0)),
                      pl.BlockSpec((1, 1, d), lambda w, wt, we, lo, hi, nw: (we[w], 0, 0))],
            out_specs=pl.BlockSpec((EXPERT_TILE, d), lambda w, wt, we, lo, hi, nw: (wt[w], 0)),
            scratch_shapes=[pltpu.VMEM((d, 2 * D_FF), BF16), pltpu.VMEM((D_FF, d), BF16)]),
        out_shape=jax.ShapeDtypeStruct((n_asg, d), F32),
        compiler_params=_params("arbitrary"),
        name="moe_experts",
    )(*work, xs, w1, b1.reshape(n_e, 1, -1), w2, b2.reshape(n_e, 1, -1))


def _combine_kernel(dest_ref, y_hbm, topw_ref, x_ref, mods_ref, lng_ref, lnb_ref, o_ref, ybuf, sem, *, alpha):
    b = pl.program_id(0)
    i = pl.program_id(1)
    rows = x_ref.shape[1]
    base = (b * pl.num_programs(1) + i) * rows * TOP_K

    def issue(t, carry):
        for k in range(TOP_K):
            row = dest_ref[base + t * TOP_K + k]
            pltpu.make_async_copy(y_hbm.at[pl.ds(row, 1)], ybuf.at[k, pl.ds(t, 1)], sem).start()
        return carry

    lax.fori_loop(0, rows, issue, 0)
    for k in range(TOP_K):
        pltpu.make_async_copy(y_hbm.at[pl.ds(0, rows)], ybuf.at[k], sem).wait()
    tw = topw_ref[0]
    ffn = ybuf[0] * tw[:, 0:1]
    for k in range(1, TOP_K):
        ffn = ffn + ybuf[k] * tw[:, k:k + 1]
    mods = mods_ref[0, 0]
    x2 = alpha * x_ref[0] + mods[5:6] * ffn
    mu = jnp.mean(x2, axis=1, keepdims=True)
    xc = x2 - mu
    var = jnp.mean(xc * xc, axis=1, keepdims=True)
    o_ref[0] = xc * lax.rsqrt(var + LN_EPS) * lng_ref[...] + lnb_ref[...]


def _combine(y_sorted, dest, topw, x1, mods, ln_g, ln_b, n_ctx_tiles, alpha, first_tile):
    bz, t, d = x1.shape
    seg = lambda i: (i + first_tile >= n_ctx_tiles).astype(jnp.int32)
    tok = lambda n: pl.BlockSpec((1, ROW_TILE, n), lambda b, i, dest: (b, i, 0))
    vec = pl.BlockSpec((1, d), lambda b, i, dest: (0, 0))
    return pl.pallas_call(
        functools.partial(_combine_kernel, alpha=alpha),
        grid_spec=pltpu.PrefetchScalarGridSpec(
            num_scalar_prefetch=1,
            grid=(bz, t // ROW_TILE),
            in_specs=[pl.BlockSpec(memory_space=pl.ANY), tok(LANES), tok(d),
                      pl.BlockSpec((1, 1, 8, d), lambda b, i, dest: (b, seg(i), 0, 0)), vec, vec],
            out_specs=tok(d),
            scratch_shapes=[pltpu.VMEM((TOP_K, ROW_TILE, d), F32), pltpu.SemaphoreType.DMA(())]),
        out_shape=jax.ShapeDtypeStruct((bz, t, d), F32),
        compiler_params=_params("arbitrary", "arbitrary"),
        name="moe_combine",
    )(dest, y_sorted, topw, x1, mods, ln_g.reshape(1, d), ln_b.reshape(1, d))


def _rope_tables(ctx_len, seq):
    idx = jnp.arange(seq)
    rowp = (idx // GRID_W).astype(F32)
    colp = (idx % GRID_W).astype(F32)
    n_freq = C_HEAD_DIM // 4
    inv = ROPE_BASE ** (-jnp.arange(n_freq, dtype=F32) / n_freq)
    ang = jnp.concatenate([rowp[:, None] * inv, colp[:, None] * inv], -1)
    cos, sin = jnp.cos(ang), jnp.sin(ang)
    cos2 = jnp.concatenate([cos, cos], -1)
    sin2 = jnp.concatenate([-sin, sin], -1)
    cos2 = jnp.concatenate([jnp.ones((ctx_len, C_HEAD_DIM), F32), cos2], 0)
    sin2 = jnp.concatenate([jnp.zeros((ctx_len, C_HEAD_DIM), F32), sin2], 0)
    return cos2, sin2


def kernel(x, c, ctx, c_ctx, w_mod, b_mod, ln_g, ln_b, even_w_in, even_w_out, a_mu, a_w0, a_wB, a_a0, a_aB, a_gB, a_kk, a_ka, a_rk, a_gn_g, a_gn_b, b_conv_w, b_conv_b, b_gate_w, b_gate_b, b_lam, odd_w_in, odd_w_out, c_log_gamma, c_gn_g, c_gn_b, d_ibias, d_fbias, d_gn_g, d_gn_b, router_w, router_b, exp_w1, exp_b1, exp_w2, exp_b2):
    bz, seq, d = x.shape
    ctx_len = ctx.shape[1]
    depth = w_mod.shape[0]
    assert ctx_len == ROW_TILE and seq % ROW_TILE == 0
    t = ctx_len + seq
    n_ctx_tiles = ctx_len // ROW_TILE
    alpha = (2 * depth) ** 0.25
    xs = jnp.concatenate([ctx, x], axis=1)
    c8 = jnp.concatenate([c, c_ctx[None], jnp.zeros((8 - bz - 1, d), F32)], 0)
    cos2, sin2 = _rope_tables(ctx_len, seq)
    row = lambda a: a.reshape(1, -1)
    vec = lambda n: pl.BlockSpec((1, n), lambda b, i: (0, 0))

    for layer in range(depth):
        first = n_ctx_tiles if layer == depth - 1 else 0
        tok = lambda n, first=first: pl.BlockSpec((1, ROW_TILE, n), lambda b, i: (b, i + first, 0))
        tok2 = lambda n, first=first: pl.BlockSpec((2, 1, ROW_TILE, n), lambda b, i: (0, b, i + first, 0))
        m = _mod_vectors(c8, w_mod[layer], b_mod[layer]).reshape(8, N_MOD, d)
        m_lat = m[:bz]
        m_ctx = jnp.broadcast_to(m[bz][None], (bz, N_MOD, d))
        mods = jnp.stack([m_ctx, m_lat], axis=1)
        mods = jnp.concatenate([mods, jnp.zeros((bz, 2, 8 - N_MOD, d), F32)], axis=2)
        shift = mods[:, :, 0:1]
        scale = mods[:, :, 1:2]
        rw = jnp.pad(router_w[layer], ((0, 0), (0, LANES - N_EXPERTS)))
        rb = jnp.pad(router_b[layer], (0, LANES - N_EXPERTS)).reshape(1, LANES)
        li = layer // 2
        if layer % 2 == 0:
            pa, pb = _inproj(xs, shift, scale, even_w_in[li].astype(BF16), A_IN, n_ctx_tiles)
            r, v, kk, g, bonus, lw, kd, a = _rwkv_prep(pa, a_mu[li], a_w0[li], a_wB[li], a_a0[li], a_aB[li],
                                                       a_gB[li], a_kk[li], a_ka[li], a_rk[li].reshape(-1),
                                                       n_ctx_tiles)
            wkv_f, wkv_b = _rwkv_scan(r, lw, kd, v, kk, a, ctx_len)
            hb = _rglru(pb, b_conv_w[li], b_conv_b[li], b_gate_w[li], b_gate_b[li], b_lam[li], n_ctx_tiles)
            mixer_args = (wkv_f, wkv_b, bonus, g, row(a_gn_g[li]), row(a_gn_b[li]), hb)
            mixer_specs = (tok(A_W), tok(A_W), tok(A_W), tok(A_W), vec(A_W), vec(A_W), tok2(B_W))
            x1, h2, topi, topw = _outproj('even', mixer_args, mixer_specs, xs, even_w_out[li].astype(BF16), mods,
                                     ln_g[layer, 0], ln_b[layer, 0], rw, rb, n_ctx_tiles, alpha, first)
        else:
            w_in = jnp.pad(odd_w_in[li], ((0, 0), (0, D_IN_PAD - D_IN))).astype(BF16)
            pc, pd = _inproj(xs, shift, scale, w_in, C_IN, n_ctx_tiles)
            oc = _retention(pc, c_log_gamma[li], cos2, sin2, ctx_len)
            bias128 = jnp.concatenate([d_ibias[li].reshape(-1), d_fbias[li].reshape(-1),
                                       jnp.zeros((LANES - 4 * D_HEADS,), F32)]).reshape(1, LANES)
            od = _mlstm(pd, bias128, ctx_len)
            gate_c = pl.BlockSpec((1, ROW_TILE, C_W), lambda b, i, first=first: (b, i + first, 3))
            gate_d = pl.BlockSpec((1, ROW_TILE, D_W), lambda b, i, first=first: (b, i + first, 3))
            mixer_args = (oc, pc, row(c_gn_g[li]), row(c_gn_b[li]), od, pd, row(d_gn_g[li]), row(d_gn_b[li]))
            mixer_specs = (tok2(C_W), gate_c, vec(C_W), vec(C_W), tok2(D_W), gate_d, vec(D_W), vec(D_W))
            x1, h2, topi, topw = _outproj('odd', mixer_args, mixer_specs, xs, odd_w_out[li].astype(BF16), mods,
                                     ln_g[layer, 0], ln_b[layer, 0], rw, rb, n_ctx_tiles, alpha, first)
        dest, work = _route(topi)
        xs_sorted = _dispatch(h2.reshape(-1, d), dest)
        y_sorted = _experts(xs_sorted, work, exp_w1[layer], exp_b1[layer], exp_w2[layer], exp_b2[layer])
        xs = _combine(y_sorted, dest, topw, x1, mods, ln_g[layer, 1], ln_b[layer, 1], n_ctx_tiles, alpha, first)
    return xs[:, xs.shape[1] - seq:]
```

```python
import functools
import math

import jax
import jax.numpy as jnp
from jax import lax
from jax.experimental import pallas as pl
from jax.experimental.pallas import tpu as pltpu

F32 = jnp.float32
BF16 = jnp.bfloat16

GRID_W = 64
A_HEADS = 8
A_HEAD_DIM = 64
A_W = 512
A_DECAY_RANK = 64
A_ICL_RANK = 64
A_GATE_RANK = 128
A_IN = 1920
A_DECAY_SCALE = math.exp(-0.5)
GN_EPS_RWKV = 64e-5
B_W = 512
B_BLOCKS = 8
B_C = 8.0
B_IN = 1024
C_HEADS = 4
C_HEAD_DIM = 128
C_W = 512
C_IN = 2048
ROPE_BASE = 10000.0
D_HEADS = 4
D_HEAD_DIM = 128
D_W = 512
D_IN = 2064
D_IN_PAD = 2176
N_EXPERTS = 32
TOP_K = 4
D_FF = 1024
SWIGLU_ALPHA = 1.702
SWIGLU_LIMIT = 7.0
LN_EPS = 1e-5
N_MOD = 6
LANES = 128

ROW_TILE = 256
RWKV_CHUNK = 64
ODD_CHUNK = 256
EXPERT_TILE = 256
VMEM_LIMIT_BYTES = 56 * 1024 * 1024


def _params(*sem):
    return pltpu.CompilerParams(dimension_semantics=sem, vmem_limit_bytes=VMEM_LIMIT_BYTES)


def _dot(a, b):
    return jnp.dot(a.astype(BF16), b.astype(BF16), preferred_element_type=F32)


def _dot_nt(a, b):
    return lax.dot_general(a.astype(BF16), b.astype(BF16), (((1,), (1,)), ((), ())),
                           preferred_element_type=F32)


def _dot_tn(a, b):
    return lax.dot_general(a.astype(BF16), b.astype(BF16), (((0,), (0,)), ((), ())),
                           preferred_element_type=F32)


def _split(x):
    hi = x.astype(BF16)
    lo = (x - hi.astype(F32)).astype(BF16)
    return hi, lo


def _dot_x3(a, b):
    ah, al = _split(a)
    bh, bl = _split(b)
    out = jnp.dot(ah, bh, preferred_element_type=F32)
    out += jnp.dot(ah, bl, preferred_element_type=F32)
    out += jnp.dot(al, bh, preferred_element_type=F32)
    return out


def _dot_nt_x3(a, b):
    ah, al = _split(a)
    bh, bl = _split(b)
    dn = (((1,), (1,)), ((), ()))
    out = lax.dot_general(ah, bh, dn, preferred_element_type=F32)
    out += lax.dot_general(ah, bl, dn, preferred_element_type=F32)
    out += lax.dot_general(al, bh, dn, preferred_element_type=F32)
    return out


def _dot_exact_lhs(a, b):
    a = a.astype(BF16)
    b1 = b.astype(BF16)
    r1 = b - b1.astype(F32)
    b2 = r1.astype(BF16)
    b3 = (r1 - b2.astype(F32)).astype(BF16)
    out = jnp.dot(a, b1, preferred_element_type=F32)
    out += jnp.dot(a, b2, preferred_element_type=F32)
    out += jnp.dot(a, b3, preferred_element_type=F32)
    return out


def _sigmoid(x):
    return 1.0 / (1.0 + jnp.exp(-x))


def _softplus(x):
    return jnp.maximum(x, 0.0) + jnp.log(1.0 + jnp.exp(-jnp.abs(x)))


def _seg_order(i, d, n_ctx, n_tot):
    rev = jnp.where(i < n_ctx, n_ctx - 1 - i, n_ctx + n_tot - 1 - i)
    return jnp.where(d == 0, i, rev)


def _mod_kernel(c_ref, w_ref, b_ref, o_ref):
    c = c_ref[...]
    s = c * _sigmoid(c)
    o_ref[...] = _dot_x3(s, w_ref[...]) + b_ref[...]


def _mod_vectors(c8, w, b, layer):
    _, d, n = w.shape
    tn = n // 4
    return pl.pallas_call(
        _mod_kernel,
        grid=(n // tn,),
        in_specs=[pl.BlockSpec((8, d), lambda j: (0, 0)),
                  pl.BlockSpec((None, d, tn), lambda j: (layer, 0, j)),
                  pl.BlockSpec((1, tn), lambda j: (0, j))],
        out_specs=pl.BlockSpec((8, tn), lambda j: (0, j)),
        out_shape=jax.ShapeDtypeStruct((8, n), F32),
        compiler_params=_params("parallel"),
        name="mod_vectors",
    )(c8, w, b.reshape(1, n))


def _inproj_kernel(x_ref, sh_ref, sc_ref, w_ref, oa_ref, ob_ref, *, n_a):
    h = x_ref[0] * (1.0 + sc_ref[0, 0]) + sh_ref[0, 0]
    p = jnp.dot(h.astype(BF16), w_ref[...], preferred_element_type=F32)
    oa_ref[0] = p[:, :n_a]
    ob_ref[0] = p[:, n_a:]


def _inproj(x, shift, scale, w_bf16, n_a, n_ctx_tiles):
    bz, t, d = x.shape
    n = w_bf16.shape[1]
    n_b = n - n_a
    seg = lambda i: (i >= n_ctx_tiles).astype(jnp.int32)
    return pl.pallas_call(
        functools.partial(_inproj_kernel, n_a=n_a),
        grid=(bz, t // ROW_TILE),
        in_specs=[pl.BlockSpec((1, ROW_TILE, d), lambda b, i: (b, i, 0)),
                  pl.BlockSpec((1, 1, 1, d), lambda b, i: (b, seg(i), 0, 0)),
                  pl.BlockSpec((1, 1, 1, d), lambda b, i: (b, seg(i), 0, 0)),
                  pl.BlockSpec((d, n), lambda b, i: (0, 0))],
        out_specs=[pl.BlockSpec((1, ROW_TILE, n_a), lambda b, i: (b, i, 0)),
                   pl.BlockSpec((1, ROW_TILE, n_b), lambda b, i: (b, i, 0))],
        out_shape=[jax.ShapeDtypeStruct((bz, t, n_a), F32),
                   jax.ShapeDtypeStruct((bz, t, n_b), F32)],
        compiler_params=_params("parallel", "parallel"),
        name="inproj",
    )(x, shift, scale, w_bf16)


def _rwkv_prep_kernel(p_ref, up_ref, dn_ref, mu_ref, w0_ref, wb_ref, a0_ref, ab_ref, gb_ref,
                      kkw_ref, kaw_ref, rkw_ref, ones_ref,
                      r_o, v_o, kk_o, g_o, bonus_o, lw_o, kd_o, a_o, *, n_ctx_tiles, n_tiles):
    i = pl.program_id(1)
    p = p_ref[0]
    tr, w = p.shape
    row = lax.broadcasted_iota(jnp.int32, (tr, w), 0)
    lane = lax.broadcasted_iota(jnp.int32, (tr, w), 1)
    prev = pltpu.roll(p, 1, 0)
    nxt = pltpu.roll(p, tr - 1, 0)
    sh_ctx = jnp.where(lane % 2 == 0, jnp.where(row == 0, 0.0, prev), jnp.where(row == tr - 1, 0.0, nxt))
    col = row % GRID_W
    left = jnp.where(col == 0, 0.0, prev)
    right = jnp.where(col == GRID_W - 1, 0.0, nxt)
    up_halo = jnp.where(i > n_ctx_tiles, up_ref[0], 0.0)
    dn_halo = jnp.where(i < n_tiles - 1, dn_ref[0], 0.0)
    up = jnp.concatenate([up_halo, p[:tr - GRID_W]], axis=0)
    down = jnp.concatenate([p[GRID_W:], dn_halo], axis=0)
    slot = lane % 4
    sh_lat = jnp.where(slot == 0, left, jnp.where(slot == 1, right, jnp.where(slot == 2, up, down)))
    sh = jnp.where(i < n_ctx_tiles, sh_ctx, sh_lat)
    z = p + (sh - p) * mu_ref[...]

    r = z[:, 0:A_W]
    k = z[:, A_W:2 * A_W]
    v = z[:, 2 * A_W:3 * A_W]
    o = 3 * A_W
    wd = z[:, o:o + 2 * A_DECAY_RANK]
    o += 2 * A_DECAY_RANK
    ad = z[:, o:o + 2 * A_ICL_RANK]
    o += 2 * A_ICL_RANK
    zg = z[:, o:o + A_GATE_RANK]

    ones_bd = ones_ref[...]

    def head_sum(x):
        hi, lo = _split(x)
        return (jnp.dot(hi, ones_bd, preferred_element_type=F32)
                + jnp.dot(lo, ones_bd, preferred_element_type=F32))

    g_o[0] = _dot(_sigmoid(zg), gb_ref[...])
    kk = k * kkw_ref[...]
    kk = kk / jnp.maximum(jnp.sqrt(head_sum(kk * kk)), 1e-12)
    r_o[0] = r
    v_o[0] = v
    kk_o[0] = kk
    rr = r * rkw_ref[...]
    bsum = jnp.zeros_like(r)
    for d in range(2):
        wdd = jnp.tanh(wd[:, d * A_DECAY_RANK:(d + 1) * A_DECAY_RANK])
        lw = -A_DECAY_SCALE * _sigmoid(w0_ref[d] + _dot(wdd, wb_ref[d]))
        a = _sigmoid(a0_ref[d] + _dot(ad[:, d * A_ICL_RANK:(d + 1) * A_ICL_RANK], ab_ref[d]))
        kd = k * (1.0 + (a - 1.0) * kaw_ref[...])
        lw_o[d, 0] = lw
        kd_o[d, 0] = kd
        a_o[d, 0] = a
        bsum = bsum + head_sum(rr * kd)
    bonus_o[0] = bsum * v


def _rwkv_prep(pa, mu, w0, wb, a0, ab, gb, kkw, kaw, rkw, n_ctx_tiles):
    bz, t, _ = pa.shape
    n_tiles = t // ROW_TILE
    hb = ROW_TILE // GRID_W
    n_hblk = t // GRID_W
    ones_bd = jnp.kron(jnp.eye(A_HEADS, dtype=F32), jnp.ones((A_HEAD_DIM, A_HEAD_DIM), F32)).astype(BF16)
    row = lambda a: a.reshape(1, -1)
    tok = pl.BlockSpec((1, ROW_TILE, A_W), lambda b, i: (b, i, 0))
    tok2 = pl.BlockSpec((2, 1, ROW_TILE, A_W), lambda b, i: (0, b, i, 0))
    full = lambda a: pl.BlockSpec(a.shape, lambda b, i: (0,) * a.ndim)
    args = (row(mu), w0.reshape(2, 1, A_W), wb.astype(BF16), a0.reshape(2, 1, A_W), ab.astype(BF16),
            gb.astype(BF16), row(kkw), row(kaw), row(rkw), ones_bd)
    s1 = jax.ShapeDtypeStruct((bz, t, A_W), F32)
    s2 = jax.ShapeDtypeStruct((2, bz, t, A_W), F32)
    return pl.pallas_call(
        functools.partial(_rwkv_prep_kernel, n_ctx_tiles=n_ctx_tiles, n_tiles=n_tiles),
        grid=(bz, n_tiles),
        in_specs=[pl.BlockSpec((1, ROW_TILE, A_IN), lambda b, i: (b, i, 0)),
                  pl.BlockSpec((1, GRID_W, A_IN), lambda b, i: (b, jnp.maximum(i * hb - 1, 0), 0)),
                  pl.BlockSpec((1, GRID_W, A_IN), lambda b, i: (b, jnp.minimum((i + 1) * hb, n_hblk - 1), 0))]
                 + [full(a) for a in args],
        out_specs=[tok, tok, tok, tok, tok, tok2, tok2, tok2],
        out_shape=[s1, s1, s1, s1, s1, s2, s2, s2],
        compiler_params=_params("parallel", "parallel"),
        name="rwkv_prep",
    )(pa, pa, pa, *args)


_BMM_NN = (((2,), (1,)), ((0,), (0,)))
_BMM_NT = (((2,), (2,)), ((0,), (0,)))
_BMM_TN = (((1,), (1,)), ((0,), (0,)))
RWKV_CHAINS = 2 * A_HEADS


def _bmm(a, b, dims):
    return lax.dot_general(a.astype(BF16), b.astype(BF16), dims, preferred_element_type=F32)


def _bmm_x3(a, b, dims):
    ah, al = _split(a)
    bh, bl = _split(b)
    rows = a.shape[1]
    p = lax.dot_general(jnp.concatenate([ah, al], axis=1), bh, dims, preferred_element_type=F32)
    return p[:, :rows] + p[:, rows:] + lax.dot_general(ah, bl, dims, preferred_element_type=F32)


def _split_heads(x):
    return jnp.stack([x[:, h * A_HEAD_DIM:(h + 1) * A_HEAD_DIM] for h in range(A_HEADS)], axis=0)


def _rwkv_scan_kernel(rf, lwf, kdf, vf, kkf, af, rb, lwb, kdb, vb, kkb, ab, of, ob, s_ref):
    i = pl.program_id(1)

    @pl.when(i == 0)
    def _():
        s_ref[...] = jnp.zeros_like(s_ref)

    n = rf.shape[1]
    ti = lax.broadcasted_iota(jnp.int32, (n, n), 0)
    tj = lax.broadcasted_iota(jnp.int32, (n, n), 1)

    def decayed(r_ref, lw_ref, kd_ref, v_ref, kk_ref, a_ref, fwd):
        lw = lw_ref[0, 0]
        kd = kd_ref[0, 0]
        kk = kk_ref[0]
        b = kk * a_ref[0, 0]
        incl = (tj <= ti) if fwd else (tj >= ti)
        c = _dot_exact_lhs(incl.astype(F32), lw)
        ctot = jnp.sum(lw, axis=0, keepdims=True)
        e_nc = jnp.exp(-c)
        e_tot = jnp.exp(ctot - c)
        return dict(rt=r_ref[0] * jnp.exp(c), at=kk * jnp.exp(c - lw), kt=kd * e_nc, bt=b * e_nc,
                    kh=kd * e_tot, bh=b * e_tot, v=v_ref[0], e=jnp.exp(ctot))

    pf = decayed(rf, lwf, kdf, vf, kkf, af, True)
    pb = decayed(rb, lwb, kdb, vb, kkb, ab, False)

    def chains(name):
        return jnp.concatenate([_split_heads(pf[name]), _split_heads(pb[name])], axis=0)

    rt, at, kt, bt, kh, bh, v = (chains(x) for x in ("rt", "at", "kt", "bt", "kh", "bh", "v"))
    ar = jnp.concatenate([at, rt], axis=1)
    kb = jnp.concatenate([kt, bt], axis=1)
    m = _bmm_x3(ar, kb, _BMM_NT)
    chain = lax.broadcasted_iota(jnp.int32, (RWKV_CHAINS, n, n), 0)
    lag = jnp.where(chain < A_HEADS, ti - tj, tj - ti)
    before = lag > 0
    before_eq = lag >= 0
    a_ak = jnp.where(before, m[:, :n, :n], 0.0)
    a_ab = jnp.where(before, m[:, :n, n:], 0.0)
    a_rk = jnp.where(before_eq, m[:, n:, :n], 0.0)
    a_rb = jnp.where(before_eq, m[:, n:, n:], 0.0)
    eye = (ti == tj).astype(F32)
    pw = -a_ab
    tinv = eye + pw
    for _ in range(int(math.log2(n)) - 1):
        pw = _bmm_x3(pw, pw, _BMM_NN)
        tinv = tinv + _bmm_x3(tinv, pw, _BMM_NN)
    sk = s_ref[...]
    a_s = _bmm(ar, sk, _BMM_NN)
    u = _bmm(tinv, a_s[:, :n] + _bmm(a_ak, v, _BMM_NN), _BMM_NN)
    vu = jnp.concatenate([v, -u], axis=1)
    out = a_s[:, n:] + _bmm(jnp.concatenate([a_rk, a_rb], axis=2), vu, _BMM_NN)
    e_rows = jnp.concatenate([_split_heads(pf["e"]), _split_heads(pb["e"])], axis=0)
    e_col = jnp.sum(eye * e_rows, axis=2, keepdims=True)
    s_ref[...] = sk * e_col + _bmm(jnp.concatenate([kh, bh], axis=1), vu, _BMM_TN)
    for h in range(A_HEADS):
        sl = slice(h * A_HEAD_DIM, (h + 1) * A_HEAD_DIM)
        of[0, :, sl] = out[h]
        ob[0, :, sl] = out[A_HEADS + h]


def _rwkv_scan(r, lw, kd, v, kk, a, ctx_len):
    bz, t, _ = r.shape
    n_ctx = ctx_len // RWKV_CHUNK
    n_chunks = t // RWKV_CHUNK
    rev = lambda i: _seg_order(i, 1, n_ctx, n_chunks)
    tok_f = pl.BlockSpec((1, RWKV_CHUNK, A_W), lambda b, i: (b, i, 0))
    tok_b = pl.BlockSpec((1, RWKV_CHUNK, A_W), lambda b, i: (b, rev(i), 0))
    dir_f = pl.BlockSpec((1, 1, RWKV_CHUNK, A_W), lambda b, i: (0, b, i, 0))
    dir_b = pl.BlockSpec((1, 1, RWKV_CHUNK, A_W), lambda b, i: (1, b, rev(i), 0))
    shp = jax.ShapeDtypeStruct((bz, t, A_W), F32)
    return pl.pallas_call(
        _rwkv_scan_kernel,
        grid=(bz, n_chunks),
        in_specs=[tok_f, dir_f, dir_f, tok_f, tok_f, dir_f, tok_b, dir_b, dir_b, tok_b, tok_b, dir_b],
        out_specs=[tok_f, tok_b],
        out_shape=[shp, shp],
        scratch_shapes=[pltpu.VMEM((RWKV_CHAINS, A_HEAD_DIM, A_HEAD_DIM), F32)],
        compiler_params=_params("parallel", "arbitrary"),
        name="rwkv_scan",
    )(r, lw, kd, v, kk, a, r, lw, kd, v, kk, a)


def _rglru_kernel(p_ref, pv_ref, nx_ref, cw_ref, cb_ref, gw_ref, gbias_ref, lam_ref, o_ref,
                  a_scr, b_scr, h_scr, carry_ref, *, n_ctx_tiles, n_tiles):
    d = pl.program_id(1)
    i = pl.program_id(2)
    blk = _seg_order(i, d, n_ctx_tiles, n_tiles)

    @pl.when(i == 0)
    def _():
        carry_ref[...] = jnp.zeros_like(carry_ref)

    p = p_ref[0]
    tr = p.shape[0]
    x = p[:, :B_W]
    gate_in = p[:, B_W:]
    has_prev = jnp.logical_and(blk != 0, blk != n_ctx_tiles)
    has_next = jnp.logical_and(blk != n_ctx_tiles - 1, blk != n_tiles - 1)
    pv = jnp.where(has_prev, pv_ref[0][:, :B_W], 0.0)
    nx = jnp.where(has_next, nx_ref[0][:, :B_W], 0.0)
    row = lax.broadcasted_iota(jnp.int32, (tr, B_W), 0)
    pv_m1 = pv[7:8]
    pv_m2 = pv[6:7]
    nx_p1 = nx[0:1]
    x_m1 = jnp.where(row == 0, pv_m1, pltpu.roll(x, 1, 0))
    x_m2 = jnp.where(row == 0, pv_m2, jnp.where(row == 1, pv_m1, pltpu.roll(x, 2, 0)))
    x_p1 = jnp.where(row == tr - 1, nx_p1, pltpu.roll(x, tr - 1, 0))
    cw = cw_ref[...]
    u = cw[0:1] * x_m2 + cw[1:2] * x_m1 + cw[2:3] * x + cw[3:4] * x_p1 + cb_ref[...]
    gate = 0.5 * gate_in * (1.0 + jnp.tanh(math.sqrt(2.0 / math.pi) * (gate_in + 0.044715 * gate_in ** 3)))
    pre = _dot(u, gw_ref[0]) + gbias_ref[0]
    rec = _sigmoid(pre[:, :B_W])
    inp = _sigmoid(pre[:, B_W:])
    log_a = -B_C * rec * _softplus(-lam_ref[0])
    a_scr[...] = jnp.exp(log_a)
    b_scr[...] = jnp.sqrt(1.0 - jnp.exp(2.0 * log_a)) * inp * u

    n_grp = tr // 8
    g8 = lax.broadcasted_iota(jnp.int32, (8, B_W), 0)

    def group_scan(a, b, fwd):
        for s in (1, 2, 4):
            if fwd:
                ident = g8 < s
                a_p = pltpu.roll(a, s, 0)
                b_p = pltpu.roll(b, s, 0)
            else:
                ident = g8 >= 8 - s
                a_p = pltpu.roll(a, 8 - s, 0)
                b_p = pltpu.roll(b, 8 - s, 0)
            a_p = jnp.where(ident, 1.0, a_p)
            b_p = jnp.where(ident, 0.0, b_p)
            b = a * b_p + b
            a = a * a_p
        return a, b

    def run(fwd):
        def body(j, carry):
            g = j if fwd else n_grp - 1 - j
            off = pl.multiple_of(g * 8, 8)
            a, b = group_scan(a_scr[pl.ds(off, 8), :], b_scr[pl.ds(off, 8), :], fwd)
            h = b + a * carry
            h_scr[pl.ds(off, 8), :] = h
            return h[7:8] if fwd else h[0:1]
        carry_ref[...] = lax.fori_loop(0, n_grp, body, carry_ref[...])

    @pl.when(d == 0)
    def _():
        run(True)

    @pl.when(d == 1)
    def _():
        run(False)

    o_ref[0, 0] = h_scr[...] * gate


def _rglru(pb, conv_w, conv_b, gate_w, gate_b, lam, n_ctx_tiles):
    bz, t, _ = pb.shape
    n_tiles = t // ROW_TILE
    r8 = ROW_TILE // 8
    n8 = t // 8
    eye = jnp.eye(B_BLOCKS, dtype=F32)
    gw = jnp.einsum('dsgio,gh->dgisho', gate_w, eye).reshape(2, B_W, 2 * B_W).astype(BF16)
    gbias = gate_b.reshape(2, 1, 2 * B_W)
    blk = lambda i, d: _seg_order(i, d, n_ctx_tiles, n_tiles)
    return pl.pallas_call(
        functools.partial(_rglru_kernel, n_ctx_tiles=n_ctx_tiles, n_tiles=n_tiles),
        grid=(bz, 2, n_tiles),
        in_specs=[pl.BlockSpec((1, ROW_TILE, B_IN), lambda b, d, i: (b, blk(i, d), 0)),
                  pl.BlockSpec((1, 8, B_IN), lambda b, d, i: (b, jnp.maximum(blk(i, d) * r8 - 1, 0), 0)),
                  pl.BlockSpec((1, 8, B_IN), lambda b, d, i: (b, jnp.minimum((blk(i, d) + 1) * r8, n8 - 1), 0)),
                  pl.BlockSpec((4, B_W), lambda b, d, i: (0, 0)),
                  pl.BlockSpec((1, B_W), lambda b, d, i: (0, 0)),
                  pl.BlockSpec((1, B_W, 2 * B_W), lambda b, d, i: (d, 0, 0)),
                  pl.BlockSpec((1, 1, 2 * B_W), lambda b, d, i: (d, 0, 0)),
                  pl.BlockSpec((1, 1, B_W), lambda b, d, i: (d, 0, 0))],
        out_specs=pl.BlockSpec((1, 1, ROW_TILE, B_W), lambda b, d, i: (d, b, blk(i, d), 0)),
        out_shape=jax.ShapeDtypeStruct((2, bz, t, B_W), F32),
        scratch_shapes=[pltpu.VMEM((ROW_TILE, B_W), F32), pltpu.VMEM((ROW_TILE, B_W), F32),
                        pltpu.VMEM((ROW_TILE, B_W), F32), pltpu.VMEM((1, B_W), F32)],
        compiler_params=_params("parallel", "parallel", "arbitrary"),
        name="rglru",
    )(pb, pb, pb, conv_w, conv_b.reshape(1, B_W), gw, gbias, lam.reshape(2, 1, B_W))


def _retention_kernel(lg_ref, p_ref, cos_ref, sin_ref, o_ref, st_ref):
    d = pl.program_id(1)
    i = pl.program_id(2)

    @pl.when(i == 0)
    def _():
        st_ref[...] = jnp.zeros_like(st_ref)

    n = p_ref.shape[1]
    cos2 = cos_ref[...]
    sin2 = sin_ref[...]
    ti = lax.broadcasted_iota(jnp.int32, (n, n), 0)
    tj = lax.broadcasted_iota(jnp.int32, (n, n), 1)
    fwd = d == 0
    diff = jnp.where(fwd, ti - tj, tj - ti).astype(F32)
    t1 = lax.broadcasted_iota(jnp.int32, (n, 1), 0)
    pos = jnp.where(fwd, t1, n - 1 - t1).astype(F32)
    for h in range(C_HEADS):
        lg = lg_ref[d, h] * jnp.ones((1, 1), F32)
        sl = slice(h * C_HEAD_DIM, (h + 1) * C_HEAD_DIM)
        q = p_ref[0, :, sl]
        k = p_ref[0, :, C_W + h * C_HEAD_DIM:C_W + (h + 1) * C_HEAD_DIM]
        v = p_ref[0, :, 2 * C_W + h * C_HEAD_DIM:2 * C_W + (h + 1) * C_HEAD_DIM]
        q = q * cos2 + pltpu.roll(q, C_HEAD_DIM // 2, 1) * sin2
        k = (k * cos2 + pltpu.roll(k, C_HEAD_DIM // 2, 1) * sin2) * (C_HEAD_DIM ** -0.5)
        decay = jnp.where(diff >= 0, jnp.exp(lg * jnp.maximum(diff, 0.0)), 0.0)
        scores = _dot_nt(q, k) * decay
        st = st_ref[h]
        out = _dot(scores, v) + _dot(q, st) * jnp.exp(lg * (pos + 1.0))
        zeta = jnp.exp(lg * (n - 1.0 - pos))
        st_ref[h] = st * jnp.exp(lg * n) + _dot_tn(k * zeta, v)
        o_ref[0, 0, :, sl] = out


def _retention(pc, log_gamma, cos2, sin2, ctx_len):
    bz, t, _ = pc.shape
    n_ctx = ctx_len // ODD_CHUNK
    n_chunks = t // ODD_CHUNK
    blk = lambda i, d: _seg_order(i, d, n_ctx, n_chunks)
    return pl.pallas_call(
        _retention_kernel,
        grid_spec=pltpu.PrefetchScalarGridSpec(
            num_scalar_prefetch=1,
            grid=(bz, 2, n_chunks),
            in_specs=[pl.BlockSpec((1, ODD_CHUNK, 3 * C_W), lambda b, d, i, lg: (b, blk(i, d), 0)),
                      pl.BlockSpec((ODD_CHUNK, C_HEAD_DIM), lambda b, d, i, lg: (blk(i, d), 0)),
                      pl.BlockSpec((ODD_CHUNK, C_HEAD_DIM), lambda b, d, i, lg: (blk(i, d), 0))],
            out_specs=pl.BlockSpec((1, 1, ODD_CHUNK, C_W), lambda b, d, i, lg: (d, b, blk(i, d), 0)),
            scratch_shapes=[pltpu.VMEM((C_HEADS, C_HEAD_DIM, C_HEAD_DIM), F32)]),
        out_shape=jax.ShapeDtypeStruct((2, bz, t, C_W), F32),
        compiler_params=_params("parallel", "parallel", "arbitrary"),
        name="retention",
    )(log_gamma, pc, cos2, sin2)


def _mlstm_kernel(p_ref, bias_ref, o_ref, c_ref, n_ref, m_ref):
    d = pl.program_id(1)
    i = pl.program_id(2)

    @pl.when(i == 0)
    def _():
        c_ref[...] = jnp.zeros_like(c_ref)
        n_ref[...] = jnp.zeros_like(n_ref)
        m_ref[...] = jnp.full_like(m_ref, -jnp.inf)

    n = p_ref.shape[1]
    ti = lax.broadcasted_iota(jnp.int32, (n, n), 0)
    tj = lax.broadcasted_iota(jnp.int32, (n, n), 1)
    fwd = d == 0
    before_eq = jnp.where(fwd, ti - tj, tj - ti) >= 0
    gts = p_ref[0, :, 4 * D_W:] + bias_ref[...]
    gts = jnp.where(fwd, gts, pltpu.roll(gts, LANES - D_HEADS, 1))
    lf = jnp.minimum(gts, 0.0) - jnp.log(1.0 + jnp.exp(-jnp.abs(gts)))
    bcum = _dot_exact_lhs(before_eq.astype(F32), lf)
    btot = jnp.sum(lf, axis=0, keepdims=True)
    gts_t = gts.T
    bcum_t = bcum.T
    for h in range(D_HEADS):
        sl = slice(h * D_HEAD_DIM, (h + 1) * D_HEAD_DIM)
        q = p_ref[0, :, sl]
        k = p_ref[0, :, D_W + h * D_HEAD_DIM:D_W + (h + 1) * D_HEAD_DIM] * (D_HEAD_DIM ** -0.5)
        v = p_ref[0, :, 2 * D_W + h * D_HEAD_DIM:2 * D_W + (h + 1) * D_HEAD_DIM]
        ig_col = gts[:, h:h + 1]
        ig_row = gts_t[h:h + 1, :]
        b_col = bcum[:, 2 * D_HEADS + h:2 * D_HEADS + h + 1]
        b_row = bcum_t[2 * D_HEADS + h:2 * D_HEADS + h + 1, :]
        b_end = btot[:, 2 * D_HEADS + h:2 * D_HEADS + h + 1]
        m_prev = m_ref[h][0:1, 0:1]
        d_log = jnp.where(before_eq, b_col - b_row + ig_row, -jnp.inf)
        g_inter = b_col + m_prev
        m_t = jnp.maximum(g_inter, jnp.max(d_log, axis=1, keepdims=True))
        s_intra = _dot_nt(q, k) * jnp.exp(d_log - m_t)
        s_inter = jnp.exp(g_inter - m_t)
        num = _dot(s_intra, v) + _dot(q, c_ref[h]) * s_inter
        den = jnp.sum(s_intra, axis=1, keepdims=True) + jnp.sum(q * n_ref[h], axis=1, keepdims=True) * s_inter
        den = jnp.maximum(jnp.abs(den), jnp.exp(-m_t))
        o_ref[0, 0, :, sl] = num / den
        w_end = b_end - b_col + ig_col
        m_i = jnp.max(w_end, axis=0, keepdims=True)
        e_end = jnp.exp(w_end - m_i)
        ke = k * e_end
        m_new = jnp.maximum(b_end + m_prev, m_i)
        s_old = jnp.exp(b_end + m_prev - m_new)
        s_new = jnp.exp(m_i - m_new)
        c_ref[h] = c_ref[h] * s_old + _dot_tn(ke, v) * s_new
        n_ref[h] = n_ref[h] * s_old + jnp.sum(ke, axis=0, keepdims=True) * s_new
        m_ref[h] = jnp.broadcast_to(m_new, m_ref.shape[1:])


def _mlstm(pd, bias128, ctx_len):
    bz, t, w = pd.shape
    n_ctx = ctx_len // ODD_CHUNK
    n_chunks = t // ODD_CHUNK
    blk = lambda i, d: _seg_order(i, d, n_ctx, n_chunks)
    return pl.pallas_call(
        _mlstm_kernel,
        grid=(bz, 2, n_chunks),
        in_specs=[pl.BlockSpec((1, ODD_CHUNK, w), lambda b, d, i: (b, blk(i, d), 0)),
                  pl.BlockSpec((1, LANES), lambda b, d, i: (0, 0))],
        out_specs=pl.BlockSpec((1, 1, ODD_CHUNK, D_W), lambda b, d, i: (d, b, blk(i, d), 0)),
        out_shape=jax.ShapeDtypeStruct((2, bz, t, D_W), F32),
        scratch_shapes=[pltpu.VMEM((D_HEADS, D_HEAD_DIM, D_HEAD_DIM), F32),
                        pltpu.VMEM((D_HEADS, 1, D_HEAD_DIM), F32),
                        pltpu.VMEM((D_HEADS, 8, LANES), F32)],
        compiler_params=_params("parallel", "parallel", "arbitrary"),
        name="mlstm",
    )(pd, bias128)


def _group_norm_lanes(x, n_groups, eps):
    gw = x.shape[1] // n_groups
    outs = []
    for g in range(n_groups):
        xg = x[:, g * gw:(g + 1) * gw]
        mu = jnp.mean(xg, axis=1, keepdims=True)
        xc = xg - mu
        var = jnp.mean(xc * xc, axis=1, keepdims=True)
        outs.append(xc * lax.rsqrt(var + eps))
    return jnp.concatenate(outs, axis=1)


def _post_mixer(ycat, x_ref, wout_ref, mods_ref, lng_ref, lnb_ref, rw_ref, rb_ref, x1_o, h2_o, topi_o, topw_o, rank_o, cnt_o, cnt_scr, alpha):
    y = jnp.dot(ycat.astype(BF16), wout_ref[...], preferred_element_type=F32)
    mods = mods_ref[0, 0]
    x1 = alpha * x_ref[0] + mods[2:3] * y
    mu = jnp.mean(x1, axis=1, keepdims=True)
    xc = x1 - mu
    var = jnp.mean(xc * xc, axis=1, keepdims=True)
    x1 = xc * lax.rsqrt(var + LN_EPS) * lng_ref[...] + lnb_ref[...]
    x1_o[0] = x1
    h2 = x1 * (1.0 + mods[4:5]) + mods[3:4]
    h2_o[0] = h2
    logits = _dot_x3(h2, rw_ref[...]) + rb_ref[...]
    lane = lax.broadcasted_iota(jnp.int32, logits.shape, 1)
    work = jnp.where(lane < N_EXPERTS, logits, -jnp.inf)
    top_i = jnp.zeros(logits.shape, jnp.int32)
    top_v = jnp.full(logits.shape, -jnp.inf, F32)
    top = None
    picks = []
    for kth in range(TOP_K):
        mk = jnp.max(work, axis=1, keepdims=True)
        idx = jnp.min(jnp.where(work == mk, lane, LANES), axis=1, keepdims=True)
        pick = lane == idx
        picks.append(pick)
        work = jnp.where(pick, -jnp.inf, work)
        top_i = jnp.where(lane == kth, idx, top_i)
        top_v = jnp.where(lane == kth, mk, top_v)
        if kth == 0:
            top = mk
    e = jnp.exp(top_v - top)
    topi_o[0] = top_i
    topw_o[0] = e / jnp.sum(e, axis=1, keepdims=True)

    @pl.when(jnp.logical_and(pl.program_id(0) == 0, pl.program_id(1) == 0))
    def _():
        cnt_scr[...] = jnp.zeros_like(cnt_scr)

    rows = logits.shape[0]
    chosen = jnp.zeros(logits.shape, F32)
    for pick in picks:
        chosen = jnp.where(pick, 1.0, chosen)
    ti = lax.broadcasted_iota(jnp.int32, (rows, rows), 0)
    tj = lax.broadcasted_iota(jnp.int32, (rows, rows), 1)
    earlier = jnp.dot((tj < ti).astype(BF16), chosen.astype(BF16), preferred_element_type=F32) + cnt_scr[...]
    rank = jnp.zeros(logits.shape, F32)
    for kth, pick in enumerate(picks):
        rank = jnp.where(lane == kth, jnp.sum(jnp.where(pick, earlier, 0.0), axis=1, keepdims=True), rank)
    rank_o[0] = rank.astype(jnp.int32)
    cnt_scr[...] = cnt_scr[...] + jnp.sum(chosen, axis=0, keepdims=True)
    cnt_o[...] = cnt_scr[...].astype(jnp.int32)


def _outproj_even_kernel(wkvf_ref, wkvb_ref, bonus_ref, g_ref, gng_ref, gnb_ref, hb_ref,
                         x_ref, wout_ref, mods_ref, lng_ref, lnb_ref, rw_ref, rb_ref,
                         x1_o, h2_o, topi_o, topw_o, rank_o, cnt_o, cnt_scr, *, alpha):
    wkv = wkvf_ref[0] + wkvb_ref[0]
    ya = (_group_norm_lanes(wkv, A_HEADS, GN_EPS_RWKV) * gng_ref[...] + gnb_ref[...] + bonus_ref[0]) * g_ref[0]
    yb = hb_ref[0, 0] + hb_ref[1, 0]
    ycat = jnp.concatenate([ya, yb], axis=1)
    _post_mixer(ycat, x_ref, wout_ref, mods_ref, lng_ref, lnb_ref, rw_ref, rb_ref, x1_o, h2_o, topi_o, topw_o, rank_o, cnt_o, cnt_scr, alpha)


def _outproj_odd_kernel(oc_ref, pc_ref, cg_ref, cb_ref, od_ref, pd_ref, dg_ref, db_ref,
                        x_ref, wout_ref, mods_ref, lng_ref, lnb_ref, rw_ref, rb_ref,
                        x1_o, h2_o, topi_o, topw_o, rank_o, cnt_o, cnt_scr, *, alpha):
    oc = oc_ref[0, 0] + oc_ref[1, 0]
    gc = pc_ref[0]
    ya = (_group_norm_lanes(oc, C_HEADS, LN_EPS) * cg_ref[...] + cb_ref[...]) * (gc * _sigmoid(gc))
    od = od_ref[0, 0] + od_ref[1, 0]
    yb = (_group_norm_lanes(od, D_HEADS, LN_EPS) * dg_ref[...] + db_ref[...]) * _sigmoid(pd_ref[0])
    ycat = jnp.concatenate([ya, yb], axis=1)
    _post_mixer(ycat, x_ref, wout_ref, mods_ref, lng_ref, lnb_ref, rw_ref, rb_ref, x1_o, h2_o, topi_o, topw_o, rank_o, cnt_o, cnt_scr, alpha)


def _outproj(kind, mixer_args, mixer_specs, x, wout, mods, ln_g, ln_b, rw, rb, n_ctx_tiles, alpha, first_tile):
    bz, t, d = x.shape
    t_out = t - first_tile * ROW_TILE
    seg = lambda i: (i + first_tile >= n_ctx_tiles).astype(jnp.int32)
    row = lambda a: a.reshape(1, -1)
    vec = lambda n: pl.BlockSpec((1, n), lambda b, i: (0, 0))
    kern = _outproj_even_kernel if kind == 'even' else _outproj_odd_kernel
    tok = lambda n: pl.BlockSpec((1, ROW_TILE, n), lambda b, i: (b, i, 0))
    return pl.pallas_call(
        functools.partial(kern, alpha=alpha),
        grid=(bz, t_out // ROW_TILE),
        in_specs=list(mixer_specs) + [
            pl.BlockSpec((1, ROW_TILE, d), lambda b, i: (b, i + first_tile, 0)),
            pl.BlockSpec((d, d), lambda b, i: (0, 0)),
            pl.BlockSpec((1, 1, 8, d), lambda b, i: (b, seg(i), 0, 0)),
            vec(d), vec(d),
            pl.BlockSpec((d, LANES), lambda b, i: (0, 0)),
            vec(LANES)],
        out_specs=[tok(d), tok(d), tok(LANES), tok(LANES), tok(LANES), vec(LANES)],
        out_shape=[jax.ShapeDtypeStruct((bz, t_out, d), F32),
                   jax.ShapeDtypeStruct((bz, t_out, d), F32),
                   jax.ShapeDtypeStruct((bz, t_out, LANES), jnp.int32),
                   jax.ShapeDtypeStruct((bz, t_out, LANES), F32),
                   jax.ShapeDtypeStruct((bz, t_out, LANES), jnp.int32),
                   jax.ShapeDtypeStruct((1, LANES), jnp.int32)],
        scratch_shapes=[pltpu.VMEM((1, LANES), F32)],
        compiler_params=_params("arbitrary", "arbitrary"),
        name="outproj_" + kind,
    )(*mixer_args, x, wout, mods, row(ln_g), row(ln_b), rw, rb)


def _route(top_i, rank, counts):
    experts = jnp.arange(N_EXPERTS, dtype=jnp.int32)
    e_flat = top_i[..., :TOP_K].reshape(-1)
    n_asg = e_flat.shape[0]
    cnt = counts[0, :N_EXPERTS]
    start = jnp.cumsum(cnt) - cnt
    dest = rank[..., :TOP_K].reshape(-1) + jnp.sum(jnp.where(e_flat[:, None] == experts, start, 0), axis=1)
    first_t = start // EXPERT_TILE
    last_t = (start + cnt - 1) // EXPERT_TILE
    n_t = jnp.where(cnt > 0, last_t - first_t + 1, 0)
    w_end = jnp.cumsum(n_t)
    n_work = w_end[-1]
    w = jnp.minimum(jnp.arange(n_asg // EXPERT_TILE + N_EXPERTS - 1, dtype=jnp.int32), n_work - 1)
    we = jnp.sum((w_end[None, :] <= w[:, None]).astype(jnp.int32), axis=1)
    pick = we[:, None] == experts
    sel = lambda v: jnp.sum(jnp.where(pick, v, 0), axis=1)
    wt = sel(first_t) + (w - sel(w_end - n_t))
    lo = sel(start)
    hi = lo + sel(cnt)
    i32 = lambda v: v.astype(jnp.int32)
    return i32(dest), (i32(wt), i32(we), i32(lo), i32(hi), i32(n_work.reshape(1)))


def _dispatch_kernel(dest_ref, h_ref, xs_hbm, sem):
    i = pl.program_id(0)
    rows = h_ref.shape[0]
    base = i * rows * TOP_K

    def issue(t, carry):
        for k in range(TOP_K):
            row = dest_ref[base + t * TOP_K + k]
            pltpu.make_async_copy(h_ref.at[pl.ds(t, 1)], xs_hbm.at[pl.ds(row, 1)], sem).start()
        return carry

    lax.fori_loop(0, rows, issue, 0)
    for _ in range(TOP_K):
        pltpu.make_async_copy(h_ref, xs_hbm.at[pl.ds(0, rows)], sem).wait()


def _dispatch(h2, dest):
    n, d = h2.shape
    return pl.pallas_call(
        _dispatch_kernel,
        grid_spec=pltpu.PrefetchScalarGridSpec(
            num_scalar_prefetch=1,
            grid=(n // ROW_TILE,),
            in_specs=[pl.BlockSpec((ROW_TILE, d), lambda i, dest: (i, 0))],
            out_specs=pl.BlockSpec(memory_space=pl.ANY),
            scratch_shapes=[pltpu.SemaphoreType.DMA(())]),
        out_shape=jax.ShapeDtypeStruct((n * TOP_K, d), F32),
        compiler_params=_params("arbitrary"),
        name="moe_dispatch",
    )(dest, h2)


def _experts_kernel(wt_ref, we_ref, lo_ref, hi_ref, nw_ref, xs_ref, w1_ref, b1_ref, w2_ref, b2_ref, y_ref,
                    w1b, w2b, *, f_chunk):
    w = pl.program_id(0)
    valid = w < nw_ref[0]
    prev = jnp.maximum(w - 1, 0)
    new_expert = jnp.logical_or(w == 0, we_ref[w] != we_ref[prev])
    new_tile = jnp.logical_or(w == 0, wt_ref[w] != wt_ref[prev])

    @pl.when(jnp.logical_and(valid, new_expert))
    def _():
        w1b[...] = w1_ref[0].astype(BF16)
        w2b[...] = w2_ref[0].astype(BF16)

    @pl.when(valid)
    def _():
        x = xs_ref[...].astype(BF16)
        acc = jnp.zeros(y_ref.shape, F32)
        for c in range(D_FF // f_chunk):
            c0 = c * f_chunk
            glu = jnp.dot(x, w1b[:, c0:c0 + f_chunk], preferred_element_type=F32) + b1_ref[0, :, c0:c0 + f_chunk]
            lin = (jnp.dot(x, w1b[:, D_FF + c0:D_FF + c0 + f_chunk], preferred_element_type=F32)
                   + b1_ref[0, :, D_FF + c0:D_FF + c0 + f_chunk])
            glu = jnp.minimum(glu, SWIGLU_LIMIT)
            lin = jnp.clip(lin, -SWIGLU_LIMIT, SWIGLU_LIMIT)
            act = (lin + 1.0) * glu * _sigmoid(SWIGLU_ALPHA * glu)
            acc += jnp.dot(act.astype(BF16), w2b[c0:c0 + f_chunk, :], preferred_element_type=F32)
        y = acc + b2_ref[0]
        rows = wt_ref[w] * EXPERT_TILE + lax.broadcasted_iota(jnp.int32, (EXPERT_TILE, 1), 0)
        mine = jnp.logical_and(rows >= lo_ref[w], rows < hi_ref[w])

        @pl.when(new_tile)
        def _():
            y_ref[...] = jnp.where(mine, y, 0.0)

        @pl.when(jnp.logical_not(new_tile))
        def _():
            y_ref[...] = jnp.where(mine, y, y_ref[...])


def _experts(xs, work, w1, b1, w2, b2, layer):
    n_asg, d = xs.shape
    n_l, n_e = w1.shape[:2]
    n_work_max = work[0].shape[0]
    return pl.pallas_call(
        functools.partial(_experts_kernel, f_chunk=512),
        grid_spec=pltpu.PrefetchScalarGridSpec(
            num_scalar_prefetch=5,
            grid=(n_work_max,),
            in_specs=[pl.BlockSpec((EXPERT_TILE, d), lambda w, wt, we, lo, hi, nw: (wt[w], 0)),
                      pl.BlockSpec((None, 1, d, 2 * D_FF), lambda w, wt, we, lo, hi, nw: (layer, we[w], 0, 0)),
                      pl.BlockSpec((None, 1, 1, 2 * D_FF), lambda w, wt, we, lo, hi, nw: (layer, we[w], 0, 0)),
                      pl.BlockSpec((None, 1, D_FF, d), lambda w, wt, we, lo, hi, nw: (layer, we[w], 0, 0)),
                      pl.BlockSpec((None, 1, 1, d), lambda w, wt, we, lo, hi, nw: (layer, we[w], 0, 0))],
            out_specs=pl.BlockSpec((EXPERT_TILE, d), lambda w, wt, we, lo, hi, nw: (wt[w], 0)),
            scratch_shapes=[pltpu.VMEM((d, 2 * D_FF), BF16), pltpu.VMEM((D_FF, d), BF16)]),
        out_shape=jax.ShapeDtypeStruct((n_asg, d), F32),
        compiler_params=_params("arbitrary"),
        name="moe_experts",
    )(*work, xs, w1, b1.reshape(n_l, n_e, 1, -1), w2, b2.reshape(n_l, n_e, 1, -1))


def _combine_kernel(dest_ref, y_hbm, topw_ref, x_ref, mods_ref, lng_ref, lnb_ref, o_ref, ybuf, sem, *, alpha):
    b = pl.program_id(0)
    i = pl.program_id(1)
    rows = x_ref.shape[1]
    base = (b * pl.num_programs(1) + i) * rows * TOP_K

    def issue(t, carry):
        for k in range(TOP_K):
            row = dest_ref[base + t * TOP_K + k]
            pltpu.make_async_copy(y_hbm.at[pl.ds(row, 1)], ybuf.at[k, pl.ds(t, 1)], sem).start()
        return carry

    lax.fori_loop(0, rows, issue, 0)
    for k in range(TOP_K):
        pltpu.make_async_copy(y_hbm.at[pl.ds(0, rows)], ybuf.at[k], sem).wait()
    tw = topw_ref[0]
    ffn = ybuf[0] * tw[:, 0:1]
    for k in range(1, TOP_K):
        ffn = ffn + ybuf[k] * tw[:, k:k + 1]
    mods = mods_ref[0, 0]
    x2 = alpha * x_ref[0] + mods[5:6] * ffn
    mu = jnp.mean(x2, axis=1, keepdims=True)
    xc = x2 - mu
    var = jnp.mean(xc * xc, axis=1, keepdims=True)
    o_ref[0] = xc * lax.rsqrt(var + LN_EPS) * lng_ref[...] + lnb_ref[...]


def _combine(y_sorted, dest, topw, x1, mods, ln_g, ln_b, n_ctx_tiles, alpha, first_tile):
    bz, t, d = x1.shape
    seg = lambda i: (i + first_tile >= n_ctx_tiles).astype(jnp.int32)
    tok = lambda n: pl.BlockSpec((1, ROW_TILE, n), lambda b, i, dest: (b, i, 0))
    vec = pl.BlockSpec((1, d), lambda b, i, dest: (0, 0))
    return pl.pallas_call(
        functools.partial(_combine_kernel, alpha=alpha),
        grid_spec=pltpu.PrefetchScalarGridSpec(
            num_scalar_prefetch=1,
            grid=(bz, t // ROW_TILE),
            in_specs=[pl.BlockSpec(memory_space=pl.ANY), tok(LANES), tok(d),
                      pl.BlockSpec((1, 1, 8, d), lambda b, i, dest: (b, seg(i), 0, 0)), vec, vec],
            out_specs=tok(d),
            scratch_shapes=[pltpu.VMEM((TOP_K, ROW_TILE, d), F32), pltpu.SemaphoreType.DMA(())]),
        out_shape=jax.ShapeDtypeStruct((bz, t, d), F32),
        compiler_params=_params("arbitrary", "arbitrary"),
        name="moe_combine",
    )(dest, y_sorted, topw, x1, mods, ln_g.reshape(1, d), ln_b.reshape(1, d))


def _rope_tables(ctx_len, seq):
    idx = jnp.arange(seq)
    rowp = (idx // GRID_W).astype(F32)
    colp = (idx % GRID_W).astype(F32)
    n_freq = C_HEAD_DIM // 4
    inv = ROPE_BASE ** (-jnp.arange(n_freq, dtype=F32) / n_freq)
    ang = jnp.concatenate([rowp[:, None] * inv, colp[:, None] * inv], -1)
    cos, sin = jnp.cos(ang), jnp.sin(ang)
    cos2 = jnp.concatenate([cos, cos], -1)
    sin2 = jnp.concatenate([-sin, sin], -1)
    cos2 = jnp.concatenate([jnp.ones((ctx_len, C_HEAD_DIM), F32), cos2], 0)
    sin2 = jnp.concatenate([jnp.zeros((ctx_len, C_HEAD_DIM), F32), sin2], 0)
    return cos2, sin2


def kernel(x, c, ctx, c_ctx, w_mod, b_mod, ln_g, ln_b, even_w_in, even_w_out, a_mu, a_w0, a_wB, a_a0, a_aB, a_gB, a_kk, a_ka, a_rk, a_gn_g, a_gn_b, b_conv_w, b_conv_b, b_gate_w, b_gate_b, b_lam, odd_w_in, odd_w_out, c_log_gamma, c_gn_g, c_gn_b, d_ibias, d_fbias, d_gn_g, d_gn_b, router_w, router_b, exp_w1, exp_b1, exp_w2, exp_b2):
    bz, seq, d = x.shape
    ctx_len = ctx.shape[1]
    depth = w_mod.shape[0]
    assert ctx_len == ROW_TILE and seq % ROW_TILE == 0
    t = ctx_len + seq
    n_ctx_tiles = ctx_len // ROW_TILE
    alpha = (2 * depth) ** 0.25
    xs = jnp.concatenate([ctx, x], axis=1)
    c8 = jnp.concatenate([c, c_ctx[None], jnp.zeros((8 - bz - 1, d), F32)], 0)
    cos2, sin2 = _rope_tables(ctx_len, seq)
    row = lambda a: a.reshape(1, -1)
    vec = lambda n: pl.BlockSpec((1, n), lambda b, i: (0, 0))

    for layer in range(depth):
        first = n_ctx_tiles if layer == depth - 1 else 0
        tok = lambda n, first=first: pl.BlockSpec((1, ROW_TILE, n), lambda b, i: (b, i + first, 0))
        tok2 = lambda n, first=first: pl.BlockSpec((2, 1, ROW_TILE, n), lambda b, i: (0, b, i + first, 0))
        m = _mod_vectors(c8, w_mod, b_mod[layer], layer).reshape(8, N_MOD, d)
        m_lat = m[:bz]
        m_ctx = jnp.broadcast_to(m[bz][None], (bz, N_MOD, d))
        mods = jnp.stack([m_ctx, m_lat], axis=1)
        mods = jnp.concatenate([mods, jnp.zeros((bz, 2, 8 - N_MOD, d), F32)], axis=2)
        shift = mods[:, :, 0:1]
        scale = mods[:, :, 1:2]
        rw = jnp.pad(router_w[layer], ((0, 0), (0, LANES - N_EXPERTS)))
        rb = jnp.pad(router_b[layer], (0, LANES - N_EXPERTS)).reshape(1, LANES)
        li = layer // 2
        if layer % 2 == 0:
            pa, pb = _inproj(xs, shift, scale, even_w_in[li].astype(BF16), A_IN, n_ctx_tiles)
            r, v, kk, g, bonus, lw, kd, a = _rwkv_prep(pa, a_mu[li], a_w0[li], a_wB[li], a_a0[li], a_aB[li],
                                                       a_gB[li], a_kk[li], a_ka[li], a_rk[li].reshape(-1),
                                                       n_ctx_tiles)
            wkv_f, wkv_b = _rwkv_scan(r, lw, kd, v, kk, a, ctx_len)
            hb = _rglru(pb, b_conv_w[li], b_conv_b[li], b_gate_w[li], b_gate_b[li], b_lam[li], n_ctx_tiles)
            mixer_args = (wkv_f, wkv_b, bonus, g, row(a_gn_g[li]), row(a_gn_b[li]), hb)
            mixer_specs = (tok(A_W), tok(A_W), tok(A_W), tok(A_W), vec(A_W), vec(A_W), tok2(B_W))
            x1, h2, topi, topw, rank, counts = _outproj('even', mixer_args, mixer_specs, xs, even_w_out[li].astype(BF16), mods,
                                     ln_g[layer, 0], ln_b[layer, 0], rw, rb, n_ctx_tiles, alpha, first)
        else:
            w_in = jnp.pad(odd_w_in[li], ((0, 0), (0, D_IN_PAD - D_IN))).astype(BF16)
            pc, pd = _inproj(xs, shift, scale, w_in, C_IN, n_ctx_tiles)
            oc = _retention(pc, c_log_gamma[li], cos2, sin2, ctx_len)
            bias128 = jnp.concatenate([d_ibias[li].reshape(-1), d_fbias[li].reshape(-1),
                                       jnp.zeros((LANES - 4 * D_HEADS,), F32)]).reshape(1, LANES)
            od = _mlstm(pd, bias128, ctx_len)
            gate_c = pl.BlockSpec((1, ROW_TILE, C_W), lambda b, i, first=first: (b, i + first, 3))
            gate_d = pl.BlockSpec((1, ROW_TILE, D_W), lambda b, i, first=first: (b, i + first, 3))
            mixer_args = (oc, pc, row(c_gn_g[li]), row(c_gn_b[li]), od, pd, row(d_gn_g[li]), row(d_gn_b[li]))
            mixer_specs = (tok2(C_W), gate_c, vec(C_W), vec(C_W), tok2(D_W), gate_d, vec(D_W), vec(D_W))
            x1, h2, topi, topw, rank, counts = _outproj('odd', mixer_args, mixer_specs, xs, odd_w_out[li].astype(BF16), mods,
                                     ln_g[layer, 0], ln_b[layer, 0], rw, rb, n_ctx_tiles, alpha, first)
        dest, work = _route(topi, rank, counts)
        xs_sorted = _dispatch(h2.reshape(-1, d), dest)
        y_sorted = _experts(xs_sorted, work, exp_w1, exp_b1, exp_w2, exp_b2, layer)
        xs = _combine(y_sorted, dest, topw, x1, mods, ln_g[layer, 1], ln_b[layer, 1], n_ctx_tiles, alpha, first)
    return xs[:, xs.shape[1] - seq:]
```

```python
import functools
import math

import jax
import jax.numpy as jnp
from jax import lax
from jax.experimental import pallas as pl
from jax.experimental.pallas import tpu as pltpu

F32 = jnp.float32
BF16 = jnp.bfloat16

GRID_W = 64
A_HEADS = 8
A_HEAD_DIM = 64
A_W = 512
A_DECAY_RANK = 64
A_ICL_RANK = 64
A_GATE_RANK = 128
A_IN = 1920
A_DECAY_SCALE = math.exp(-0.5)
GN_EPS_RWKV = 64e-5
B_W = 512
B_BLOCKS = 8
B_C = 8.0
B_IN = 1024
C_HEADS = 4
C_HEAD_DIM = 128
C_W = 512
C_IN = 2048
ROPE_BASE = 10000.0
D_HEADS = 4
D_HEAD_DIM = 128
D_W = 512
D_IN = 2064
D_IN_PAD = 2176
N_EXPERTS = 32
TOP_K = 4
D_FF = 1024
SWIGLU_ALPHA = 1.702
SWIGLU_LIMIT = 7.0
LN_EPS = 1e-5
N_MOD = 6
LANES = 128

ROW_TILE = 256
RWKV_CHUNK = 64
ODD_CHUNK = 256
EXPERT_TILE = 256
VMEM_LIMIT_BYTES = 56 * 1024 * 1024


def _params(*sem):
    return pltpu.CompilerParams(dimension_semantics=sem, vmem_limit_bytes=VMEM_LIMIT_BYTES)


def _dot(a, b):
    return jnp.dot(a.astype(BF16), b.astype(BF16), preferred_element_type=F32)


def _dot_nt(a, b):
    return lax.dot_general(a.astype(BF16), b.astype(BF16), (((1,), (1,)), ((), ())),
                           preferred_element_type=F32)


def _dot_tn(a, b):
    return lax.dot_general(a.astype(BF16), b.astype(BF16), (((0,), (0,)), ((), ())),
                           preferred_element_type=F32)


def _split(x):
    hi = x.astype(BF16)
    lo = (x - hi.astype(F32)).astype(BF16)
    return hi, lo


def _dot_x3(a, b):
    ah, al = _split(a)
    bh, bl = _split(b)
    out = jnp.dot(ah, bh, preferred_element_type=F32)
    out += jnp.dot(ah, bl, preferred_element_type=F32)
    out += jnp.dot(al, bh, preferred_element_type=F32)
    return out


def _dot_nt_x3(a, b):
    ah, al = _split(a)
    bh, bl = _split(b)
    dn = (((1,), (1,)), ((), ()))
    out = lax.dot_general(ah, bh, dn, preferred_element_type=F32)
    out += lax.dot_general(ah, bl, dn, preferred_element_type=F32)
    out += lax.dot_general(al, bh, dn, preferred_element_type=F32)
    return out


def _dot_exact_lhs(a, b):
    a = a.astype(BF16)
    b1 = b.astype(BF16)
    r1 = b - b1.astype(F32)
    b2 = r1.astype(BF16)
    b3 = (r1 - b2.astype(F32)).astype(BF16)
    out = jnp.dot(a, b1, preferred_element_type=F32)
    out += jnp.dot(a, b2, preferred_element_type=F32)
    out += jnp.dot(a, b3, preferred_element_type=F32)
    return out


def _sigmoid(x):
    return 1.0 / (1.0 + jnp.exp(-x))


def _softplus(x):
    return jnp.maximum(x, 0.0) + jnp.log(1.0 + jnp.exp(-jnp.abs(x)))


def _seg_order(i, d, n_ctx, n_tot):
    rev = jnp.where(i < n_ctx, n_ctx - 1 - i, n_ctx + n_tot - 1 - i)
    return jnp.where(d == 0, i, rev)


def _mod_kernel(c_ref, w_ref, b_ref, o_ref):
    c = c_ref[...]
    s = c * _sigmoid(c)
    o_ref[...] = _dot_x3(s, w_ref[...]) + b_ref[...]


def _mod_vectors(c8, w, b, layer):
    _, d, n = w.shape
    tn = n // 4
    return pl.pallas_call(
        _mod_kernel,
        grid=(n // tn,),
        in_specs=[pl.BlockSpec((8, d), lambda j: (0, 0)),
                  pl.BlockSpec((None, d, tn), lambda j: (layer, 0, j)),
                  pl.BlockSpec((1, tn), lambda j: (0, j))],
        out_specs=pl.BlockSpec((8, tn), lambda j: (0, j)),
        out_shape=jax.ShapeDtypeStruct((8, n), F32),
        compiler_params=_params("parallel"),
        name="mod_vectors",
    )(c8, w, b.reshape(1, n))


def _inproj_kernel(x_ref, sh_ref, sc_ref, w_ref, oa_ref, ob_ref, *, n_a):
    h = x_ref[0] * (1.0 + sc_ref[0, 0]) + sh_ref[0, 0]
    p = jnp.dot(h.astype(BF16), w_ref[...], preferred_element_type=F32)
    oa_ref[0] = p[:, :n_a]
    ob_ref[0] = p[:, n_a:]


def _inproj(x, shift, scale, w_bf16, n_a, n_ctx_tiles):
    bz, t, d = x.shape
    n = w_bf16.shape[1]
    n_b = n - n_a
    seg = lambda i: (i >= n_ctx_tiles).astype(jnp.int32)
    return pl.pallas_call(
        functools.partial(_inproj_kernel, n_a=n_a),
        grid=(bz, t // ROW_TILE),
        in_specs=[pl.BlockSpec((1, ROW_TILE, d), lambda b, i: (b, i, 0)),
                  pl.BlockSpec((1, 1, 1, d), lambda b, i: (b, seg(i), 0, 0)),
                  pl.BlockSpec((1, 1, 1, d), lambda b, i: (b, seg(i), 0, 0)),
                  pl.BlockSpec((d, n), lambda b, i: (0, 0))],
        out_specs=[pl.BlockSpec((1, ROW_TILE, n_a), lambda b, i: (b, i, 0)),
                   pl.BlockSpec((1, ROW_TILE, n_b), lambda b, i: (b, i, 0))],
        out_shape=[jax.ShapeDtypeStruct((bz, t, n_a), F32),
                   jax.ShapeDtypeStruct((bz, t, n_b), F32)],
        compiler_params=_params("parallel", "parallel"),
        name="inproj",
    )(x, shift, scale, w_bf16)


def _rwkv_prep_kernel(p_ref, up_ref, dn_ref, mu_ref, w0_ref, wb_ref, a0_ref, ab_ref, gb_ref,
                      kkw_ref, kaw_ref, rkw_ref, ones_ref,
                      r_o, v_o, kk_o, g_o, bonus_o, lw_o, kd_o, a_o, *, n_ctx_tiles, n_tiles):
    i = pl.program_id(1)
    p = p_ref[0]
    tr, w = p.shape
    row = lax.broadcasted_iota(jnp.int32, (tr, w), 0)
    lane = lax.broadcasted_iota(jnp.int32, (tr, w), 1)
    prev = pltpu.roll(p, 1, 0)
    nxt = pltpu.roll(p, tr - 1, 0)
    sh_ctx = jnp.where(lane % 2 == 0, jnp.where(row == 0, 0.0, prev), jnp.where(row == tr - 1, 0.0, nxt))
    col = row % GRID_W
    left = jnp.where(col == 0, 0.0, prev)
    right = jnp.where(col == GRID_W - 1, 0.0, nxt)
    up_halo = jnp.where(i > n_ctx_tiles, up_ref[0], 0.0)
    dn_halo = jnp.where(i < n_tiles - 1, dn_ref[0], 0.0)
    up = jnp.concatenate([up_halo, p[:tr - GRID_W]], axis=0)
    down = jnp.concatenate([p[GRID_W:], dn_halo], axis=0)
    slot = lane % 4
    sh_lat = jnp.where(slot == 0, left, jnp.where(slot == 1, right, jnp.where(slot == 2, up, down)))
    sh = jnp.where(i < n_ctx_tiles, sh_ctx, sh_lat)
    z = p + (sh - p) * mu_ref[...]

    r = z[:, 0:A_W]
    k = z[:, A_W:2 * A_W]
    v = z[:, 2 * A_W:3 * A_W]
    o = 3 * A_W
    wd = z[:, o:o + 2 * A_DECAY_RANK]
    o += 2 * A_DECAY_RANK
    ad = z[:, o:o + 2 * A_ICL_RANK]
    o += 2 * A_ICL_RANK
    zg = z[:, o:o + A_GATE_RANK]

    ones_bd = ones_ref[...]

    def head_sum(x):
        hi, lo = _split(x)
        return (jnp.dot(hi, ones_bd, preferred_element_type=F32)
                + jnp.dot(lo, ones_bd, preferred_element_type=F32))

    g_o[0] = _dot(_sigmoid(zg), gb_ref[...]).astype(BF16)
    kk = k * kkw_ref[...]
    kk = kk / jnp.maximum(jnp.sqrt(head_sum(kk * kk)), 1e-12)
    r_o[0] = r.astype(BF16)
    v_o[0] = v.astype(BF16)
    kk_o[0] = kk.astype(BF16)
    rr = r * rkw_ref[...]
    bsum = jnp.zeros_like(r)
    for d in range(2):
        wdd = jnp.tanh(wd[:, d * A_DECAY_RANK:(d + 1) * A_DECAY_RANK])
        lw = -A_DECAY_SCALE * _sigmoid(w0_ref[d] + _dot(wdd, wb_ref[d]))
        a = _sigmoid(a0_ref[d] + _dot(ad[:, d * A_ICL_RANK:(d + 1) * A_ICL_RANK], ab_ref[d]))
        kd = k * (1.0 + (a - 1.0) * kaw_ref[...])
        lw_o[d, 0] = lw
        kd_o[d, 0] = kd.astype(BF16)
        a_o[d, 0] = a.astype(BF16)
        bsum = bsum + head_sum(rr * kd)
    bonus_o[0] = (bsum * v).astype(BF16)


def _rwkv_prep(pa, mu, w0, wb, a0, ab, gb, kkw, kaw, rkw, n_ctx_tiles):
    bz, t, _ = pa.shape
    n_tiles = t // ROW_TILE
    hb = ROW_TILE // GRID_W
    n_hblk = t // GRID_W
    ones_bd = jnp.kron(jnp.eye(A_HEADS, dtype=F32), jnp.ones((A_HEAD_DIM, A_HEAD_DIM), F32)).astype(BF16)
    row = lambda a: a.reshape(1, -1)
    tok = pl.BlockSpec((1, ROW_TILE, A_W), lambda b, i: (b, i, 0))
    tok2 = pl.BlockSpec((2, 1, ROW_TILE, A_W), lambda b, i: (0, b, i, 0))
    full = lambda a: pl.BlockSpec(a.shape, lambda b, i: (0,) * a.ndim)
    args = (row(mu), w0.reshape(2, 1, A_W), wb.astype(BF16), a0.reshape(2, 1, A_W), ab.astype(BF16),
            gb.astype(BF16), row(kkw), row(kaw), row(rkw), ones_bd)
    s1 = jax.ShapeDtypeStruct((bz, t, A_W), BF16)
    s2 = jax.ShapeDtypeStruct((2, bz, t, A_W), BF16)
    s2f = jax.ShapeDtypeStruct((2, bz, t, A_W), F32)
    return pl.pallas_call(
        functools.partial(_rwkv_prep_kernel, n_ctx_tiles=n_ctx_tiles, n_tiles=n_tiles),
        grid=(bz, n_tiles),
        in_specs=[pl.BlockSpec((1, ROW_TILE, A_IN), lambda b, i: (b, i, 0)),
                  pl.BlockSpec((1, GRID_W, A_IN), lambda b, i: (b, jnp.maximum(i * hb - 1, 0), 0)),
                  pl.BlockSpec((1, GRID_W, A_IN), lambda b, i: (b, jnp.minimum((i + 1) * hb, n_hblk - 1), 0))]
                 + [full(a) for a in args],
        out_specs=[tok, tok, tok, tok, tok, tok2, tok2, tok2],
        out_shape=[s1, s1, s1, s1, s1, s2f, s2, s2],
        compiler_params=_params("parallel", "parallel"),
        name="rwkv_prep",
    )(pa, pa, pa, *args)


_BMM_NN = (((2,), (1,)), ((0,), (0,)))
_BMM_NT = (((2,), (2,)), ((0,), (0,)))
_BMM_TN = (((1,), (1,)), ((0,), (0,)))
RWKV_CHAINS = 2 * A_HEADS


def _bmm(a, b, dims):
    return lax.dot_general(a.astype(BF16), b.astype(BF16), dims, preferred_element_type=F32)


def _bmm_x3(a, b, dims):
    ah, al = _split(a)
    bh, bl = _split(b)
    rows = a.shape[1]
    p = lax.dot_general(jnp.concatenate([ah, al], axis=1), bh, dims, preferred_element_type=F32)
    return p[:, :rows] + p[:, rows:] + lax.dot_general(ah, bl, dims, preferred_element_type=F32)


def _split_heads(x):
    return jnp.stack([x[:, h * A_HEAD_DIM:(h + 1) * A_HEAD_DIM] for h in range(A_HEADS)], axis=0)


def _rwkv_scan_kernel(rf, lwf, kdf, vf, kkf, af, rb, lwb, kdb, vb, kkb, ab, of, ob, s_ref):
    i = pl.program_id(1)

    @pl.when(i == 0)
    def _():
        s_ref[...] = jnp.zeros_like(s_ref)

    n = rf.shape[1]
    ti = lax.broadcasted_iota(jnp.int32, (n, n), 0)
    tj = lax.broadcasted_iota(jnp.int32, (n, n), 1)

    def decayed(r_ref, lw_ref, kd_ref, v_ref, kk_ref, a_ref, fwd):
        lw = lw_ref[0, 0]
        kd = kd_ref[0, 0].astype(F32)
        kk = kk_ref[0].astype(F32)
        b = kk * a_ref[0, 0].astype(F32)
        incl = (tj <= ti) if fwd else (tj >= ti)
        c = _dot_exact_lhs(incl.astype(F32), lw)
        ctot = jnp.sum(lw, axis=0, keepdims=True)
        e_nc = jnp.exp(-c)
        e_tot = jnp.exp(ctot - c)
        return dict(rt=r_ref[0].astype(F32) * jnp.exp(c), at=kk * jnp.exp(c - lw), kt=kd * e_nc, bt=b * e_nc,
                    kh=kd * e_tot, bh=b * e_tot, v=v_ref[0].astype(F32), e=jnp.exp(ctot))

    pf = decayed(rf, lwf, kdf, vf, kkf, af, True)
    pb = decayed(rb, lwb, kdb, vb, kkb, ab, False)

    def chains(name):
        return jnp.concatenate([_split_heads(pf[name]), _split_heads(pb[name])], axis=0)

    rt, at, kt, bt, kh, bh, v = (chains(x) for x in ("rt", "at", "kt", "bt", "kh", "bh", "v"))
    ar = jnp.concatenate([at, rt], axis=1)
    kb = jnp.concatenate([kt, bt], axis=1)
    m = _bmm(ar, kb, _BMM_NT)
    chain = lax.broadcasted_iota(jnp.int32, (RWKV_CHAINS, n, n), 0)
    lag = jnp.where(chain < A_HEADS, ti - tj, tj - ti)
    before = lag > 0
    before_eq = lag >= 0
    a_ak = jnp.where(before, m[:, :n, :n], 0.0)
    a_ab = jnp.where(before, m[:, :n, n:], 0.0)
    a_rk = jnp.where(before_eq, m[:, n:, :n], 0.0)
    a_rb = jnp.where(before_eq, m[:, n:, n:], 0.0)
    eye = (ti == tj).astype(F32)
    pw = -a_ab
    tinv = eye + pw
    for _ in range(int(math.log2(n)) - 1):
        pw = _bmm_x3(pw, pw, _BMM_NN)
        tinv = tinv + _bmm_x3(tinv, pw, _BMM_NN)
    sk = s_ref[...]
    a_s = _bmm(ar, sk, _BMM_NN)
    u = _bmm(tinv, a_s[:, :n] + _bmm(a_ak, v, _BMM_NN), _BMM_NN)
    vu = jnp.concatenate([v, -u], axis=1)
    out = a_s[:, n:] + _bmm(jnp.concatenate([a_rk, a_rb], axis=2), vu, _BMM_NN)
    e_rows = jnp.concatenate([_split_heads(pf["e"]), _split_heads(pb["e"])], axis=0)
    e_col = jnp.sum(eye * e_rows, axis=2, keepdims=True)
    s_ref[...] = sk * e_col + _bmm(jnp.concatenate([kh, bh], axis=1), vu, _BMM_TN)
    for h in range(A_HEADS):
        sl = slice(h * A_HEAD_DIM, (h + 1) * A_HEAD_DIM)
        of[0, :, sl] = out[h]
        ob[0, :, sl] = out[A_HEADS + h]


def _rwkv_scan(r, lw, kd, v, kk, a, ctx_len):
    bz, t, _ = r.shape
    n_ctx = ctx_len // RWKV_CHUNK
    n_chunks = t // RWKV_CHUNK
    rev = lambda i: _seg_order(i, 1, n_ctx, n_chunks)
    tok_f = pl.BlockSpec((1, RWKV_CHUNK, A_W), lambda b, i: (b, i, 0))
    tok_b = pl.BlockSpec((1, RWKV_CHUNK, A_W), lambda b, i: (b, rev(i), 0))
    dir_f = pl.BlockSpec((1, 1, RWKV_CHUNK, A_W), lambda b, i: (0, b, i, 0))
    dir_b = pl.BlockSpec((1, 1, RWKV_CHUNK, A_W), lambda b, i: (1, b, rev(i), 0))
    shp = jax.ShapeDtypeStruct((bz, t, A_W), F32)
    return pl.pallas_call(
        _rwkv_scan_kernel,
        grid=(bz, n_chunks),
        in_specs=[tok_f, dir_f, dir_f, tok_f, tok_f, dir_f, tok_b, dir_b, dir_b, tok_b, tok_b, dir_b],
        out_specs=[tok_f, tok_b],
        out_shape=[shp, shp],
        scratch_shapes=[pltpu.VMEM((RWKV_CHAINS, A_HEAD_DIM, A_HEAD_DIM), F32)],
        compiler_params=_params("parallel", "arbitrary"),
        name="rwkv_scan",
    )(r, lw, kd, v, kk, a, r, lw, kd, v, kk, a)


def _rglru_kernel(p_ref, pv_ref, nx_ref, cw_ref, cb_ref, gw_ref, gbias_ref, lam_ref, o_ref,
                  a_scr, b_scr, h_scr, carry_ref, *, n_ctx_tiles, n_tiles):
    d = pl.program_id(1)
    i = pl.program_id(2)
    blk = _seg_order(i, d, n_ctx_tiles, n_tiles)

    @pl.when(i == 0)
    def _():
        carry_ref[...] = jnp.zeros_like(carry_ref)

    p = p_ref[0]
    tr = p.shape[0]
    x = p[:, :B_W]
    gate_in = p[:, B_W:]
    has_prev = jnp.logical_and(blk != 0, blk != n_ctx_tiles)
    has_next = jnp.logical_and(blk != n_ctx_tiles - 1, blk != n_tiles - 1)
    pv = jnp.where(has_prev, pv_ref[0][:, :B_W], 0.0)
    nx = jnp.where(has_next, nx_ref[0][:, :B_W], 0.0)
    row = lax.broadcasted_iota(jnp.int32, (tr, B_W), 0)
    pv_m1 = pv[7:8]
    pv_m2 = pv[6:7]
    nx_p1 = nx[0:1]
    x_m1 = jnp.where(row == 0, pv_m1, pltpu.roll(x, 1, 0))
    x_m2 = jnp.where(row == 0, pv_m2, jnp.where(row == 1, pv_m1, pltpu.roll(x, 2, 0)))
    x_p1 = jnp.where(row == tr - 1, nx_p1, pltpu.roll(x, tr - 1, 0))
    cw = cw_ref[...]
    u = cw[0:1] * x_m2 + cw[1:2] * x_m1 + cw[2:3] * x + cw[3:4] * x_p1 + cb_ref[...]
    gate = 0.5 * gate_in * (1.0 + jnp.tanh(math.sqrt(2.0 / math.pi) * (gate_in + 0.044715 * gate_in ** 3)))
    pre = _dot(u, gw_ref[0]) + gbias_ref[0]
    rec = _sigmoid(pre[:, :B_W])
    inp = _sigmoid(pre[:, B_W:])
    log_a = -B_C * rec * _softplus(-lam_ref[0])
    a_scr[...] = jnp.exp(log_a)
    b_scr[...] = jnp.sqrt(1.0 - jnp.exp(2.0 * log_a)) * inp * u

    n_grp = tr // 8
    g8 = lax.broadcasted_iota(jnp.int32, (8, B_W), 0)

    def group_scan(a, b, fwd):
        for s in (1, 2, 4):
            if fwd:
                ident = g8 < s
                a_p = pltpu.roll(a, s, 0)
                b_p = pltpu.roll(b, s, 0)
            else:
                ident = g8 >= 8 - s
                a_p = pltpu.roll(a, 8 - s, 0)
                b_p = pltpu.roll(b, 8 - s, 0)
            a_p = jnp.where(ident, 1.0, a_p)
            b_p = jnp.where(ident, 0.0, b_p)
            b = a * b_p + b
            a = a * a_p
        return a, b

    def run(fwd):
        def body(j, carry):
            g = j if fwd else n_grp - 1 - j
            off = pl.multiple_of(g * 8, 8)
            a, b = group_scan(a_scr[pl.ds(off, 8), :], b_scr[pl.ds(off, 8), :], fwd)
            h = b + a * carry
            h_scr[pl.ds(off, 8), :] = h
            return h[7:8] if fwd else h[0:1]
        carry_ref[...] = lax.fori_loop(0, n_grp, body, carry_ref[...])

    @pl.when(d == 0)
    def _():
        run(True)

    @pl.when(d == 1)
    def _():
        run(False)

    o_ref[0, 0] = (h_scr[...] * gate).astype(o_ref.dtype)


def _rglru(pb, conv_w, conv_b, gate_w, gate_b, lam, n_ctx_tiles):
    bz, t, _ = pb.shape
    n_tiles = t // ROW_TILE
    r8 = ROW_TILE // 8
    n8 = t // 8
    eye = jnp.eye(B_BLOCKS, dtype=F32)
    gw = jnp.einsum('dsgio,gh->dgisho', gate_w, eye).reshape(2, B_W, 2 * B_W).astype(BF16)
    gbias = gate_b.reshape(2, 1, 2 * B_W)
    blk = lambda i, d: _seg_order(i, d, n_ctx_tiles, n_tiles)
    return pl.pallas_call(
        functools.partial(_rglru_kernel, n_ctx_tiles=n_ctx_tiles, n_tiles=n_tiles),
        grid=(bz, 2, n_tiles),
        in_specs=[pl.BlockSpec((1, ROW_TILE, B_IN), lambda b, d, i: (b, blk(i, d), 0)),
                  pl.BlockSpec((1, 8, B_IN), lambda b, d, i: (b, jnp.maximum(blk(i, d) * r8 - 1, 0), 0)),
                  pl.BlockSpec((1, 8, B_IN), lambda b, d, i: (b, jnp.minimum((blk(i, d) + 1) * r8, n8 - 1), 0)),
                  pl.BlockSpec((4, B_W), lambda b, d, i: (0, 0)),
                  pl.BlockSpec((1, B_W), lambda b, d, i: (0, 0)),
                  pl.BlockSpec((1, B_W, 2 * B_W), lambda b, d, i: (d, 0, 0)),
                  pl.BlockSpec((1, 1, 2 * B_W), lambda b, d, i: (d, 0, 0)),
                  pl.BlockSpec((1, 1, B_W), lambda b, d, i: (d, 0, 0))],
        out_specs=pl.BlockSpec((1, 1, ROW_TILE, B_W), lambda b, d, i: (d, b, blk(i, d), 0)),
        out_shape=jax.ShapeDtypeStruct((2, bz, t, B_W), BF16),
        scratch_shapes=[pltpu.VMEM((ROW_TILE, B_W), F32), pltpu.VMEM((ROW_TILE, B_W), F32),
                        pltpu.VMEM((ROW_TILE, B_W), F32), pltpu.VMEM((1, B_W), F32)],
        compiler_params=_params("parallel", "parallel", "arbitrary"),
        name="rglru",
    )(pb, pb, pb, conv_w, conv_b.reshape(1, B_W), gw, gbias, lam.reshape(2, 1, B_W))


def _retention_kernel(lg_ref, p_ref, cos_ref, sin_ref, o_ref, st_ref):
    d = pl.program_id(1)
    i = pl.program_id(2)

    @pl.when(i == 0)
    def _():
        st_ref[...] = jnp.zeros_like(st_ref)

    n = p_ref.shape[1]
    cos2 = cos_ref[...]
    sin2 = sin_ref[...]
    ti = lax.broadcasted_iota(jnp.int32, (n, n), 0)
    tj = lax.broadcasted_iota(jnp.int32, (n, n), 1)
    fwd = d == 0
    diff = jnp.where(fwd, ti - tj, tj - ti).astype(F32)
    t1 = lax.broadcasted_iota(jnp.int32, (n, 1), 0)
    pos = jnp.where(fwd, t1, n - 1 - t1).astype(F32)
    for h in range(C_HEADS):
        lg = lg_ref[d, h] * jnp.ones((1, 1), F32)
        sl = slice(h * C_HEAD_DIM, (h + 1) * C_HEAD_DIM)
        q = p_ref[0, :, sl]
        k = p_ref[0, :, C_W + h * C_HEAD_DIM:C_W + (h + 1) * C_HEAD_DIM]
        v = p_ref[0, :, 2 * C_W + h * C_HEAD_DIM:2 * C_W + (h + 1) * C_HEAD_DIM]
        q = q * cos2 + pltpu.roll(q, C_HEAD_DIM // 2, 1) * sin2
        k = (k * cos2 + pltpu.roll(k, C_HEAD_DIM // 2, 1) * sin2) * (C_HEAD_DIM ** -0.5)
        decay = jnp.where(diff >= 0, jnp.exp(lg * jnp.maximum(diff, 0.0)), 0.0)
        scores = _dot_nt(q, k) * decay
        st = st_ref[h]
        out = _dot(scores, v) + _dot(q, st) * jnp.exp(lg * (pos + 1.0))
        zeta = jnp.exp(lg * (n - 1.0 - pos))
        st_ref[h] = st * jnp.exp(lg * n) + _dot_tn(k * zeta, v)
        o_ref[0, 0, :, sl] = out.astype(o_ref.dtype)


def _retention(pc, log_gamma, cos2, sin2, ctx_len):
    bz, t, _ = pc.shape
    n_ctx = ctx_len // ODD_CHUNK
    n_chunks = t // ODD_CHUNK
    blk = lambda i, d: _seg_order(i, d, n_ctx, n_chunks)
    return pl.pallas_call(
        _retention_kernel,
        grid_spec=pltpu.PrefetchScalarGridSpec(
            num_scalar_prefetch=1,
            grid=(bz, 2, n_chunks),
            in_specs=[pl.BlockSpec((1, ODD_CHUNK, 3 * C_W), lambda b, d, i, lg: (b, blk(i, d), 0)),
                      pl.BlockSpec((ODD_CHUNK, C_HEAD_DIM), lambda b, d, i, lg: (blk(i, d), 0)),
                      pl.BlockSpec((ODD_CHUNK, C_HEAD_DIM), lambda b, d, i, lg: (blk(i, d), 0))],
            out_specs=pl.BlockSpec((1, 1, ODD_CHUNK, C_W), lambda b, d, i, lg: (d, b, blk(i, d), 0)),
            scratch_shapes=[pltpu.VMEM((C_HEADS, C_HEAD_DIM, C_HEAD_DIM), F32)]),
        out_shape=jax.ShapeDtypeStruct((2, bz, t, C_W), BF16),
        compiler_params=_params("parallel", "parallel", "arbitrary"),
        name="retention",
    )(log_gamma, pc, cos2, sin2)


def _mlstm_kernel(p_ref, bias_ref, o_ref, c_ref, n_ref, m_ref):
    d = pl.program_id(1)
    i = pl.program_id(2)

    @pl.when(i == 0)
    def _():
        c_ref[...] = jnp.zeros_like(c_ref)
        n_ref[...] = jnp.zeros_like(n_ref)
        m_ref[...] = jnp.full_like(m_ref, -jnp.inf)

    n = p_ref.shape[1]
    ti = lax.broadcasted_iota(jnp.int32, (n, n), 0)
    tj = lax.broadcasted_iota(jnp.int32, (n, n), 1)
    fwd = d == 0
    before_eq = jnp.where(fwd, ti - tj, tj - ti) >= 0
    gts = p_ref[0, :, 4 * D_W:] + bias_ref[...]
    gts = jnp.where(fwd, gts, pltpu.roll(gts, LANES - D_HEADS, 1))
    lf = jnp.minimum(gts, 0.0) - jnp.log(1.0 + jnp.exp(-jnp.abs(gts)))
    bcum = _dot_exact_lhs(before_eq.astype(F32), lf)
    btot = jnp.sum(lf, axis=0, keepdims=True)
    gts_t = gts.T
    bcum_t = bcum.T
    for h in range(D_HEADS):
        sl = slice(h * D_HEAD_DIM, (h + 1) * D_HEAD_DIM)
        q = p_ref[0, :, sl]
        k = p_ref[0, :, D_W + h * D_HEAD_DIM:D_W + (h + 1) * D_HEAD_DIM] * (D_HEAD_DIM ** -0.5)
        v = p_ref[0, :, 2 * D_W + h * D_HEAD_DIM:2 * D_W + (h + 1) * D_HEAD_DIM]
        ig_col = gts[:, h:h + 1]
        ig_row = gts_t[h:h + 1, :]
        b_col = bcum[:, 2 * D_HEADS + h:2 * D_HEADS + h + 1]
        b_row = bcum_t[2 * D_HEADS + h:2 * D_HEADS + h + 1, :]
        b_end = btot[:, 2 * D_HEADS + h:2 * D_HEADS + h + 1]
        m_prev = m_ref[h][0:1, 0:1]
        d_log = jnp.where(before_eq, b_col - b_row + ig_row, -jnp.inf)
        g_inter = b_col + m_prev
        m_t = jnp.maximum(g_inter, jnp.max(d_log, axis=1, keepdims=True))
        s_intra = _dot_nt(q, k) * jnp.exp(d_log - m_t)
        s_inter = jnp.exp(g_inter - m_t)
        num = _dot(s_intra, v) + _dot(q, c_ref[h]) * s_inter
        den = jnp.sum(s_intra, axis=1, keepdims=True) + jnp.sum(q * n_ref[h], axis=1, keepdims=True) * s_inter
        den = jnp.maximum(jnp.abs(den), jnp.exp(-m_t))
        o_ref[0, 0, :, sl] = (num / den).astype(o_ref.dtype)
        w_end = b_end - b_col + ig_col
        m_i = jnp.max(w_end, axis=0, keepdims=True)
        e_end = jnp.exp(w_end - m_i)
        ke = k * e_end
        m_new = jnp.maximum(b_end + m_prev, m_i)
        s_old = jnp.exp(b_end + m_prev - m_new)
        s_new = jnp.exp(m_i - m_new)
        c_ref[h] = c_ref[h] * s_old + _dot_tn(ke, v) * s_new
        n_ref[h] = n_ref[h] * s_old + jnp.sum(ke, axis=0, keepdims=True) * s_new
        m_ref[h] = jnp.broadcast_to(m_new, m_ref.shape[1:])


def _mlstm(pd, bias128, ctx_len):
    bz, t, w = pd.shape
    n_ctx = ctx_len // ODD_CHUNK
    n_chunks = t // ODD_CHUNK
    blk = lambda i, d: _seg_order(i, d, n_ctx, n_chunks)
    return pl.pallas_call(
        _mlstm_kernel,
        grid=(bz, 2, n_chunks),
        in_specs=[pl.BlockSpec((1, ODD_CHUNK, w), lambda b, d, i: (b, blk(i, d), 0)),
                  pl.BlockSpec((1, LANES), lambda b, d, i: (0, 0))],
        out_specs=pl.BlockSpec((1, 1, ODD_CHUNK, D_W), lambda b, d, i: (d, b, blk(i, d), 0)),
        out_shape=jax.ShapeDtypeStruct((2, bz, t, D_W), BF16),
        scratch_shapes=[pltpu.VMEM((D_HEADS, D_HEAD_DIM, D_HEAD_DIM), F32),
                        pltpu.VMEM((D_HEADS, 1, D_HEAD_DIM), F32),
                        pltpu.VMEM((D_HEADS, 8, LANES), F32)],
        compiler_params=_params("parallel", "parallel", "arbitrary"),
        name="mlstm",
    )(pd, bias128)


def _group_norm_lanes(x, n_groups, eps):
    gw = x.shape[1] // n_groups
    outs = []
    for g in range(n_groups):
        xg = x[:, g * gw:(g + 1) * gw]
        mu = jnp.mean(xg, axis=1, keepdims=True)
        xc = xg - mu
        var = jnp.mean(xc * xc, axis=1, keepdims=True)
        outs.append(xc * lax.rsqrt(var + eps))
    return jnp.concatenate(outs, axis=1)


def _post_mixer(ycat, x_ref, wout_ref, mods_ref, lng_ref, lnb_ref, rw_ref, rb_ref, x1_o, h2_o, topi_o, topw_o, rank_o, cnt_o, cnt_scr, alpha):
    y = jnp.dot(ycat.astype(BF16), wout_ref[...], preferred_element_type=F32)
    mods = mods_ref[0, 0]
    x1 = alpha * x_ref[0] + mods[2:3] * y
    mu = jnp.mean(x1, axis=1, keepdims=True)
    xc = x1 - mu
    var = jnp.mean(xc * xc, axis=1, keepdims=True)
    x1 = xc * lax.rsqrt(var + LN_EPS) * lng_ref[...] + lnb_ref[...]
    x1_o[0] = x1
    h2 = x1 * (1.0 + mods[4:5]) + mods[3:4]
    h2_o[0] = h2
    logits = _dot_x3(h2, rw_ref[...]) + rb_ref[...]
    lane = lax.broadcasted_iota(jnp.int32, logits.shape, 1)
    work = jnp.where(lane < N_EXPERTS, logits, -jnp.inf)
    top_i = jnp.zeros(logits.shape, jnp.int32)
    top_v = jnp.full(logits.shape, -jnp.inf, F32)
    top = None
    picks = []
    for kth in range(TOP_K):
        mk = jnp.max(work, axis=1, keepdims=True)
        idx = jnp.min(jnp.where(work == mk, lane, LANES), axis=1, keepdims=True)
        pick = lane == idx
        picks.append(pick)
        work = jnp.where(pick, -jnp.inf, work)
        top_i = jnp.where(lane == kth, idx, top_i)
        top_v = jnp.where(lane == kth, mk, top_v)
        if kth == 0:
            top = mk
    e = jnp.exp(top_v - top)
    topi_o[0] = top_i
    topw_o[0] = e / jnp.sum(e, axis=1, keepdims=True)

    @pl.when(jnp.logical_and(pl.program_id(0) == 0, pl.program_id(1) == 0))
    def _():
        cnt_scr[...] = jnp.zeros_like(cnt_scr)

    rows = logits.shape[0]
    chosen = jnp.zeros(logits.shape, F32)
    for pick in picks:
        chosen = jnp.where(pick, 1.0, chosen)
    ti = lax.broadcasted_iota(jnp.int32, (rows, rows), 0)
    tj = lax.broadcasted_iota(jnp.int32, (rows, rows), 1)
    earlier = jnp.dot((tj < ti).astype(BF16), chosen.astype(BF16), preferred_element_type=F32) + cnt_scr[...]
    rank = jnp.zeros(logits.shape, F32)
    for kth, pick in enumerate(picks):
        rank = jnp.where(lane == kth, jnp.sum(jnp.where(pick, earlier, 0.0), axis=1, keepdims=True), rank)
    rank_o[0] = rank.astype(jnp.int32)
    cnt_scr[...] = cnt_scr[...] + jnp.sum(chosen, axis=0, keepdims=True)
    cnt_o[...] = cnt_scr[...].astype(jnp.int32)


def _outproj_even_kernel(wkvf_ref, wkvb_ref, bonus_ref, g_ref, gng_ref, gnb_ref, hb_ref,
                         x_ref, wout_ref, mods_ref, lng_ref, lnb_ref, rw_ref, rb_ref,
                         x1_o, h2_o, topi_o, topw_o, rank_o, cnt_o, cnt_scr, *, alpha):
    wkv = wkvf_ref[0] + wkvb_ref[0]
    ya = ((_group_norm_lanes(wkv, A_HEADS, GN_EPS_RWKV) * gng_ref[...] + gnb_ref[...] + bonus_ref[0].astype(F32))
          * g_ref[0].astype(F32))
    yb = hb_ref[0, 0].astype(F32) + hb_ref[1, 0].astype(F32)
    ycat = jnp.concatenate([ya, yb], axis=1)
    _post_mixer(ycat, x_ref, wout_ref, mods_ref, lng_ref, lnb_ref, rw_ref, rb_ref, x1_o, h2_o, topi_o, topw_o, rank_o, cnt_o, cnt_scr, alpha)


def _outproj_odd_kernel(oc_ref, pc_ref, cg_ref, cb_ref, od_ref, pd_ref, dg_ref, db_ref,
                        x_ref, wout_ref, mods_ref, lng_ref, lnb_ref, rw_ref, rb_ref,
                        x1_o, h2_o, topi_o, topw_o, rank_o, cnt_o, cnt_scr, *, alpha):
    oc = oc_ref[0, 0].astype(F32) + oc_ref[1, 0].astype(F32)
    gc = pc_ref[0]
    ya = (_group_norm_lanes(oc, C_HEADS, LN_EPS) * cg_ref[...] + cb_ref[...]) * (gc * _sigmoid(gc))
    od = od_ref[0, 0].astype(F32) + od_ref[1, 0].astype(F32)
    yb = (_group_norm_lanes(od, D_HEADS, LN_EPS) * dg_ref[...] + db_ref[...]) * _sigmoid(pd_ref[0])
    ycat = jnp.concatenate([ya, yb], axis=1)
    _post_mixer(ycat, x_ref, wout_ref, mods_ref, lng_ref, lnb_ref, rw_ref, rb_ref, x1_o, h2_o, topi_o, topw_o, rank_o, cnt_o, cnt_scr, alpha)


def _outproj(kind, mixer_args, mixer_specs, x, wout, mods, ln_g, ln_b, rw, rb, n_ctx_tiles, alpha, first_tile):
    bz, t, d = x.shape
    t_out = t - first_tile * ROW_TILE
    seg = lambda i: (i + first_tile >= n_ctx_tiles).astype(jnp.int32)
    row = lambda a: a.reshape(1, -1)
    vec = lambda n: pl.BlockSpec((1, n), lambda b, i: (0, 0))
    kern = _outproj_even_kernel if kind == 'even' else _outproj_odd_kernel
    tok = lambda n: pl.BlockSpec((1, ROW_TILE, n), lambda b, i: (b, i, 0))
    return pl.pallas_call(
        functools.partial(kern, alpha=alpha),
        grid=(bz, t_out // ROW_TILE),
        in_specs=list(mixer_specs) + [
            pl.BlockSpec((1, ROW_TILE, d), lambda b, i: (b, i + first_tile, 0)),
            pl.BlockSpec((d, d), lambda b, i: (0, 0)),
            pl.BlockSpec((1, 1, 8, d), lambda b, i: (b, seg(i), 0, 0)),
            vec(d), vec(d),
            pl.BlockSpec((d, LANES), lambda b, i: (0, 0)),
            vec(LANES)],
        out_specs=[tok(d), tok(d), tok(LANES), tok(LANES), tok(LANES), vec(LANES)],
        out_shape=[jax.ShapeDtypeStruct((bz, t_out, d), F32),
                   jax.ShapeDtypeStruct((bz, t_out, d), F32),
                   jax.ShapeDtypeStruct((bz, t_out, LANES), jnp.int32),
                   jax.ShapeDtypeStruct((bz, t_out, LANES), F32),
                   jax.ShapeDtypeStruct((bz, t_out, LANES), jnp.int32),
                   jax.ShapeDtypeStruct((1, LANES), jnp.int32)],
        scratch_shapes=[pltpu.VMEM((1, LANES), F32)],
        compiler_params=_params("arbitrary", "arbitrary"),
        name="outproj_" + kind,
    )(*mixer_args, x, wout, mods, row(ln_g), row(ln_b), rw, rb)


def _route(top_i, rank, counts):
    experts = jnp.arange(N_EXPERTS, dtype=jnp.int32)
    e_flat = top_i[..., :TOP_K].reshape(-1)
    n_asg = e_flat.shape[0]
    cnt = counts[0, :N_EXPERTS]
    start = jnp.cumsum(cnt) - cnt
    dest = rank[..., :TOP_K].reshape(-1) + jnp.sum(jnp.where(e_flat[:, None] == experts, start, 0), axis=1)
    first_t = start // EXPERT_TILE
    last_t = (start + cnt - 1) // EXPERT_TILE
    n_t = jnp.where(cnt > 0, last_t - first_t + 1, 0)
    w_end = jnp.cumsum(n_t)
    n_work = w_end[-1]
    w = jnp.minimum(jnp.arange(n_asg // EXPERT_TILE + N_EXPERTS - 1, dtype=jnp.int32), n_work - 1)
    we = jnp.sum((w_end[None, :] <= w[:, None]).astype(jnp.int32), axis=1)
    pick = we[:, None] == experts
    sel = lambda v: jnp.sum(jnp.where(pick, v, 0), axis=1)
    wt = sel(first_t) + (w - sel(w_end - n_t))
    lo = sel(start)
    hi = lo + sel(cnt)
    slot = sel(jnp.cumsum((cnt > 0).astype(jnp.int32)) - 1) % 2
    w_next = sel(w_end)
    nxt = jnp.where(w_next < n_work, jnp.sum((w_end[None, :] <= w_next[:, None]).astype(jnp.int32), axis=1), -1)
    i32 = lambda v: v.astype(jnp.int32)
    return i32(dest), (i32(wt), i32(we), i32(lo), i32(hi), i32(n_work.reshape(1)), i32(slot), i32(nxt))


def _dispatch_kernel(dest_ref, h_ref, xs_hbm, sem):
    i = pl.program_id(0)
    rows = h_ref.shape[0]
    base = i * rows * TOP_K

    def issue(t, carry):
        for k in range(TOP_K):
            row = dest_ref[base + t * TOP_K + k]
            pltpu.make_async_copy(h_ref.at[pl.ds(t, 1)], xs_hbm.at[pl.ds(row, 1)], sem).start()
        return carry

    lax.fori_loop(0, rows, issue, 0)
    for _ in range(TOP_K):
        pltpu.make_async_copy(h_ref, xs_hbm.at[pl.ds(0, rows)], sem).wait()


def _dispatch(h2, dest):
    n, d = h2.shape
    return pl.pallas_call(
        _dispatch_kernel,
        grid_spec=pltpu.PrefetchScalarGridSpec(
            num_scalar_prefetch=1,
            grid=(n // ROW_TILE,),
            in_specs=[pl.BlockSpec((ROW_TILE, d), lambda i, dest: (i, 0))],
            out_specs=pl.BlockSpec(memory_space=pl.ANY),
            scratch_shapes=[pltpu.SemaphoreType.DMA(())]),
        out_shape=jax.ShapeDtypeStruct((n * TOP_K, d), F32),
        compiler_params=_params("arbitrary"),
        name="moe_dispatch",
    )(dest, h2)


def _experts_kernel(wt_ref, we_ref, lo_ref, hi_ref, nw_ref, slot_ref, nxt_ref, xs_ref, w1_hbm, b1_ref, w2_hbm,
                    b2_ref, y_ref, w1f, w2f, w1b, w2b, sems, *, f_chunk, layer):
    w = pl.program_id(0)
    valid = w < nw_ref[0]
    prev = jnp.maximum(w - 1, 0)
    new_expert = jnp.logical_or(w == 0, we_ref[w] != we_ref[prev])
    new_tile = jnp.logical_or(w == 0, wt_ref[w] != wt_ref[prev])

    def weight_copies(e, s):
        return (pltpu.make_async_copy(w1_hbm.at[layer, e], w1f.at[s], sems.at[0, s]),
                pltpu.make_async_copy(w2_hbm.at[layer, e], w2f.at[s], sems.at[1, s]))

    @pl.when(w == 0)
    def _():
        for cp in weight_copies(we_ref[0], slot_ref[0]):
            cp.start()

    @pl.when(jnp.logical_and(valid, new_expert))
    def _():
        s = slot_ref[w]
        for cp in weight_copies(we_ref[w], s):
            cp.wait()
        w1b[...] = w1f[s].astype(BF16)
        w2b[...] = w2f[s].astype(BF16)

        @pl.when(nxt_ref[w] >= 0)
        def _():
            for cp in weight_copies(nxt_ref[w], 1 - s):
                cp.start()

    @pl.when(valid)
    def _():
        x = xs_ref[...].astype(BF16)
        acc = jnp.zeros(y_ref.shape, F32)
        for c in range(D_FF // f_chunk):
            c0 = c * f_chunk
            glu = jnp.dot(x, w1b[:, c0:c0 + f_chunk], preferred_element_type=F32) + b1_ref[0, :, c0:c0 + f_chunk]
            lin = (jnp.dot(x, w1b[:, D_FF + c0:D_FF + c0 + f_chunk], preferred_element_type=F32)
                   + b1_ref[0, :, D_FF + c0:D_FF + c0 + f_chunk])
            glu = jnp.minimum(glu, SWIGLU_LIMIT)
            lin = jnp.clip(lin, -SWIGLU_LIMIT, SWIGLU_LIMIT)
            act = (lin + 1.0) * glu * _sigmoid(SWIGLU_ALPHA * glu)
            acc += jnp.dot(act.astype(BF16), w2b[c0:c0 + f_chunk, :], preferred_element_type=F32)
        y = acc + b2_ref[0]
        rows = wt_ref[w] * EXPERT_TILE + lax.broadcasted_iota(jnp.int32, (EXPERT_TILE, 1), 0)
        mine = jnp.logical_and(rows >= lo_ref[w], rows < hi_ref[w])

        @pl.when(new_tile)
        def _():
            y_ref[...] = jnp.where(mine, y, 0.0)

        @pl.when(jnp.logical_not(new_tile))
        def _():
            y_ref[...] = jnp.where(mine, y, y_ref[...])


def _experts(xs, work, w1, b1, w2, b2, layer):
    n_asg, d = xs.shape
    n_l, n_e = w1.shape[:2]
    n_work_max = work[0].shape[0]
    return pl.pallas_call(
        functools.partial(_experts_kernel, f_chunk=512, layer=layer),
        grid_spec=pltpu.PrefetchScalarGridSpec(
            num_scalar_prefetch=7,
            grid=(n_work_max,),
            in_specs=[pl.BlockSpec((EXPERT_TILE, d), lambda w, wt, we, *_: (wt[w], 0)),
                      pl.BlockSpec(memory_space=pl.ANY),
                      pl.BlockSpec((None, 1, 1, 2 * D_FF), lambda w, wt, we, *_: (layer, we[w], 0, 0)),
                      pl.BlockSpec(memory_space=pl.ANY),
                      pl.BlockSpec((None, 1, 1, d), lambda w, wt, we, *_: (layer, we[w], 0, 0))],
            out_specs=pl.BlockSpec((EXPERT_TILE, d), lambda w, wt, we, *_: (wt[w], 0)),
            scratch_shapes=[pltpu.VMEM((2, d, 2 * D_FF), F32), pltpu.VMEM((2, D_FF, d), F32),
                            pltpu.VMEM((d, 2 * D_FF), BF16), pltpu.VMEM((D_FF, d), BF16),
                            pltpu.SemaphoreType.DMA((2, 2))]),
        out_shape=jax.ShapeDtypeStruct((n_asg, d), F32),
        compiler_params=_params("arbitrary"),
        name="moe_experts",
    )(*work, xs, w1, b1.reshape(n_l, n_e, 1, -1), w2, b2.reshape(n_l, n_e, 1, -1))


def _combine_kernel(dest_ref, y_hbm, topw_ref, x_ref, mods_ref, lng_ref, lnb_ref, o_ref, ybuf, sem, *, alpha):
    b = pl.program_id(0)
    i = pl.program_id(1)
    rows = x_ref.shape[1]
    base = (b * pl.num_programs(1) + i) * rows * TOP_K

    def issue(t, carry):
        for k in range(TOP_K):
            row = dest_ref[base + t * TOP_K + k]
            pltpu.make_async_copy(y_hbm.at[pl.ds(row, 1)], ybuf.at[k, pl.ds(t, 1)], sem).start()
        return carry

    lax.fori_loop(0, rows, issue, 0)
    for k in range(TOP_K):
        pltpu.make_async_copy(y_hbm.at[pl.ds(0, rows)], ybuf.at[k], sem).wait()
    tw = topw_ref[0]
    ffn = ybuf[0] * tw[:, 0:1]
    for k in range(1, TOP_K):
        ffn = ffn + ybuf[k] * tw[:, k:k + 1]
    mods = mods_ref[0, 0]
    x2 = alpha * x_ref[0] + mods[5:6] * ffn
    mu = jnp.mean(x2, axis=1, keepdims=True)
    xc = x2 - mu
    var = jnp.mean(xc * xc, axis=1, keepdims=True)
    o_ref[0] = xc * lax.rsqrt(var + LN_EPS) * lng_ref[...] + lnb_ref[...]


def _combine(y_sorted, dest, topw, x1, mods, ln_g, ln_b, n_ctx_tiles, alpha, first_tile):
    bz, t, d = x1.shape
    seg = lambda i: (i + first_tile >= n_ctx_tiles).astype(jnp.int32)
    tok = lambda n: pl.BlockSpec((1, ROW_TILE, n), lambda b, i, dest: (b, i, 0))
    vec = pl.BlockSpec((1, d), lambda b, i, dest: (0, 0))
    return pl.pallas_call(
        functools.partial(_combine_kernel, alpha=alpha),
        grid_spec=pltpu.PrefetchScalarGridSpec(
            num_scalar_prefetch=1,
            grid=(bz, t // ROW_TILE),
            in_specs=[pl.BlockSpec(memory_space=pl.ANY), tok(LANES), tok(d),
                      pl.BlockSpec((1, 1, 8, d), lambda b, i, dest: (b, seg(i), 0, 0)), vec, vec],
            out_specs=tok(d),
            scratch_shapes=[pltpu.VMEM((TOP_K, ROW_TILE, d), F32), pltpu.SemaphoreType.DMA(())]),
        out_shape=jax.ShapeDtypeStruct((bz, t, d), F32),
        compiler_params=_params("arbitrary", "arbitrary"),
        name="moe_combine",
    )(dest, y_sorted, topw, x1, mods, ln_g.reshape(1, d), ln_b.reshape(1, d))


def _rope_tables(ctx_len, seq):
    idx = jnp.arange(seq)
    rowp = (idx // GRID_W).astype(F32)
    colp = (idx % GRID_W).astype(F32)
    n_freq = C_HEAD_DIM // 4
    inv = ROPE_BASE ** (-jnp.arange(n_freq, dtype=F32) / n_freq)
    ang = jnp.concatenate([rowp[:, None] * inv, colp[:, None] * inv], -1)
    cos, sin = jnp.cos(ang), jnp.sin(ang)
    cos2 = jnp.concatenate([cos, cos], -1)
    sin2 = jnp.concatenate([-sin, sin], -1)
    cos2 = jnp.concatenate([jnp.ones((ctx_len, C_HEAD_DIM), F32), cos2], 0)
    sin2 = jnp.concatenate([jnp.zeros((ctx_len, C_HEAD_DIM), F32), sin2], 0)
    return cos2, sin2


def kernel(x, c, ctx, c_ctx, w_mod, b_mod, ln_g, ln_b, even_w_in, even_w_out, a_mu, a_w0, a_wB, a_a0, a_aB, a_gB, a_kk, a_ka, a_rk, a_gn_g, a_gn_b, b_conv_w, b_conv_b, b_gate_w, b_gate_b, b_lam, odd_w_in, odd_w_out, c_log_gamma, c_gn_g, c_gn_b, d_ibias, d_fbias, d_gn_g, d_gn_b, router_w, router_b, exp_w1, exp_b1, exp_w2, exp_b2):
    bz, seq, d = x.shape
    ctx_len = ctx.shape[1]
    depth = w_mod.shape[0]
    assert ctx_len == ROW_TILE and seq % ROW_TILE == 0
    t = ctx_len + seq
    n_ctx_tiles = ctx_len // ROW_TILE
    alpha = (2 * depth) ** 0.25
    xs = jnp.concatenate([ctx, x], axis=1)
    c8 = jnp.concatenate([c, c_ctx[None], jnp.zeros((8 - bz - 1, d), F32)], 0)
    cos2, sin2 = _rope_tables(ctx_len, seq)
    row = lambda a: a.reshape(1, -1)
    vec = lambda n: pl.BlockSpec((1, n), lambda b, i: (0, 0))

    for layer in range(depth):
        first = n_ctx_tiles if layer == depth - 1 else 0
        tok = lambda n, first=first: pl.BlockSpec((1, ROW_TILE, n), lambda b, i: (b, i + first, 0))
        tok2 = lambda n, first=first: pl.BlockSpec((2, 1, ROW_TILE, n), lambda b, i: (0, b, i + first, 0))
        m = _mod_vectors(c8, w_mod, b_mod[layer], layer).reshape(8, N_MOD, d)
        m_lat = m[:bz]
        m_ctx = jnp.broadcast_to(m[bz][None], (bz, N_MOD, d))
        mods = jnp.stack([m_ctx, m_lat], axis=1)
        mods = jnp.concatenate([mods, jnp.zeros((bz, 2, 8 - N_MOD, d), F32)], axis=2)
        shift = mods[:, :, 0:1]
        scale = mods[:, :, 1:2]
        rw = jnp.pad(router_w[layer], ((0, 0), (0, LANES - N_EXPERTS)))
        rb = jnp.pad(router_b[layer], (0, LANES - N_EXPERTS)).reshape(1, LANES)
        li = layer // 2
        if layer % 2 == 0:
            pa, pb = _inproj(xs, shift, scale, even_w_in[li].astype(BF16), A_IN, n_ctx_tiles)
            r, v, kk, g, bonus, lw, kd, a = _rwkv_prep(pa, a_mu[li], a_w0[li], a_wB[li], a_a0[li], a_aB[li],
                                                       a_gB[li], a_kk[li], a_ka[li], a_rk[li].reshape(-1),
                                                       n_ctx_tiles)
            wkv_f, wkv_b = _rwkv_scan(r, lw, kd, v, kk, a, ctx_len)
            hb = _rglru(pb, b_conv_w[li], b_conv_b[li], b_gate_w[li], b_gate_b[li], b_lam[li], n_ctx_tiles)
            mixer_args = (wkv_f, wkv_b, bonus, g, row(a_gn_g[li]), row(a_gn_b[li]), hb)
            mixer_specs = (tok(A_W), tok(A_W), tok(A_W), tok(A_W), vec(A_W), vec(A_W), tok2(B_W))
            x1, h2, topi, topw, rank, counts = _outproj('even', mixer_args, mixer_specs, xs, even_w_out[li].astype(BF16), mods,
                                     ln_g[layer, 0], ln_b[layer, 0], rw, rb, n_ctx_tiles, alpha, first)
        else:
            w_in = jnp.pad(odd_w_in[li], ((0, 0), (0, D_IN_PAD - D_IN))).astype(BF16)
            pc, pd = _inproj(xs, shift, scale, w_in, C_IN, n_ctx_tiles)
            oc = _retention(pc, c_log_gamma[li], cos2, sin2, ctx_len)
            bias128 = jnp.concatenate([d_ibias[li].reshape(-1), d_fbias[li].reshape(-1),
                                       jnp.zeros((LANES - 4 * D_HEADS,), F32)]).reshape(1, LANES)
            od = _mlstm(pd, bias128, ctx_len)
            gate_c = pl.BlockSpec((1, ROW_TILE, C_W), lambda b, i, first=first: (b, i + first, 3))
            gate_d = pl.BlockSpec((1, ROW_TILE, D_W), lambda b, i, first=first: (b, i + first, 3))
            mixer_args = (oc, pc, row(c_gn_g[li]), row(c_gn_b[li]), od, pd, row(d_gn_g[li]), row(d_gn_b[li]))
            mixer_specs = (tok2(C_W), gate_c, vec(C_W), vec(C_W), tok2(D_W), gate_d, vec(D_W), vec(D_W))
            x1, h2, topi, topw, rank, counts = _outproj('odd', mixer_args, mixer_specs, xs, odd_w_out[li].astype(BF16), mods,
                                     ln_g[layer, 0], ln_b[layer, 0], rw, rb, n_ctx_tiles, alpha, first)
        dest, work = _route(topi, rank, counts)
        xs_sorted = _dispatch(h2.reshape(-1, d), dest)
        y_sorted = _experts(xs_sorted, work, exp_w1, exp_b1, exp_w2, exp_b2, layer)
        xs = _combine(y_sorted, dest, topw, x1, mods, ln_g[layer, 1], ln_b[layer, 1], n_ctx_tiles, alpha, first)
    return xs[:, xs.shape[1] - seq:]
```

```python
import functools
import math

import jax
import jax.numpy as jnp
from jax import lax
from jax.experimental import pallas as pl
from jax.experimental.pallas import tpu as pltpu

F32 = jnp.float32
BF16 = jnp.bfloat16

GRID_W = 64
A_HEADS = 8
A_HEAD_DIM = 64
A_W = 512
A_DECAY_RANK = 64
A_ICL_RANK = 64
A_GATE_RANK = 128
A_IN = 1920
A_DECAY_SCALE = math.exp(-0.5)
GN_EPS_RWKV = 64e-5
B_W = 512
B_BLOCKS = 8
B_C = 8.0
B_IN = 1024
C_HEADS = 4
C_HEAD_DIM = 128
C_W = 512
C_IN = 2048
ROPE_BASE = 10000.0
D_HEADS = 4
D_HEAD_DIM = 128
D_W = 512
D_IN = 2064
D_IN_PAD = 2176
N_EXPERTS = 32
TOP_K = 4
D_FF = 1024
SWIGLU_ALPHA = 1.702
SWIGLU_LIMIT = 7.0
LN_EPS = 1e-5
N_MOD = 6
LANES = 128

ROW_TILE = 256
RWKV_CHUNK = 64
ODD_CHUNK = 256
EXPERT_TILE = 256
VMEM_LIMIT_BYTES = 56 * 1024 * 1024


def _params(*sem):
    return pltpu.CompilerParams(dimension_semantics=sem, vmem_limit_bytes=VMEM_LIMIT_BYTES)


def _dot(a, b):
    return jnp.dot(a.astype(BF16), b.astype(BF16), preferred_element_type=F32)


def _dot_nt(a, b):
    return lax.dot_general(a.astype(BF16), b.astype(BF16), (((1,), (1,)), ((), ())),
                           preferred_element_type=F32)


def _dot_tn(a, b):
    return lax.dot_general(a.astype(BF16), b.astype(BF16), (((0,), (0,)), ((), ())),
                           preferred_element_type=F32)


def _split(x):
    hi = x.astype(BF16)
    lo = (x - hi.astype(F32)).astype(BF16)
    return hi, lo


def _dot_x3(a, b):
    ah, al = _split(a)
    bh, bl = _split(b)
    out = jnp.dot(ah, bh, preferred_element_type=F32)
    out += jnp.dot(ah, bl, preferred_element_type=F32)
    out += jnp.dot(al, bh, preferred_element_type=F32)
    return out


def _dot_nt_x3(a, b):
    ah, al = _split(a)
    bh, bl = _split(b)
    dn = (((1,), (1,)), ((), ()))
    out = lax.dot_general(ah, bh, dn, preferred_element_type=F32)
    out += lax.dot_general(ah, bl, dn, preferred_element_type=F32)
    out += lax.dot_general(al, bh, dn, preferred_element_type=F32)
    return out


def _dot_exact_lhs(a, b):
    a = a.astype(BF16)
    b1 = b.astype(BF16)
    r1 = b - b1.astype(F32)
    b2 = r1.astype(BF16)
    b3 = (r1 - b2.astype(F32)).astype(BF16)
    out = jnp.dot(a, b1, preferred_element_type=F32)
    out += jnp.dot(a, b2, preferred_element_type=F32)
    out += jnp.dot(a, b3, preferred_element_type=F32)
    return out


def _sigmoid(x):
    return 1.0 / (1.0 + jnp.exp(-x))


def _softplus(x):
    return jnp.maximum(x, 0.0) + jnp.log(1.0 + jnp.exp(-jnp.abs(x)))


def _seg_order(i, d, n_ctx, n_tot):
    rev = jnp.where(i < n_ctx, n_ctx - 1 - i, n_ctx + n_tot - 1 - i)
    return jnp.where(d == 0, i, rev)


def _mod_kernel(c_ref, w_ref, b_ref, o_ref):
    c = c_ref[...]
    s = c * _sigmoid(c)
    o_ref[...] = _dot_x3(s, w_ref[...]) + b_ref[...]


def _mod_vectors(c8, w, b, layer):
    _, d, n = w.shape
    tn = n // 4
    return pl.pallas_call(
        _mod_kernel,
        grid=(n // tn,),
        in_specs=[pl.BlockSpec((8, d), lambda j: (0, 0)),
                  pl.BlockSpec((None, d, tn), lambda j: (layer, 0, j)),
                  pl.BlockSpec((1, tn), lambda j: (0, j))],
        out_specs=pl.BlockSpec((8, tn), lambda j: (0, j)),
        out_shape=jax.ShapeDtypeStruct((8, n), F32),
        compiler_params=_params("parallel"),
        name="mod_vectors",
    )(c8, w, b.reshape(1, n))


def _inproj_kernel(x_ref, sh_ref, sc_ref, w_ref, oa_ref, ob_ref, *, n_a):
    h = x_ref[0] * (1.0 + sc_ref[0, 0]) + sh_ref[0, 0]
    p = jnp.dot(h.astype(BF16), w_ref[...], preferred_element_type=F32)
    oa_ref[0] = p[:, :n_a]
    ob_ref[0] = p[:, n_a:]


def _inproj(x, shift, scale, w_bf16, n_a, n_ctx_tiles):
    bz, t, d = x.shape
    n = w_bf16.shape[1]
    n_b = n - n_a
    seg = lambda i: (i >= n_ctx_tiles).astype(jnp.int32)
    return pl.pallas_call(
        functools.partial(_inproj_kernel, n_a=n_a),
        grid=(bz, t // ROW_TILE),
        in_specs=[pl.BlockSpec((1, ROW_TILE, d), lambda b, i: (b, i, 0)),
                  pl.BlockSpec((1, 1, 1, d), lambda b, i: (b, seg(i), 0, 0)),
                  pl.BlockSpec((1, 1, 1, d), lambda b, i: (b, seg(i), 0, 0)),
                  pl.BlockSpec((d, n), lambda b, i: (0, 0))],
        out_specs=[pl.BlockSpec((1, ROW_TILE, n_a), lambda b, i: (b, i, 0)),
                   pl.BlockSpec((1, ROW_TILE, n_b), lambda b, i: (b, i, 0))],
        out_shape=[jax.ShapeDtypeStruct((bz, t, n_a), F32),
                   jax.ShapeDtypeStruct((bz, t, n_b), F32)],
        compiler_params=_params("parallel", "parallel"),
        name="inproj",
    )(x, shift, scale, w_bf16)


def _rwkv_prep_kernel(p_ref, up_ref, dn_ref, mu_ref, w0_ref, wb_ref, a0_ref, ab_ref, gb_ref,
                      kkw_ref, kaw_ref, rkw_ref, ones_ref,
                      r_o, v_o, kk_o, g_o, bonus_o, lw_o, kd_o, a_o, *, n_ctx_tiles, n_tiles):
    i = pl.program_id(1)
    p = p_ref[0]
    tr, w = p.shape
    row = lax.broadcasted_iota(jnp.int32, (tr, w), 0)
    lane = lax.broadcasted_iota(jnp.int32, (tr, w), 1)
    prev = pltpu.roll(p, 1, 0)
    nxt = pltpu.roll(p, tr - 1, 0)
    sh_ctx = jnp.where(lane % 2 == 0, jnp.where(row == 0, 0.0, prev), jnp.where(row == tr - 1, 0.0, nxt))
    col = row % GRID_W
    left = jnp.where(col == 0, 0.0, prev)
    right = jnp.where(col == GRID_W - 1, 0.0, nxt)
    up_halo = jnp.where(i > n_ctx_tiles, up_ref[0], 0.0)
    dn_halo = jnp.where(i < n_tiles - 1, dn_ref[0], 0.0)
    up = jnp.concatenate([up_halo, p[:tr - GRID_W]], axis=0)
    down = jnp.concatenate([p[GRID_W:], dn_halo], axis=0)
    slot = lane % 4
    sh_lat = jnp.where(slot == 0, left, jnp.where(slot == 1, right, jnp.where(slot == 2, up, down)))
    sh = jnp.where(i < n_ctx_tiles, sh_ctx, sh_lat)
    z = p + (sh - p) * mu_ref[...]

    r = z[:, 0:A_W]
    k = z[:, A_W:2 * A_W]
    v = z[:, 2 * A_W:3 * A_W]
    o = 3 * A_W
    wd = z[:, o:o + 2 * A_DECAY_RANK]
    o += 2 * A_DECAY_RANK
    ad = z[:, o:o + 2 * A_ICL_RANK]
    o += 2 * A_ICL_RANK
    zg = z[:, o:o + A_GATE_RANK]

    ones_bd = ones_ref[...]

    def head_sum(x):
        hi, lo = _split(x)
        return (jnp.dot(hi, ones_bd, preferred_element_type=F32)
                + jnp.dot(lo, ones_bd, preferred_element_type=F32))

    g_o[0] = _dot(_sigmoid(zg), gb_ref[...]).astype(BF16)
    kk = k * kkw_ref[...]
    kk = kk / jnp.maximum(jnp.sqrt(head_sum(kk * kk)), 1e-12)
    r_o[0] = r.astype(BF16)
    v_o[0] = v.astype(BF16)
    kk_o[0] = kk.astype(BF16)
    rr = r * rkw_ref[...]
    bsum = jnp.zeros_like(r)
    for d in range(2):
        wdd = jnp.tanh(wd[:, d * A_DECAY_RANK:(d + 1) * A_DECAY_RANK])
        lw = -A_DECAY_SCALE * _sigmoid(w0_ref[d] + _dot(wdd, wb_ref[d]))
        a = _sigmoid(a0_ref[d] + _dot(ad[:, d * A_ICL_RANK:(d + 1) * A_ICL_RANK], ab_ref[d]))
        kd = k * (1.0 + (a - 1.0) * kaw_ref[...])
        lw_o[d, 0] = lw
        kd_o[d, 0] = kd.astype(BF16)
        a_o[d, 0] = a.astype(BF16)
        bsum = bsum + head_sum(rr * kd)
    bonus_o[0] = (bsum * v).astype(BF16)


def _rwkv_prep(pa, mu, w0, wb, a0, ab, gb, kkw, kaw, rkw, n_ctx_tiles):
    bz, t, _ = pa.shape
    n_tiles = t // ROW_TILE
    hb = ROW_TILE // GRID_W
    n_hblk = t // GRID_W
    ones_bd = jnp.kron(jnp.eye(A_HEADS, dtype=F32), jnp.ones((A_HEAD_DIM, A_HEAD_DIM), F32)).astype(BF16)
    row = lambda a: a.reshape(1, -1)
    tok = pl.BlockSpec((1, ROW_TILE, A_W), lambda b, i: (b, i, 0))
    tok2 = pl.BlockSpec((2, 1, ROW_TILE, A_W), lambda b, i: (0, b, i, 0))
    full = lambda a: pl.BlockSpec(a.shape, lambda b, i: (0,) * a.ndim)
    args = (row(mu), w0.reshape(2, 1, A_W), wb.astype(BF16), a0.reshape(2, 1, A_W), ab.astype(BF16),
            gb.astype(BF16), row(kkw), row(kaw), row(rkw), ones_bd)
    s1 = jax.ShapeDtypeStruct((bz, t, A_W), BF16)
    s2 = jax.ShapeDtypeStruct((2, bz, t, A_W), BF16)
    s2f = jax.ShapeDtypeStruct((2, bz, t, A_W), F32)
    return pl.pallas_call(
        functools.partial(_rwkv_prep_kernel, n_ctx_tiles=n_ctx_tiles, n_tiles=n_tiles),
        grid=(bz, n_tiles),
        in_specs=[pl.BlockSpec((1, ROW_TILE, A_IN), lambda b, i: (b, i, 0)),
                  pl.BlockSpec((1, GRID_W, A_IN), lambda b, i: (b, jnp.maximum(i * hb - 1, 0), 0)),
                  pl.BlockSpec((1, GRID_W, A_IN), lambda b, i: (b, jnp.minimum((i + 1) * hb, n_hblk - 1), 0))]
                 + [full(a) for a in args],
        out_specs=[tok, tok, tok, tok, tok, tok2, tok2, tok2],
        out_shape=[s1, s1, s1, s1, s1, s2f, s2, s2],
        compiler_params=_params("parallel", "parallel"),
        name="rwkv_prep",
    )(pa, pa, pa, *args)


_BMM_NN = (((2,), (1,)), ((0,), (0,)))
_BMM_NT = (((2,), (2,)), ((0,), (0,)))
_BMM_TN = (((1,), (1,)), ((0,), (0,)))
RWKV_CHAINS = 2 * A_HEADS


def _bmm(a, b, dims):
    return lax.dot_general(a.astype(BF16), b.astype(BF16), dims, preferred_element_type=F32)


def _bmm_x3(a, b, dims):
    ah, al = _split(a)
    bh, bl = _split(b)
    rows = a.shape[1]
    p = lax.dot_general(jnp.concatenate([ah, al], axis=1), bh, dims, preferred_element_type=F32)
    return p[:, :rows] + p[:, rows:] + lax.dot_general(ah, bl, dims, preferred_element_type=F32)


def _split_heads(x):
    return jnp.stack([x[:, h * A_HEAD_DIM:(h + 1) * A_HEAD_DIM] for h in range(A_HEADS)], axis=0)


def _rwkv_scan_kernel(rf, lwf, kdf, vf, kkf, af, rb, lwb, kdb, vb, kkb, ab, of, ob, s_ref):
    i = pl.program_id(1)

    @pl.when(i == 0)
    def _():
        s_ref[...] = jnp.zeros_like(s_ref)

    n = rf.shape[1]
    ti = lax.broadcasted_iota(jnp.int32, (n, n), 0)
    tj = lax.broadcasted_iota(jnp.int32, (n, n), 1)

    def decayed(r_ref, lw_ref, kd_ref, v_ref, kk_ref, a_ref, fwd):
        lw = lw_ref[0, 0]
        kd = kd_ref[0, 0].astype(F32)
        kk = kk_ref[0].astype(F32)
        b = kk * a_ref[0, 0].astype(F32)
        incl = (tj <= ti) if fwd else (tj >= ti)
        c = _dot_exact_lhs(incl.astype(F32), lw)
        ctot = jnp.sum(lw, axis=0, keepdims=True)
        e_nc = jnp.exp(-c)
        e_tot = jnp.exp(ctot - c)
        return dict(rt=r_ref[0].astype(F32) * jnp.exp(c), at=kk * jnp.exp(c - lw), kt=kd * e_nc, bt=b * e_nc,
                    kh=kd * e_tot, bh=b * e_tot, v=v_ref[0].astype(F32), e=jnp.exp(ctot))

    pf = decayed(rf, lwf, kdf, vf, kkf, af, True)
    pb = decayed(rb, lwb, kdb, vb, kkb, ab, False)

    def chains(name):
        return jnp.concatenate([_split_heads(pf[name]), _split_heads(pb[name])], axis=0)

    rt, at, kt, bt, kh, bh, v = (chains(x) for x in ("rt", "at", "kt", "bt", "kh", "bh", "v"))
    ar = jnp.concatenate([at, rt], axis=1)
    kb = jnp.concatenate([kt, bt], axis=1)
    m = _bmm(ar, kb, _BMM_NT)
    chain = lax.broadcasted_iota(jnp.int32, (RWKV_CHAINS, n, n), 0)
    lag = jnp.where(chain < A_HEADS, ti - tj, tj - ti)
    before = lag > 0
    before_eq = lag >= 0
    a_ak = jnp.where(before, m[:, :n, :n], 0.0)
    a_ab = jnp.where(before, m[:, :n, n:], 0.0)
    a_rk = jnp.where(before_eq, m[:, n:, :n], 0.0)
    a_rb = jnp.where(before_eq, m[:, n:, n:], 0.0)
    eye = (ti == tj).astype(F32)
    pw = -a_ab
    tinv = eye + pw
    for _ in range(int(math.log2(n)) - 1):
        pw = _bmm_x3(pw, pw, _BMM_NN)
        tinv = tinv + _bmm_x3(tinv, pw, _BMM_NN)
    sk = s_ref[...]
    a_s = _bmm(ar, sk, _BMM_NN)
    u = _bmm(tinv, a_s[:, :n] + _bmm(a_ak, v, _BMM_NN), _BMM_NN)
    vu = jnp.concatenate([v, -u], axis=1)
    out = a_s[:, n:] + _bmm(jnp.concatenate([a_rk, a_rb], axis=2), vu, _BMM_NN)
    e_rows = jnp.concatenate([_split_heads(pf["e"]), _split_heads(pb["e"])], axis=0)
    e_col = jnp.sum(eye * e_rows, axis=2, keepdims=True)
    s_ref[...] = sk * e_col + _bmm(jnp.concatenate([kh, bh], axis=1), vu, _BMM_TN)
    for h in range(A_HEADS):
        sl = slice(h * A_HEAD_DIM, (h + 1) * A_HEAD_DIM)
        of[0, :, sl] = out[h]
        ob[0, :, sl] = out[A_HEADS + h]


def _rwkv_scan(r, lw, kd, v, kk, a, ctx_len):
    bz, t, _ = r.shape
    n_ctx = ctx_len // RWKV_CHUNK
    n_chunks = t // RWKV_CHUNK
    rev = lambda i: _seg_order(i, 1, n_ctx, n_chunks)
    tok_f = pl.BlockSpec((1, RWKV_CHUNK, A_W), lambda b, i: (b, i, 0))
    tok_b = pl.BlockSpec((1, RWKV_CHUNK, A_W), lambda b, i: (b, rev(i), 0))
    dir_f = pl.BlockSpec((1, 1, RWKV_CHUNK, A_W), lambda b, i: (0, b, i, 0))
    dir_b = pl.BlockSpec((1, 1, RWKV_CHUNK, A_W), lambda b, i: (1, b, rev(i), 0))
    shp = jax.ShapeDtypeStruct((bz, t, A_W), F32)
    return pl.pallas_call(
        _rwkv_scan_kernel,
        grid=(bz, n_chunks),
        in_specs=[tok_f, dir_f, dir_f, tok_f, tok_f, dir_f, tok_b, dir_b, dir_b, tok_b, tok_b, dir_b],
        out_specs=[tok_f, tok_b],
        out_shape=[shp, shp],
        scratch_shapes=[pltpu.VMEM((RWKV_CHAINS, A_HEAD_DIM, A_HEAD_DIM), F32)],
        compiler_params=_params("parallel", "arbitrary"),
        name="rwkv_scan",
    )(r, lw, kd, v, kk, a, r, lw, kd, v, kk, a)


def _rglru_kernel(p_ref, pv_ref, nx_ref, cw_ref, cb_ref, gw_ref, gbias_ref, lam_ref, o_ref,
                  a_scr, b_scr, h_scr, carry_ref, *, n_ctx_tiles, n_tiles):
    d = pl.program_id(1)
    i = pl.program_id(2)
    blk = _seg_order(i, d, n_ctx_tiles, n_tiles)

    @pl.when(i == 0)
    def _():
        carry_ref[...] = jnp.zeros_like(carry_ref)

    p = p_ref[0]
    tr = p.shape[0]
    x = p[:, :B_W]
    gate_in = p[:, B_W:]
    has_prev = jnp.logical_and(blk != 0, blk != n_ctx_tiles)
    has_next = jnp.logical_and(blk != n_ctx_tiles - 1, blk != n_tiles - 1)
    pv = jnp.where(has_prev, pv_ref[0][:, :B_W], 0.0)
    nx = jnp.where(has_next, nx_ref[0][:, :B_W], 0.0)
    row = lax.broadcasted_iota(jnp.int32, (tr, B_W), 0)
    pv_m1 = pv[7:8]
    pv_m2 = pv[6:7]
    nx_p1 = nx[0:1]
    x_m1 = jnp.where(row == 0, pv_m1, pltpu.roll(x, 1, 0))
    x_m2 = jnp.where(row == 0, pv_m2, jnp.where(row == 1, pv_m1, pltpu.roll(x, 2, 0)))
    x_p1 = jnp.where(row == tr - 1, nx_p1, pltpu.roll(x, tr - 1, 0))
    cw = cw_ref[...]
    u = cw[0:1] * x_m2 + cw[1:2] * x_m1 + cw[2:3] * x + cw[3:4] * x_p1 + cb_ref[...]
    gate = 0.5 * gate_in * (1.0 + jnp.tanh(math.sqrt(2.0 / math.pi) * (gate_in + 0.044715 * gate_in ** 3)))
    pre = _dot(u, gw_ref[0]) + gbias_ref[0]
    rec = _sigmoid(pre[:, :B_W])
    inp = _sigmoid(pre[:, B_W:])
    log_a = -B_C * rec * _softplus(-lam_ref[0])
    a_scr[...] = jnp.exp(log_a)
    b_scr[...] = jnp.sqrt(1.0 - jnp.exp(2.0 * log_a)) * inp * u

    n_grp = tr // 8
    g8 = lax.broadcasted_iota(jnp.int32, (8, B_W), 0)

    def group_scan(a, b, fwd):
        for s in (1, 2, 4):
            if fwd:
                ident = g8 < s
                a_p = pltpu.roll(a, s, 0)
                b_p = pltpu.roll(b, s, 0)
            else:
                ident = g8 >= 8 - s
                a_p = pltpu.roll(a, 8 - s, 0)
                b_p = pltpu.roll(b, 8 - s, 0)
            a_p = jnp.where(ident, 1.0, a_p)
            b_p = jnp.where(ident, 0.0, b_p)
            b = a * b_p + b
            a = a * a_p
        return a, b

    def run(fwd):
        def body(j, carry):
            g = j if fwd else n_grp - 1 - j
            off = pl.multiple_of(g * 8, 8)
            a, b = group_scan(a_scr[pl.ds(off, 8), :], b_scr[pl.ds(off, 8), :], fwd)
            h = b + a * carry
            h_scr[pl.ds(off, 8), :] = h
            return h[7:8] if fwd else h[0:1]
        carry_ref[...] = lax.fori_loop(0, n_grp, body, carry_ref[...])

    @pl.when(d == 0)
    def _():
        run(True)

    @pl.when(d == 1)
    def _():
        run(False)

    o_ref[0, 0] = (h_scr[...] * gate).astype(o_ref.dtype)


def _rglru(pb, conv_w, conv_b, gate_w, gate_b, lam, n_ctx_tiles):
    bz, t, _ = pb.shape
    n_tiles = t // ROW_TILE
    r8 = ROW_TILE // 8
    n8 = t // 8
    eye = jnp.eye(B_BLOCKS, dtype=F32)
    gw = jnp.einsum('dsgio,gh->dgisho', gate_w, eye).reshape(2, B_W, 2 * B_W).astype(BF16)
    gbias = gate_b.reshape(2, 1, 2 * B_W)
    blk = lambda i, d: _seg_order(i, d, n_ctx_tiles, n_tiles)
    return pl.pallas_call(
        functools.partial(_rglru_kernel, n_ctx_tiles=n_ctx_tiles, n_tiles=n_tiles),
        grid=(bz, 2, n_tiles),
        in_specs=[pl.BlockSpec((1, ROW_TILE, B_IN), lambda b, d, i: (b, blk(i, d), 0)),
                  pl.BlockSpec((1, 8, B_IN), lambda b, d, i: (b, jnp.maximum(blk(i, d) * r8 - 1, 0), 0)),
                  pl.BlockSpec((1, 8, B_IN), lambda b, d, i: (b, jnp.minimum((blk(i, d) + 1) * r8, n8 - 1), 0)),
                  pl.BlockSpec((4, B_W), lambda b, d, i: (0, 0)),
                  pl.BlockSpec((1, B_W), lambda b, d, i: (0, 0)),
                  pl.BlockSpec((1, B_W, 2 * B_W), lambda b, d, i: (d, 0, 0)),
                  pl.BlockSpec((1, 1, 2 * B_W), lambda b, d, i: (d, 0, 0)),
                  pl.BlockSpec((1, 1, B_W), lambda b, d, i: (d, 0, 0))],
        out_specs=pl.BlockSpec((1, 1, ROW_TILE, B_W), lambda b, d, i: (d, b, blk(i, d), 0)),
        out_shape=jax.ShapeDtypeStruct((2, bz, t, B_W), BF16),
        scratch_shapes=[pltpu.VMEM((ROW_TILE, B_W), F32), pltpu.VMEM((ROW_TILE, B_W), F32),
                        pltpu.VMEM((ROW_TILE, B_W), F32), pltpu.VMEM((1, B_W), F32)],
        compiler_params=_params("parallel", "parallel", "arbitrary"),
        name="rglru",
    )(pb, pb, pb, conv_w, conv_b.reshape(1, B_W), gw, gbias, lam.reshape(2, 1, B_W))


def _retention_kernel(lg_ref, p_ref, cos_ref, sin_ref, o_ref, st_ref):
    d = pl.program_id(1)
    i = pl.program_id(2)

    @pl.when(i == 0)
    def _():
        st_ref[...] = jnp.zeros_like(st_ref)

    n = p_ref.shape[1]
    cos2 = cos_ref[...]
    sin2 = sin_ref[...]
    ti = lax.broadcasted_iota(jnp.int32, (n, n), 0)
    tj = lax.broadcasted_iota(jnp.int32, (n, n), 1)
    fwd = d == 0
    diff = jnp.where(fwd, ti - tj, tj - ti).astype(F32)
    t1 = lax.broadcasted_iota(jnp.int32, (n, 1), 0)
    pos = jnp.where(fwd, t1, n - 1 - t1).astype(F32)
    for h in range(C_HEADS):
        lg = lg_ref[d, h] * jnp.ones((1, 1), F32)
        sl = slice(h * C_HEAD_DIM, (h + 1) * C_HEAD_DIM)
        q = p_ref[0, :, sl]
        k = p_ref[0, :, C_W + h * C_HEAD_DIM:C_W + (h + 1) * C_HEAD_DIM]
        v = p_ref[0, :, 2 * C_W + h * C_HEAD_DIM:2 * C_W + (h + 1) * C_HEAD_DIM]
        q = q * cos2 + pltpu.roll(q, C_HEAD_DIM // 2, 1) * sin2
        k = (k * cos2 + pltpu.roll(k, C_HEAD_DIM // 2, 1) * sin2) * (C_HEAD_DIM ** -0.5)
        decay = jnp.where(diff >= 0, jnp.exp(lg * jnp.maximum(diff, 0.0)), 0.0)
        scores = _dot_nt(q, k) * decay
        st = st_ref[h]
        out = _dot(scores, v) + _dot(q, st) * jnp.exp(lg * (pos + 1.0))
        zeta = jnp.exp(lg * (n - 1.0 - pos))
        st_ref[h] = st * jnp.exp(lg * n) + _dot_tn(k * zeta, v)
        o_ref[0, 0, :, sl] = out.astype(o_ref.dtype)


def _retention(pc, log_gamma, cos2, sin2, ctx_len):
    bz, t, _ = pc.shape
    n_ctx = ctx_len // ODD_CHUNK
    n_chunks = t // ODD_CHUNK
    blk = lambda i, d: _seg_order(i, d, n_ctx, n_chunks)
    return pl.pallas_call(
        _retention_kernel,
        grid_spec=pltpu.PrefetchScalarGridSpec(
            num_scalar_prefetch=1,
            grid=(bz, 2, n_chunks),
            in_specs=[pl.BlockSpec((1, ODD_CHUNK, 3 * C_W), lambda b, d, i, lg: (b, blk(i, d), 0)),
                      pl.BlockSpec((ODD_CHUNK, C_HEAD_DIM), lambda b, d, i, lg: (blk(i, d), 0)),
                      pl.BlockSpec((ODD_CHUNK, C_HEAD_DIM), lambda b, d, i, lg: (blk(i, d), 0))],
            out_specs=pl.BlockSpec((1, 1, ODD_CHUNK, C_W), lambda b, d, i, lg: (d, b, blk(i, d), 0)),
            scratch_shapes=[pltpu.VMEM((C_HEADS, C_HEAD_DIM, C_HEAD_DIM), F32)]),
        out_shape=jax.ShapeDtypeStruct((2, bz, t, C_W), BF16),
        compiler_params=_params("parallel", "parallel", "arbitrary"),
        name="retention",
    )(log_gamma, pc, cos2, sin2)


def _mlstm_kernel(p_ref, bias_ref, o_ref, c_ref, n_ref, m_ref):
    d = pl.program_id(1)
    i = pl.program_id(2)

    @pl.when(i == 0)
    def _():
        c_ref[...] = jnp.zeros_like(c_ref)
        n_ref[...] = jnp.zeros_like(n_ref)
        m_ref[...] = jnp.full_like(m_ref, -jnp.inf)

    n = p_ref.shape[1]
    ti = lax.broadcasted_iota(jnp.int32, (n, n), 0)
    tj = lax.broadcasted_iota(jnp.int32, (n, n), 1)
    fwd = d == 0
    before_eq = jnp.where(fwd, ti - tj, tj - ti) >= 0
    gts = p_ref[0, :, 4 * D_W:] + bias_ref[...]
    gts = jnp.where(fwd, gts, pltpu.roll(gts, LANES - D_HEADS, 1))
    lf = jnp.minimum(gts, 0.0) - jnp.log(1.0 + jnp.exp(-jnp.abs(gts)))
    bcum = _dot_exact_lhs(before_eq.astype(F32), lf)
    btot = jnp.sum(lf, axis=0, keepdims=True)
    gts_t = gts.T
    bcum_t = bcum.T
    for h in range(D_HEADS):
        sl = slice(h * D_HEAD_DIM, (h + 1) * D_HEAD_DIM)
        q = p_ref[0, :, sl]
        k = p_ref[0, :, D_W + h * D_HEAD_DIM:D_W + (h + 1) * D_HEAD_DIM] * (D_HEAD_DIM ** -0.5)
        v = p_ref[0, :, 2 * D_W + h * D_HEAD_DIM:2 * D_W + (h + 1) * D_HEAD_DIM]
        ig_col = gts[:, h:h + 1]
        ig_row = gts_t[h:h + 1, :]
        b_col = bcum[:, 2 * D_HEADS + h:2 * D_HEADS + h + 1]
        b_row = bcum_t[2 * D_HEADS + h:2 * D_HEADS + h + 1, :]
        b_end = btot[:, 2 * D_HEADS + h:2 * D_HEADS + h + 1]
        m_prev = m_ref[h][0:1, 0:1]
        d_log = jnp.where(before_eq, b_col - b_row + ig_row, -jnp.inf)
        g_inter = b_col + m_prev
        m_t = jnp.maximum(g_inter, jnp.max(d_log, axis=1, keepdims=True))
        s_intra = _dot_nt(q, k) * jnp.exp(d_log - m_t)
        s_inter = jnp.exp(g_inter - m_t)
        num = _dot(s_intra, v) + _dot(q, c_ref[h]) * s_inter
        den = jnp.sum(s_intra, axis=1, keepdims=True) + jnp.sum(q * n_ref[h], axis=1, keepdims=True) * s_inter
        den = jnp.maximum(jnp.abs(den), jnp.exp(-m_t))
        o_ref[0, 0, :, sl] = (num / den).astype(o_ref.dtype)
        w_end = b_end - b_col + ig_col
        m_i = jnp.max(w_end, axis=0, keepdims=True)
        e_end = jnp.exp(w_end - m_i)
        ke = k * e_end
        m_new = jnp.maximum(b_end + m_prev, m_i)
        s_old = jnp.exp(b_end + m_prev - m_new)
        s_new = jnp.exp(m_i - m_new)
        c_ref[h] = c_ref[h] * s_old + _dot_tn(ke, v) * s_new
        n_ref[h] = n_ref[h] * s_old + jnp.sum(ke, axis=0, keepdims=True) * s_new
        m_ref[h] = jnp.broadcast_to(m_new, m_ref.shape[1:])


def _mlstm(pd, bias128, ctx_len):
    bz, t, w = pd.shape
    n_ctx = ctx_len // ODD_CHUNK
    n_chunks = t // ODD_CHUNK
    blk = lambda i, d: _seg_order(i, d, n_ctx, n_chunks)
    return pl.pallas_call(
        _mlstm_kernel,
        grid=(bz, 2, n_chunks),
        in_specs=[pl.BlockSpec((1, ODD_CHUNK, w), lambda b, d, i: (b, blk(i, d), 0)),
                  pl.BlockSpec((1, LANES), lambda b, d, i: (0, 0))],
        out_specs=pl.BlockSpec((1, 1, ODD_CHUNK, D_W), lambda b, d, i: (d, b, blk(i, d), 0)),
        out_shape=jax.ShapeDtypeStruct((2, bz, t, D_W), BF16),
        scratch_shapes=[pltpu.VMEM((D_HEADS, D_HEAD_DIM, D_HEAD_DIM), F32),
                        pltpu.VMEM((D_HEADS, 1, D_HEAD_DIM), F32),
                        pltpu.VMEM((D_HEADS, 8, LANES), F32)],
        compiler_params=_params("parallel", "parallel", "arbitrary"),
        name="mlstm",
    )(pd, bias128)


def _group_norm_lanes(x, n_groups, eps):
    gw = x.shape[1] // n_groups
    outs = []
    for g in range(n_groups):
        xg = x[:, g * gw:(g + 1) * gw]
        mu = jnp.mean(xg, axis=1, keepdims=True)
        xc = xg - mu
        var = jnp.mean(xc * xc, axis=1, keepdims=True)
        outs.append(xc * lax.rsqrt(var + eps))
    return jnp.concatenate(outs, axis=1)


def _post_mixer(ycat, x_ref, wout_ref, mods_ref, lng_ref, lnb_ref, rw_ref, rb_ref, x1_o, h2_o, topi_o, topw_o, rank_o, cnt_o, cnt_scr, alpha):
    y = jnp.dot(ycat.astype(BF16), wout_ref[...], preferred_element_type=F32)
    mods = mods_ref[0, 0]
    x1 = alpha * x_ref[0] + mods[2:3] * y
    mu = jnp.mean(x1, axis=1, keepdims=True)
    xc = x1 - mu
    var = jnp.mean(xc * xc, axis=1, keepdims=True)
    x1 = xc * lax.rsqrt(var + LN_EPS) * lng_ref[...] + lnb_ref[...]
    x1_o[0] = x1
    h2 = x1 * (1.0 + mods[4:5]) + mods[3:4]
    h2_o[0] = h2
    logits = _dot_x3(h2, rw_ref[...]) + rb_ref[...]
    lane = lax.broadcasted_iota(jnp.int32, logits.shape, 1)
    work = jnp.where(lane < N_EXPERTS, logits, -jnp.inf)
    top_i = jnp.zeros(logits.shape, jnp.int32)
    top_v = jnp.full(logits.shape, -jnp.inf, F32)
    top = None
    picks = []
    for kth in range(TOP_K):
        mk = jnp.max(work, axis=1, keepdims=True)
        idx = jnp.min(jnp.where(work == mk, lane, LANES), axis=1, keepdims=True)
        pick = lane == idx
        picks.append(pick)
        work = jnp.where(pick, -jnp.inf, work)
        top_i = jnp.where(lane == kth, idx, top_i)
        top_v = jnp.where(lane == kth, mk, top_v)
        if kth == 0:
            top = mk
    e = jnp.exp(top_v - top)
    topi_o[0] = top_i
    topw_o[0] = e / jnp.sum(e, axis=1, keepdims=True)

    @pl.when(jnp.logical_and(pl.program_id(0) == 0, pl.program_id(1) == 0))
    def _():
        cnt_scr[...] = jnp.zeros_like(cnt_scr)

    rows = logits.shape[0]
    chosen = jnp.zeros(logits.shape, F32)
    for pick in picks:
        chosen = jnp.where(pick, 1.0, chosen)
    ti = lax.broadcasted_iota(jnp.int32, (rows, rows), 0)
    tj = lax.broadcasted_iota(jnp.int32, (rows, rows), 1)
    earlier = jnp.dot((tj < ti).astype(BF16), chosen.astype(BF16), preferred_element_type=F32) + cnt_scr[...]
    rank = jnp.zeros(logits.shape, F32)
    for kth, pick in enumerate(picks):
        rank = jnp.where(lane == kth, jnp.sum(jnp.where(pick, earlier, 0.0), axis=1, keepdims=True), rank)
    rank_o[0] = rank.astype(jnp.int32)
    cnt_scr[...] = cnt_scr[...] + jnp.sum(chosen, axis=0, keepdims=True)
    cnt_o[...] = cnt_scr[...].astype(jnp.int32)


def _outproj_even_kernel(wkvf_ref, wkvb_ref, bonus_ref, g_ref, gng_ref, gnb_ref, hb_ref,
                         x_ref, wout_ref, mods_ref, lng_ref, lnb_ref, rw_ref, rb_ref,
                         x1_o, h2_o, topi_o, topw_o, rank_o, cnt_o, cnt_scr, *, alpha):
    wkv = wkvf_ref[0] + wkvb_ref[0]
    ya = ((_group_norm_lanes(wkv, A_HEADS, GN_EPS_RWKV) * gng_ref[...] + gnb_ref[...] + bonus_ref[0].astype(F32))
          * g_ref[0].astype(F32))
    yb = hb_ref[0, 0].astype(F32) + hb_ref[1, 0].astype(F32)
    ycat = jnp.concatenate([ya, yb], axis=1)
    _post_mixer(ycat, x_ref, wout_ref, mods_ref, lng_ref, lnb_ref, rw_ref, rb_ref, x1_o, h2_o, topi_o, topw_o, rank_o, cnt_o, cnt_scr, alpha)


def _outproj_odd_kernel(oc_ref, pc_ref, cg_ref, cb_ref, od_ref, pd_ref, dg_ref, db_ref,
                        x_ref, wout_ref, mods_ref, lng_ref, lnb_ref, rw_ref, rb_ref,
                        x1_o, h2_o, topi_o, topw_o, rank_o, cnt_o, cnt_scr, *, alpha):
    oc = oc_ref[0, 0].astype(F32) + oc_ref[1, 0].astype(F32)
    gc = pc_ref[0]
    ya = (_group_norm_lanes(oc, C_HEADS, LN_EPS) * cg_ref[...] + cb_ref[...]) * (gc * _sigmoid(gc))
    od = od_ref[0, 0].astype(F32) + od_ref[1, 0].astype(F32)
    yb = (_group_norm_lanes(od, D_HEADS, LN_EPS) * dg_ref[...] + db_ref[...]) * _sigmoid(pd_ref[0])
    ycat = jnp.concatenate([ya, yb], axis=1)
    _post_mixer(ycat, x_ref, wout_ref, mods_ref, lng_ref, lnb_ref, rw_ref, rb_ref, x1_o, h2_o, topi_o, topw_o, rank_o, cnt_o, cnt_scr, alpha)


def _outproj(kind, mixer_args, mixer_specs, x, wout, mods, ln_g, ln_b, rw, rb, n_ctx_tiles, alpha, first_tile):
    bz, t, d = x.shape
    t_out = t - first_tile * ROW_TILE
    seg = lambda i: (i + first_tile >= n_ctx_tiles).astype(jnp.int32)
    row = lambda a: a.reshape(1, -1)
    vec = lambda n: pl.BlockSpec((1, n), lambda b, i: (0, 0))
    kern = _outproj_even_kernel if kind == 'even' else _outproj_odd_kernel
    tok = lambda n: pl.BlockSpec((1, ROW_TILE, n), lambda b, i: (b, i, 0))
    return pl.pallas_call(
        functools.partial(kern, alpha=alpha),
        grid=(bz, t_out // ROW_TILE),
        in_specs=list(mixer_specs) + [
            pl.BlockSpec((1, ROW_TILE, d), lambda b, i: (b, i + first_tile, 0)),
            pl.BlockSpec((d, d), lambda b, i: (0, 0)),
            pl.BlockSpec((1, 1, 8, d), lambda b, i: (b, seg(i), 0, 0)),
            vec(d), vec(d),
            pl.BlockSpec((d, LANES), lambda b, i: (0, 0)),
            vec(LANES)],
        out_specs=[tok(d), tok(d), tok(LANES), tok(LANES), tok(LANES), vec(LANES)],
        out_shape=[jax.ShapeDtypeStruct((bz, t_out, d), F32),
                   jax.ShapeDtypeStruct((bz, t_out, d), F32),
                   jax.ShapeDtypeStruct((bz, t_out, LANES), jnp.int32),
                   jax.ShapeDtypeStruct((bz, t_out, LANES), F32),
                   jax.ShapeDtypeStruct((bz, t_out, LANES), jnp.int32),
                   jax.ShapeDtypeStruct((1, LANES), jnp.int32)],
        scratch_shapes=[pltpu.VMEM((1, LANES), F32)],
        compiler_params=_params("arbitrary", "arbitrary"),
        name="outproj_" + kind,
    )(*mixer_args, x, wout, mods, row(ln_g), row(ln_b), rw, rb)


def _route(top_i, rank, counts):
    experts = jnp.arange(N_EXPERTS, dtype=jnp.int32)
    e_flat = top_i[..., :TOP_K].reshape(-1)
    n_asg = e_flat.shape[0]
    cnt = counts[0, :N_EXPERTS]
    start = jnp.cumsum(cnt) - cnt
    dest = rank[..., :TOP_K].reshape(-1) + jnp.sum(jnp.where(e_flat[:, None] == experts, start, 0), axis=1)
    first_t = start // EXPERT_TILE
    last_t = (start + cnt - 1) // EXPERT_TILE
    n_t = jnp.where(cnt > 0, last_t - first_t + 1, 0)
    w_end = jnp.cumsum(n_t)
    n_work = w_end[-1]
    w = jnp.minimum(jnp.arange(n_asg // EXPERT_TILE + N_EXPERTS - 1, dtype=jnp.int32), n_work - 1)
    we = jnp.sum((w_end[None, :] <= w[:, None]).astype(jnp.int32), axis=1)
    pick = we[:, None] == experts
    sel = lambda v: jnp.sum(jnp.where(pick, v, 0), axis=1)
    wt = sel(first_t) + (w - sel(w_end - n_t))
    lo = sel(start)
    hi = lo + sel(cnt)
    slot = sel(jnp.cumsum((cnt > 0).astype(jnp.int32)) - 1) % 2
    w_next = sel(w_end)
    nxt = jnp.where(w_next < n_work, jnp.sum((w_end[None, :] <= w_next[:, None]).astype(jnp.int32), axis=1), -1)
    i32 = lambda v: v.astype(jnp.int32)
    return i32(dest), (i32(wt), i32(we), i32(lo), i32(hi), i32(n_work.reshape(1)), i32(slot), i32(nxt))


def _dispatch_kernel(dest_ref, h_ref, xs_hbm, stage, sems):
    i = pl.program_id(0)
    rows = h_ref.shape[0]
    base = i * rows * TOP_K

    def drain(s):
        for _ in range(TOP_K):
            pltpu.make_async_copy(stage.at[s], xs_hbm.at[pl.ds(0, rows)], sems.at[s]).wait()

    for s in range(2):
        @pl.when(i % 2 == s)
        def _(s=s):
            stage[s] = h_ref[...]

            def issue(t, carry):
                for k in range(TOP_K):
                    row = dest_ref[base + t * TOP_K + k]
                    pltpu.make_async_copy(stage.at[s, pl.ds(t, 1)], xs_hbm.at[pl.ds(row, 1)], sems.at[s]).start()
                return carry

            lax.fori_loop(0, rows, issue, 0)

            @pl.when(i > 0)
            def _():
                drain(1 - s)

            @pl.when(i == pl.num_programs(0) - 1)
            def _():
                drain(s)


def _dispatch(h2, dest):
    n, d = h2.shape
    return pl.pallas_call(
        _dispatch_kernel,
        grid_spec=pltpu.PrefetchScalarGridSpec(
            num_scalar_prefetch=1,
            grid=(n // ROW_TILE,),
            in_specs=[pl.BlockSpec((ROW_TILE, d), lambda i, dest: (i, 0))],
            out_specs=pl.BlockSpec(memory_space=pl.ANY),
            scratch_shapes=[pltpu.VMEM((2, ROW_TILE, d), F32), pltpu.SemaphoreType.DMA((2,))]),
        out_shape=jax.ShapeDtypeStruct((n * TOP_K, d), F32),
        compiler_params=_params("arbitrary"),
        name="moe_dispatch",
    )(dest, h2)


def _experts_kernel(wt_ref, we_ref, lo_ref, hi_ref, nw_ref, slot_ref, nxt_ref, xs_ref, w1_hbm, b1_ref, w2_hbm,
                    b2_ref, y_ref, w1f, w2f, w1b, w2b, sems, *, f_chunk, layer):
    w = pl.program_id(0)
    valid = w < nw_ref[0]
    prev = jnp.maximum(w - 1, 0)
    new_expert = jnp.logical_or(w == 0, we_ref[w] != we_ref[prev])
    new_tile = jnp.logical_or(w == 0, wt_ref[w] != wt_ref[prev])

    def weight_copies(e, s):
        return (pltpu.make_async_copy(w1_hbm.at[layer, e], w1f.at[s], sems.at[0, s]),
                pltpu.make_async_copy(w2_hbm.at[layer, e], w2f.at[s], sems.at[1, s]))

    @pl.when(w == 0)
    def _():
        for cp in weight_copies(we_ref[0], slot_ref[0]):
            cp.start()

    @pl.when(jnp.logical_and(valid, new_expert))
    def _():
        s = slot_ref[w]
        for cp in weight_copies(we_ref[w], s):
            cp.wait()
        w1b[...] = w1f[s].astype(BF16)
        w2b[...] = w2f[s].astype(BF16)

        @pl.when(nxt_ref[w] >= 0)
        def _():
            for cp in weight_copies(nxt_ref[w], 1 - s):
                cp.start()

    @pl.when(valid)
    def _():
        x = xs_ref[...].astype(BF16)
        acc = jnp.zeros(y_ref.shape, F32)
        for c in range(D_FF // f_chunk):
            c0 = c * f_chunk
            glu = jnp.dot(x, w1b[:, c0:c0 + f_chunk], preferred_element_type=F32) + b1_ref[0, :, c0:c0 + f_chunk]
            lin = (jnp.dot(x, w1b[:, D_FF + c0:D_FF + c0 + f_chunk], preferred_element_type=F32)
                   + b1_ref[0, :, D_FF + c0:D_FF + c0 + f_chunk])
            glu = jnp.minimum(glu, SWIGLU_LIMIT)
            lin = jnp.clip(lin, -SWIGLU_LIMIT, SWIGLU_LIMIT)
            act = (lin + 1.0) * glu * _sigmoid(SWIGLU_ALPHA * glu)
            acc += jnp.dot(act.astype(BF16), w2b[c0:c0 + f_chunk, :], preferred_element_type=F32)
        y = acc + b2_ref[0]
        rows = wt_ref[w] * EXPERT_TILE + lax.broadcasted_iota(jnp.int32, (EXPERT_TILE, 1), 0)
        mine = jnp.logical_and(rows >= lo_ref[w], rows < hi_ref[w])

        @pl.when(new_tile)
        def _():
            y_ref[...] = jnp.where(mine, y, 0.0)

        @pl.when(jnp.logical_not(new_tile))
        def _():
            y_ref[...] = jnp.where(mine, y, y_ref[...])


def _experts(xs, work, w1, b1, w2, b2, layer):
    n_asg, d = xs.shape
    n_l, n_e = w1.shape[:2]
    n_work_max = work[0].shape[0]
    return pl.pallas_call(
        functools.partial(_experts_kernel, f_chunk=512, layer=layer),
        grid_spec=pltpu.PrefetchScalarGridSpec(
            num_scalar_prefetch=7,
            grid=(n_work_max,),
            in_specs=[pl.BlockSpec((EXPERT_TILE, d), lambda w, wt, we, *_: (wt[w], 0)),
                      pl.BlockSpec(memory_space=pl.ANY),
                      pl.BlockSpec((None, 1, 1, 2 * D_FF), lambda w, wt, we, *_: (layer, we[w], 0, 0)),
                      pl.BlockSpec(memory_space=pl.ANY),
                      pl.BlockSpec((None, 1, 1, d), lambda w, wt, we, *_: (layer, we[w], 0, 0))],
            out_specs=pl.BlockSpec((EXPERT_TILE, d), lambda w, wt, we, *_: (wt[w], 0)),
            scratch_shapes=[pltpu.VMEM((2, d, 2 * D_FF), F32), pltpu.VMEM((2, D_FF, d), F32),
                            pltpu.VMEM((d, 2 * D_FF), BF16), pltpu.VMEM((D_FF, d), BF16),
                            pltpu.SemaphoreType.DMA((2, 2))]),
        out_shape=jax.ShapeDtypeStruct((n_asg, d), F32),
        compiler_params=_params("arbitrary"),
        name="moe_experts",
    )(*work, xs, w1, b1.reshape(n_l, n_e, 1, -1), w2, b2.reshape(n_l, n_e, 1, -1))


def _combine_kernel(dest_ref, y_hbm, topw_ref, x_ref, mods_ref, lng_ref, lnb_ref, o_ref, ybuf, sems, *, alpha):
    rows = x_ref.shape[1]
    tile = pl.program_id(0) * pl.num_programs(1) + pl.program_id(1)
    n_tiles = pl.num_programs(0) * pl.num_programs(1)

    def gather(which, s):
        base = which * rows * TOP_K

        def issue(t, carry):
            for k in range(TOP_K):
                row = dest_ref[base + t * TOP_K + k]
                pltpu.make_async_copy(y_hbm.at[pl.ds(row, 1)], ybuf.at[s, k, pl.ds(t, 1)], sems.at[s]).start()
            return carry

        lax.fori_loop(0, rows, issue, 0)

    @pl.when(tile == 0)
    def _():
        gather(0, 0)

    for s in range(2):
        @pl.when(tile % 2 == s)
        def _(s=s):
            @pl.when(tile + 1 < n_tiles)
            def _():
                gather(tile + 1, 1 - s)

            for k in range(TOP_K):
                pltpu.make_async_copy(y_hbm.at[pl.ds(0, rows)], ybuf.at[s, k], sems.at[s]).wait()
            tw = topw_ref[0]
            ffn = ybuf[s, 0] * tw[:, 0:1]
            for k in range(1, TOP_K):
                ffn = ffn + ybuf[s, k] * tw[:, k:k + 1]
            mods = mods_ref[0, 0]
            x2 = alpha * x_ref[0] + mods[5:6] * ffn
            mu = jnp.mean(x2, axis=1, keepdims=True)
            xc = x2 - mu
            var = jnp.mean(xc * xc, axis=1, keepdims=True)
            o_ref[0] = xc * lax.rsqrt(var + LN_EPS) * lng_ref[...] + lnb_ref[...]


def _combine(y_sorted, dest, topw, x1, mods, ln_g, ln_b, n_ctx_tiles, alpha, first_tile):
    bz, t, d = x1.shape
    seg = lambda i: (i + first_tile >= n_ctx_tiles).astype(jnp.int32)
    tok = lambda n: pl.BlockSpec((1, ROW_TILE, n), lambda b, i, dest: (b, i, 0))
    vec = pl.BlockSpec((1, d), lambda b, i, dest: (0, 0))
    return pl.pallas_call(
        functools.partial(_combine_kernel, alpha=alpha),
        grid_spec=pltpu.PrefetchScalarGridSpec(
            num_scalar_prefetch=1,
            grid=(bz, t // ROW_TILE),
            in_specs=[pl.BlockSpec(memory_space=pl.ANY), tok(LANES), tok(d),
                      pl.BlockSpec((1, 1, 8, d), lambda b, i, dest: (b, seg(i), 0, 0)), vec, vec],
            out_specs=tok(d),
            scratch_shapes=[pltpu.VMEM((2, TOP_K, ROW_TILE, d), F32), pltpu.SemaphoreType.DMA((2,))]),
        out_shape=jax.ShapeDtypeStruct((bz, t, d), F32),
        compiler_params=_params("arbitrary", "arbitrary"),
        name="moe_combine",
    )(dest, y_sorted, topw, x1, mods, ln_g.reshape(1, d), ln_b.reshape(1, d))


def _rope_tables(ctx_len, seq):
    idx = jnp.arange(seq)
    rowp = (idx // GRID_W).astype(F32)
    colp = (idx % GRID_W).astype(F32)
    n_freq = C_HEAD_DIM // 4
    inv = ROPE_BASE ** (-jnp.arange(n_freq, dtype=F32) / n_freq)
    ang = jnp.concatenate([rowp[:, None] * inv, colp[:, None] * inv], -1)
    cos, sin = jnp.cos(ang), jnp.sin(ang)
    cos2 = jnp.concatenate([cos, cos], -1)
    sin2 = jnp.concatenate([-sin, sin], -1)
    cos2 = jnp.concatenate([jnp.ones((ctx_len, C_HEAD_DIM), F32), cos2], 0)
    sin2 = jnp.concatenate([jnp.zeros((ctx_len, C_HEAD_DIM), F32), sin2], 0)
    return cos2, sin2


def kernel(x, c, ctx, c_ctx, w_mod, b_mod, ln_g, ln_b, even_w_in, even_w_out, a_mu, a_w0, a_wB, a_a0, a_aB, a_gB, a_kk, a_ka, a_rk, a_gn_g, a_gn_b, b_conv_w, b_conv_b, b_gate_w, b_gate_b, b_lam, odd_w_in, odd_w_out, c_log_gamma, c_gn_g, c_gn_b, d_ibias, d_fbias, d_gn_g, d_gn_b, router_w, router_b, exp_w1, exp_b1, exp_w2, exp_b2):
    bz, seq, d = x.shape
    ctx_len = ctx.shape[1]
    depth = w_mod.shape[0]
    assert ctx_len == ROW_TILE and seq % ROW_TILE == 0
    t = ctx_len + seq
    n_ctx_tiles = ctx_len // ROW_TILE
    alpha = (2 * depth) ** 0.25
    xs = jnp.concatenate([ctx, x], axis=1)
    c8 = jnp.concatenate([c, c_ctx[None], jnp.zeros((8 - bz - 1, d), F32)], 0)
    cos2, sin2 = _rope_tables(ctx_len, seq)
    row = lambda a: a.reshape(1, -1)
    vec = lambda n: pl.BlockSpec((1, n), lambda b, i: (0, 0))

    for layer in range(depth):
        first = n_ctx_tiles if layer == depth - 1 else 0
        tok = lambda n, first=first: pl.BlockSpec((1, ROW_TILE, n), lambda b, i: (b, i + first, 0))
        tok2 = lambda n, first=first: pl.BlockSpec((2, 1, ROW_TILE, n), lambda b, i: (0, b, i + first, 0))
        m = _mod_vectors(c8, w_mod, b_mod[layer], layer).reshape(8, N_MOD, d)
        m_lat = m[:bz]
        m_ctx = jnp.broadcast_to(m[bz][None], (bz, N_MOD, d))
        mods = jnp.stack([m_ctx, m_lat], axis=1)
        mods = jnp.concatenate([mods, jnp.zeros((bz, 2, 8 - N_MOD, d), F32)], axis=2)
        shift = mods[:, :, 0:1]
        scale = mods[:, :, 1:2]
        rw = jnp.pad(router_w[layer], ((0, 0), (0, LANES - N_EXPERTS)))
        rb = jnp.pad(router_b[layer], (0, LANES - N_EXPERTS)).reshape(1, LANES)
        li = layer // 2
        if layer % 2 == 0:
            pa, pb = _inproj(xs, shift, scale, even_w_in[li].astype(BF16), A_IN, n_ctx_tiles)
            r, v, kk, g, bonus, lw, kd, a = _rwkv_prep(pa, a_mu[li], a_w0[li], a_wB[li], a_a0[li], a_aB[li],
                                                       a_gB[li], a_kk[li], a_ka[li], a_rk[li].reshape(-1),
                                                       n_ctx_tiles)
            wkv_f, wkv_b = _rwkv_scan(r, lw, kd, v, kk, a, ctx_len)
            hb = _rglru(pb, b_conv_w[li], b_conv_b[li], b_gate_w[li], b_gate_b[li], b_lam[li], n_ctx_tiles)
            mixer_args = (wkv_f, wkv_b, bonus, g, row(a_gn_g[li]), row(a_gn_b[li]), hb)
            mixer_specs = (tok(A_W), tok(A_W), tok(A_W), tok(A_W), vec(A_W), vec(A_W), tok2(B_W))
            x1, h2, topi, topw, rank, counts = _outproj('even', mixer_args, mixer_specs, xs, even_w_out[li].astype(BF16), mods,
                                     ln_g[layer, 0], ln_b[layer, 0], rw, rb, n_ctx_tiles, alpha, first)
        else:
            w_in = jnp.pad(odd_w_in[li], ((0, 0), (0, D_IN_PAD - D_IN))).astype(BF16)
            pc, pd = _inproj(xs, shift, scale, w_in, C_IN, n_ctx_tiles)
            oc = _retention(pc, c_log_gamma[li], cos2, sin2, ctx_len)
            bias128 = jnp.concatenate([d_ibias[li].reshape(-1), d_fbias[li].reshape(-1),
                                       jnp.zeros((LANES - 4 * D_HEADS,), F32)]).reshape(1, LANES)
            od = _mlstm(pd, bias128, ctx_len)
            gate_c = pl.BlockSpec((1, ROW_TILE, C_W), lambda b, i, first=first: (b, i + first, 3))
            gate_d = pl.BlockSpec((1, ROW_TILE, D_W), lambda b, i, first=first: (b, i + first, 3))
            mixer_args = (oc, pc, row(c_gn_g[li]), row(c_gn_b[li]), od, pd, row(d_gn_g[li]), row(d_gn_b[li]))
            mixer_specs = (tok2(C_W), gate_c, vec(C_W), vec(C_W), tok2(D_W), gate_d, vec(D_W), vec(D_W))
            x1, h2, topi, topw, rank, counts = _outproj('odd', mixer_args, mixer_specs, xs, odd_w_out[li].astype(BF16), mods,
                                     ln_g[layer, 0], ln_b[layer, 0], rw, rb, n_ctx_tiles, alpha, first)
        dest, work = _route(topi, rank, counts)
        xs_sorted = _dispatch(h2.reshape(-1, d), dest)
        y_sorted = _experts(xs_sorted, work, exp_w1, exp_b1, exp_w2, exp_b2, layer)
        xs = _combine(y_sorted, dest, topw, x1, mods, ln_g[layer, 1], ln_b[layer, 1], n_ctx_tiles, alpha, first)
    return xs[:, xs.shape[1] - seq:]
```

```python
import functools
import math

import jax
import jax.numpy as jnp
from jax import lax
from jax.experimental import pallas as pl
from jax.experimental.pallas import tpu as pltpu

F32 = jnp.float32
BF16 = jnp.bfloat16

GRID_W = 64
A_HEADS = 8
A_HEAD_DIM = 64
A_W = 512
A_DECAY_RANK = 64
A_ICL_RANK = 64
A_GATE_RANK = 128
A_IN = 1920
A_DECAY_SCALE = math.exp(-0.5)
GN_EPS_RWKV = 64e-5
B_W = 512
B_BLOCKS = 8
B_C = 8.0
B_IN = 1024
C_HEADS = 4
C_HEAD_DIM = 128
C_W = 512
C_IN = 2048
ROPE_BASE = 10000.0
D_HEADS = 4
D_HEAD_DIM = 128
D_W = 512
D_IN = 2064
D_IN_PAD = 2176
N_EXPERTS = 32
TOP_K = 4
D_FF = 1024
SWIGLU_ALPHA = 1.702
SWIGLU_LIMIT = 7.0
LN_EPS = 1e-5
N_MOD = 6
LANES = 128

ROW_TILE = 256
RWKV_CHUNK = 64
RWKV_STEP_CHUNKS = 4
ODD_CHUNK = 256
EXPERT_TILE = 512
VMEM_LIMIT_BYTES = 56 * 1024 * 1024


def _params(*sem):
    return pltpu.CompilerParams(dimension_semantics=sem, vmem_limit_bytes=VMEM_LIMIT_BYTES)


def _dot(a, b):
    return jnp.dot(a.astype(BF16), b.astype(BF16), preferred_element_type=F32)


def _dot_nt(a, b):
    return lax.dot_general(a.astype(BF16), b.astype(BF16), (((1,), (1,)), ((), ())),
                           preferred_element_type=F32)


def _dot_tn(a, b):
    return lax.dot_general(a.astype(BF16), b.astype(BF16), (((0,), (0,)), ((), ())),
                           preferred_element_type=F32)


def _split(x):
    hi = x.astype(BF16)
    lo = (x - hi.astype(F32)).astype(BF16)
    return hi, lo


def _dot_x3(a, b):
    ah, al = _split(a)
    bh, bl = _split(b)
    out = jnp.dot(ah, bh, preferred_element_type=F32)
    out += jnp.dot(ah, bl, preferred_element_type=F32)
    out += jnp.dot(al, bh, preferred_element_type=F32)
    return out


def _dot_nt_x3(a, b):
    ah, al = _split(a)
    bh, bl = _split(b)
    dn = (((1,), (1,)), ((), ()))
    out = lax.dot_general(ah, bh, dn, preferred_element_type=F32)
    out += lax.dot_general(ah, bl, dn, preferred_element_type=F32)
    out += lax.dot_general(al, bh, dn, preferred_element_type=F32)
    return out


def _dot_exact_lhs(a, b):
    a = a.astype(BF16)
    b1 = b.astype(BF16)
    r1 = b - b1.astype(F32)
    b2 = r1.astype(BF16)
    b3 = (r1 - b2.astype(F32)).astype(BF16)
    out = jnp.dot(a, b1, preferred_element_type=F32)
    out += jnp.dot(a, b2, preferred_element_type=F32)
    out += jnp.dot(a, b3, preferred_element_type=F32)
    return out


def _sigmoid(x):
    return 1.0 / (1.0 + jnp.exp(-x))


def _softplus(x):
    return jnp.maximum(x, 0.0) + jnp.log(1.0 + jnp.exp(-jnp.abs(x)))


def _seg_order(i, d, n_ctx, n_tot):
    rev = jnp.where(i < n_ctx, n_ctx - 1 - i, n_ctx + n_tot - 1 - i)
    return jnp.where(d == 0, i, rev)


def _mod_kernel(c_ref, w_ref, b_ref, o_ref):
    c = c_ref[...]
    s = c * _sigmoid(c)
    o_ref[...] = _dot_x3(s, w_ref[...]) + b_ref[...]


def _mod_vectors(c8, w, b, layer):
    _, d, n = w.shape
    tn = n // 4
    return pl.pallas_call(
        _mod_kernel,
        grid=(n // tn,),
        in_specs=[pl.BlockSpec((8, d), lambda j: (0, 0)),
                  pl.BlockSpec((None, d, tn), lambda j: (layer, 0, j)),
                  pl.BlockSpec((1, tn), lambda j: (0, j))],
        out_specs=pl.BlockSpec((8, tn), lambda j: (0, j)),
        out_shape=jax.ShapeDtypeStruct((8, n), F32),
        compiler_params=_params("parallel"),
        name="mod_vectors",
    )(c8, w, b.reshape(1, n))


def _inproj_kernel(x_ref, sh_ref, sc_ref, w_ref, oa_ref, ob_ref, *, n_a):
    h = x_ref[0] * (1.0 + sc_ref[0, 0]) + sh_ref[0, 0]
    p = jnp.dot(h.astype(BF16), w_ref[...], preferred_element_type=F32)
    oa_ref[0] = p[:, :n_a]
    ob_ref[0] = p[:, n_a:]


def _inproj(x, shift, scale, w_bf16, n_a, n_ctx_tiles):
    bz, t, d = x.shape
    n = w_bf16.shape[1]
    n_b = n - n_a
    seg = lambda i: (i >= n_ctx_tiles).astype(jnp.int32)
    return pl.pallas_call(
        functools.partial(_inproj_kernel, n_a=n_a),
        grid=(bz, t // ROW_TILE),
        in_specs=[pl.BlockSpec((1, ROW_TILE, d), lambda b, i: (b, i, 0)),
                  pl.BlockSpec((1, 1, 1, d), lambda b, i: (b, seg(i), 0, 0)),
                  pl.BlockSpec((1, 1, 1, d), lambda b, i: (b, seg(i), 0, 0)),
                  pl.BlockSpec((d, n), lambda b, i: (0, 0))],
        out_specs=[pl.BlockSpec((1, ROW_TILE, n_a), lambda b, i: (b, i, 0)),
                   pl.BlockSpec((1, ROW_TILE, n_b), lambda b, i: (b, i, 0))],
        out_shape=[jax.ShapeDtypeStruct((bz, t, n_a), F32),
                   jax.ShapeDtypeStruct((bz, t, n_b), F32)],
        compiler_params=_params("parallel", "parallel"),
        name="inproj",
    )(x, shift, scale, w_bf16)


def _rwkv_prep_kernel(p_ref, up_ref, dn_ref, mu_ref, w0_ref, wb_ref, a0_ref, ab_ref, gb_ref,
                      kkw_ref, kaw_ref, rkw_ref, ones_ref,
                      r_o, v_o, kk_o, g_o, bonus_o, lw_o, kd_o, a_o, *, n_ctx_tiles, n_tiles):
    i = pl.program_id(1)
    p = p_ref[0]
    tr, w = p.shape
    row = lax.broadcasted_iota(jnp.int32, (tr, w), 0)
    lane = lax.broadcasted_iota(jnp.int32, (tr, w), 1)
    prev = pltpu.roll(p, 1, 0)
    nxt = pltpu.roll(p, tr - 1, 0)
    sh_ctx = jnp.where(lane % 2 == 0, jnp.where(row == 0, 0.0, prev), jnp.where(row == tr - 1, 0.0, nxt))
    col = row % GRID_W
    left = jnp.where(col == 0, 0.0, prev)
    right = jnp.where(col == GRID_W - 1, 0.0, nxt)
    up_halo = jnp.where(i > n_ctx_tiles, up_ref[0], 0.0)
    dn_halo = jnp.where(i < n_tiles - 1, dn_ref[0], 0.0)
    up = jnp.concatenate([up_halo, p[:tr - GRID_W]], axis=0)
    down = jnp.concatenate([p[GRID_W:], dn_halo], axis=0)
    slot = lane % 4
    sh_lat = jnp.where(slot == 0, left, jnp.where(slot == 1, right, jnp.where(slot == 2, up, down)))
    sh = jnp.where(i < n_ctx_tiles, sh_ctx, sh_lat)
    z = p + (sh - p) * mu_ref[...]

    r = z[:, 0:A_W]
    k = z[:, A_W:2 * A_W]
    v = z[:, 2 * A_W:3 * A_W]
    o = 3 * A_W
    wd = z[:, o:o + 2 * A_DECAY_RANK]
    o += 2 * A_DECAY_RANK
    ad = z[:, o:o + 2 * A_ICL_RANK]
    o += 2 * A_ICL_RANK
    zg = z[:, o:o + A_GATE_RANK]

    ones_bd = ones_ref[...]

    def head_sum(x):
        hi, lo = _split(x)
        return (jnp.dot(hi, ones_bd, preferred_element_type=F32)
                + jnp.dot(lo, ones_bd, preferred_element_type=F32))

    g_o[0] = _dot(_sigmoid(zg), gb_ref[...]).astype(BF16)
    kk = k * kkw_ref[...]
    kk = kk / jnp.maximum(jnp.sqrt(head_sum(kk * kk)), 1e-12)
    r_o[0] = r.astype(BF16)
    v_o[0] = v.astype(BF16)
    kk_o[0] = kk.astype(BF16)
    rr = r * rkw_ref[...]
    bsum = jnp.zeros_like(r)
    for d in range(2):
        wdd = jnp.tanh(wd[:, d * A_DECAY_RANK:(d + 1) * A_DECAY_RANK])
        lw = -A_DECAY_SCALE * _sigmoid(w0_ref[d] + _dot(wdd, wb_ref[d]))
        a = _sigmoid(a0_ref[d] + _dot(ad[:, d * A_ICL_RANK:(d + 1) * A_ICL_RANK], ab_ref[d]))
        kd = k * (1.0 + (a - 1.0) * kaw_ref[...])
        lw_o[d, 0] = lw
        kd_o[d, 0] = kd.astype(BF16)
        a_o[d, 0] = a.astype(BF16)
        bsum = bsum + head_sum(rr * kd)
    bonus_o[0] = (bsum * v).astype(BF16)


def _rwkv_prep(pa, mu, w0, wb, a0, ab, gb, kkw, kaw, rkw, n_ctx_tiles):
    bz, t, _ = pa.shape
    n_tiles = t // ROW_TILE
    hb = ROW_TILE // GRID_W
    n_hblk = t // GRID_W
    ones_bd = jnp.kron(jnp.eye(A_HEADS, dtype=F32), jnp.ones((A_HEAD_DIM, A_HEAD_DIM), F32)).astype(BF16)
    row = lambda a: a.reshape(1, -1)
    tok = pl.BlockSpec((1, ROW_TILE, A_W), lambda b, i: (b, i, 0))
    tok2 = pl.BlockSpec((2, 1, ROW_TILE, A_W), lambda b, i: (0, b, i, 0))
    full = lambda a: pl.BlockSpec(a.shape, lambda b, i: (0,) * a.ndim)
    args = (row(mu), w0.reshape(2, 1, A_W), wb.astype(BF16), a0.reshape(2, 1, A_W), ab.astype(BF16),
            gb.astype(BF16), row(kkw), row(kaw), row(rkw), ones_bd)
    s1 = jax.ShapeDtypeStruct((bz, t, A_W), BF16)
    s2 = jax.ShapeDtypeStruct((2, bz, t, A_W), BF16)
    s2f = jax.ShapeDtypeStruct((2, bz, t, A_W), F32)
    return pl.pallas_call(
        functools.partial(_rwkv_prep_kernel, n_ctx_tiles=n_ctx_tiles, n_tiles=n_tiles),
        grid=(bz, n_tiles),
        in_specs=[pl.BlockSpec((1, ROW_TILE, A_IN), lambda b, i: (b, i, 0)),
                  pl.BlockSpec((1, GRID_W, A_IN), lambda b, i: (b, jnp.maximum(i * hb - 1, 0), 0)),
                  pl.BlockSpec((1, GRID_W, A_IN), lambda b, i: (b, jnp.minimum((i + 1) * hb, n_hblk - 1), 0))]
                 + [full(a) for a in args],
        out_specs=[tok, tok, tok, tok, tok, tok2, tok2, tok2],
        out_shape=[s1, s1, s1, s1, s1, s2f, s2, s2],
        compiler_params=_params("parallel", "parallel"),
        name="rwkv_prep",
    )(pa, pa, pa, *args)


_BMM_NN = (((2,), (1,)), ((0,), (0,)))
_BMM_NT = (((2,), (2,)), ((0,), (0,)))
_BMM_TN = (((1,), (1,)), ((0,), (0,)))
RWKV_CHAINS = 2 * A_HEADS


def _bmm(a, b, dims):
    return lax.dot_general(a.astype(BF16), b.astype(BF16), dims, preferred_element_type=F32)


def _bmm_x3(a, b, dims):
    ah, al = _split(a)
    bh, bl = _split(b)
    rows = a.shape[1]
    p = lax.dot_general(jnp.concatenate([ah, al], axis=1), bh, dims, preferred_element_type=F32)
    return p[:, :rows] + p[:, rows:] + lax.dot_general(ah, bl, dims, preferred_element_type=F32)


def _split_heads(x):
    return jnp.stack([x[:, h * A_HEAD_DIM:(h + 1) * A_HEAD_DIM] for h in range(A_HEADS)], axis=0)


def _rwkv_scan_kernel(rf, lwf, kdf, vf, kkf, af, rb, lwb, kdb, vb, kkb, ab, of, ob, s_ref):
    i = pl.program_id(1)

    @pl.when(i == 0)
    def _():
        s_ref[...] = jnp.zeros_like(s_ref)

    n = RWKV_CHUNK
    n_sub = rf.shape[1] // n
    ti = lax.broadcasted_iota(jnp.int32, (n, n), 0)
    tj = lax.broadcasted_iota(jnp.int32, (n, n), 1)
    for c_idx in range(n_sub):
        rows_f = pl.ds(c_idx * n, n)
        rows_b = pl.ds((n_sub - 1 - c_idx) * n, n)
        _rwkv_chunk((rf, lwf, kdf, vf, kkf, af), (rb, lwb, kdb, vb, kkb, ab), of, ob, s_ref, rows_f, rows_b, ti, tj)


def _rwkv_chunk(fwd_refs, bwd_refs, of, ob, s_ref, rows_f, rows_b, ti, tj):
    n = RWKV_CHUNK

    def decayed(refs, rows, fwd):
        r_ref, lw_ref, kd_ref, v_ref, kk_ref, a_ref = refs
        lw = lw_ref[0, 0, rows, :]
        kd = kd_ref[0, 0, rows, :].astype(F32)
        kk = kk_ref[0, rows, :].astype(F32)
        b = kk * a_ref[0, 0, rows, :].astype(F32)
        incl = (tj <= ti) if fwd else (tj >= ti)
        c = _dot_exact_lhs(incl.astype(F32), lw)
        ctot = jnp.sum(lw, axis=0, keepdims=True)
        e_nc = jnp.exp(-c)
        e_tot = jnp.exp(ctot - c)
        return dict(rt=r_ref[0, rows, :].astype(F32) * jnp.exp(c), at=kk * jnp.exp(c - lw), kt=kd * e_nc,
                    bt=b * e_nc, kh=kd * e_tot, bh=b * e_tot, v=v_ref[0, rows, :].astype(F32), e=jnp.exp(ctot))

    pf = decayed(fwd_refs, rows_f, True)
    pb = decayed(bwd_refs, rows_b, False)

    def chains(name):
        return jnp.concatenate([_split_heads(pf[name]), _split_heads(pb[name])], axis=0)

    rt, at, kt, bt, kh, bh, v = (chains(x) for x in ("rt", "at", "kt", "bt", "kh", "bh", "v"))
    ar = jnp.concatenate([at, rt], axis=1)
    kb = jnp.concatenate([kt, bt], axis=1)
    m = _bmm(ar, kb, _BMM_NT)
    chain = lax.broadcasted_iota(jnp.int32, (RWKV_CHAINS, n, n), 0)
    lag = jnp.where(chain < A_HEADS, ti - tj, tj - ti)
    before = lag > 0
    before_eq = lag >= 0
    a_ak = jnp.where(before, m[:, :n, :n], 0.0)
    a_ab = jnp.where(before, m[:, :n, n:], 0.0)
    a_rk = jnp.where(before_eq, m[:, n:, :n], 0.0)
    a_rb = jnp.where(before_eq, m[:, n:, n:], 0.0)
    eye = (ti == tj).astype(F32)
    pw = -a_ab
    tinv = eye + pw
    for _ in range(int(math.log2(n)) - 1):
        pw = _bmm_x3(pw, pw, _BMM_NN)
        tinv = tinv + _bmm_x3(tinv, pw, _BMM_NN)
    sk = s_ref[...]
    a_s = _bmm(ar, sk, _BMM_NN)
    u = _bmm(tinv, a_s[:, :n] + _bmm(a_ak, v, _BMM_NN), _BMM_NN)
    vu = jnp.concatenate([v, -u], axis=1)
    out = a_s[:, n:] + _bmm(jnp.concatenate([a_rk, a_rb], axis=2), vu, _BMM_NN)
    e_rows = jnp.concatenate([_split_heads(pf["e"]), _split_heads(pb["e"])], axis=0)
    e_col = jnp.sum(eye * e_rows, axis=2, keepdims=True)
    s_ref[...] = sk * e_col + _bmm(jnp.concatenate([kh, bh], axis=1), vu, _BMM_TN)
    for h in range(A_HEADS):
        sl = slice(h * A_HEAD_DIM, (h + 1) * A_HEAD_DIM)
        of[0, rows_f, sl] = out[h]
        ob[0, rows_b, sl] = out[A_HEADS + h]


def _rwkv_scan(r, lw, kd, v, kk, a, ctx_len):
    bz, t, _ = r.shape
    blk = RWKV_CHUNK * RWKV_STEP_CHUNKS
    n_ctx = ctx_len // blk
    n_blocks = t // blk
    rev = lambda i: _seg_order(i, 1, n_ctx, n_blocks)
    tok_f = pl.BlockSpec((1, blk, A_W), lambda b, i: (b, i, 0))
    tok_b = pl.BlockSpec((1, blk, A_W), lambda b, i: (b, rev(i), 0))
    dir_f = pl.BlockSpec((1, 1, blk, A_W), lambda b, i: (0, b, i, 0))
    dir_b = pl.BlockSpec((1, 1, blk, A_W), lambda b, i: (1, b, rev(i), 0))
    shp = jax.ShapeDtypeStruct((bz, t, A_W), F32)
    return pl.pallas_call(
        _rwkv_scan_kernel,
        grid=(bz, n_blocks),
        in_specs=[tok_f, dir_f, dir_f, tok_f, tok_f, dir_f, tok_b, dir_b, dir_b, tok_b, tok_b, dir_b],
        out_specs=[tok_f, tok_b],
        out_shape=[shp, shp],
        scratch_shapes=[pltpu.VMEM((RWKV_CHAINS, A_HEAD_DIM, A_HEAD_DIM), F32)],
        compiler_params=_params("parallel", "arbitrary"),
        name="rwkv_scan",
    )(r, lw, kd, v, kk, a, r, lw, kd, v, kk, a)


def _rglru_kernel(p_ref, pv_ref, nx_ref, cw_ref, cb_ref, gw_ref, gbias_ref, lam_ref, o_ref,
                  a_scr, b_scr, h_scr, carry_ref, *, n_ctx_tiles, n_tiles):
    d = pl.program_id(1)
    i = pl.program_id(2)
    blk = _seg_order(i, d, n_ctx_tiles, n_tiles)

    @pl.when(i == 0)
    def _():
        carry_ref[...] = jnp.zeros_like(carry_ref)

    p = p_ref[0]
    tr = p.shape[0]
    x = p[:, :B_W]
    gate_in = p[:, B_W:]
    has_prev = jnp.logical_and(blk != 0, blk != n_ctx_tiles)
    has_next = jnp.logical_and(blk != n_ctx_tiles - 1, blk != n_tiles - 1)
    pv = jnp.where(has_prev, pv_ref[0][:, :B_W], 0.0)
    nx = jnp.where(has_next, nx_ref[0][:, :B_W], 0.0)
    row = lax.broadcasted_iota(jnp.int32, (tr, B_W), 0)
    pv_m1 = pv[7:8]
    pv_m2 = pv[6:7]
    nx_p1 = nx[0:1]
    x_m1 = jnp.where(row == 0, pv_m1, pltpu.roll(x, 1, 0))
    x_m2 = jnp.where(row == 0, pv_m2, jnp.where(row == 1, pv_m1, pltpu.roll(x, 2, 0)))
    x_p1 = jnp.where(row == tr - 1, nx_p1, pltpu.roll(x, tr - 1, 0))
    cw = cw_ref[...]
    u = cw[0:1] * x_m2 + cw[1:2] * x_m1 + cw[2:3] * x + cw[3:4] * x_p1 + cb_ref[...]
    gate = 0.5 * gate_in * (1.0 + jnp.tanh(math.sqrt(2.0 / math.pi) * (gate_in + 0.044715 * gate_in ** 3)))
    pre = _dot(u, gw_ref[0]) + gbias_ref[0]
    rec = _sigmoid(pre[:, :B_W])
    inp = _sigmoid(pre[:, B_W:])
    log_a = -B_C * rec * _softplus(-lam_ref[0])
    a_scr[...] = jnp.exp(log_a)
    b_scr[...] = jnp.sqrt(1.0 - jnp.exp(2.0 * log_a)) * inp * u

    n_grp = tr // 8
    g8 = lax.broadcasted_iota(jnp.int32, (8, B_W), 0)

    def group_scan(a, b, fwd):
        for s in (1, 2, 4):
            if fwd:
                ident = g8 < s
                a_p = pltpu.roll(a, s, 0)
                b_p = pltpu.roll(b, s, 0)
            else:
                ident = g8 >= 8 - s
                a_p = pltpu.roll(a, 8 - s, 0)
                b_p = pltpu.roll(b, 8 - s, 0)
            a_p = jnp.where(ident, 1.0, a_p)
            b_p = jnp.where(ident, 0.0, b_p)
            b = a * b_p + b
            a = a * a_p
        return a, b

    def run(fwd):
        def body(j, carry):
            g = j if fwd else n_grp - 1 - j
            off = pl.multiple_of(g * 8, 8)
            a, b = group_scan(a_scr[pl.ds(off, 8), :], b_scr[pl.ds(off, 8), :], fwd)
            h = b + a * carry
            h_scr[pl.ds(off, 8), :] = h
            return h[7:8] if fwd else h[0:1]
        carry_ref[...] = lax.fori_loop(0, n_grp, body, carry_ref[...], unroll=4)

    @pl.when(d == 0)
    def _():
        run(True)

    @pl.when(d == 1)
    def _():
        run(False)

    o_ref[0, 0] = (h_scr[...] * gate).astype(o_ref.dtype)


def _rglru(pb, conv_w, conv_b, gate_w, gate_b, lam, n_ctx_tiles):
    bz, t, _ = pb.shape
    n_tiles = t // ROW_TILE
    r8 = ROW_TILE // 8
    n8 = t // 8
    eye = jnp.eye(B_BLOCKS, dtype=F32)
    gw = jnp.einsum('dsgio,gh->dgisho', gate_w, eye).reshape(2, B_W, 2 * B_W).astype(BF16)
    gbias = gate_b.reshape(2, 1, 2 * B_W)
    blk = lambda i, d: _seg_order(i, d, n_ctx_tiles, n_tiles)
    return pl.pallas_call(
        functools.partial(_rglru_kernel, n_ctx_tiles=n_ctx_tiles, n_tiles=n_tiles),
        grid=(bz, 2, n_tiles),
        in_specs=[pl.BlockSpec((1, ROW_TILE, B_IN), lambda b, d, i: (b, blk(i, d), 0)),
                  pl.BlockSpec((1, 8, B_IN), lambda b, d, i: (b, jnp.maximum(blk(i, d) * r8 - 1, 0), 0)),
                  pl.BlockSpec((1, 8, B_IN), lambda b, d, i: (b, jnp.minimum((blk(i, d) + 1) * r8, n8 - 1), 0)),
                  pl.BlockSpec((4, B_W), lambda b, d, i: (0, 0)),
                  pl.BlockSpec((1, B_W), lambda b, d, i: (0, 0)),
                  pl.BlockSpec((1, B_W, 2 * B_W), lambda b, d, i: (d, 0, 0)),
                  pl.BlockSpec((1, 1, 2 * B_W), lambda b, d, i: (d, 0, 0)),
                  pl.BlockSpec((1, 1, B_W), lambda b, d, i: (d, 0, 0))],
        out_specs=pl.BlockSpec((1, 1, ROW_TILE, B_W), lambda b, d, i: (d, b, blk(i, d), 0)),
        out_shape=jax.ShapeDtypeStruct((2, bz, t, B_W), BF16),
        scratch_shapes=[pltpu.VMEM((ROW_TILE, B_W), F32), pltpu.VMEM((ROW_TILE, B_W), F32),
                        pltpu.VMEM((ROW_TILE, B_W), F32), pltpu.VMEM((1, B_W), F32)],
        compiler_params=_params("parallel", "parallel", "arbitrary"),
        name="rglru",
    )(pb, pb, pb, conv_w, conv_b.reshape(1, B_W), gw, gbias, lam.reshape(2, 1, B_W))


def _retention_kernel(lg_ref, p_ref, cos_ref, sin_ref, o_ref, st_ref):
    d = pl.program_id(1)
    i = pl.program_id(2)

    @pl.when(i == 0)
    def _():
        st_ref[...] = jnp.zeros_like(st_ref)

    n = p_ref.shape[1]
    cos2 = cos_ref[...]
    sin2 = sin_ref[...]
    ti = lax.broadcasted_iota(jnp.int32, (n, n), 0)
    tj = lax.broadcasted_iota(jnp.int32, (n, n), 1)
    fwd = d == 0
    diff = jnp.where(fwd, ti - tj, tj - ti).astype(F32)
    t1 = lax.broadcasted_iota(jnp.int32, (n, 1), 0)
    pos = jnp.where(fwd, t1, n - 1 - t1).astype(F32)
    for h in range(C_HEADS):
        lg = lg_ref[d, h] * jnp.ones((1, 1), F32)
        sl = slice(h * C_HEAD_DIM, (h + 1) * C_HEAD_DIM)
        q = p_ref[0, :, sl]
        k = p_ref[0, :, C_W + h * C_HEAD_DIM:C_W + (h + 1) * C_HEAD_DIM]
        v = p_ref[0, :, 2 * C_W + h * C_HEAD_DIM:2 * C_W + (h + 1) * C_HEAD_DIM]
        q = q * cos2 + pltpu.roll(q, C_HEAD_DIM // 2, 1) * sin2
        k = (k * cos2 + pltpu.roll(k, C_HEAD_DIM // 2, 1) * sin2) * (C_HEAD_DIM ** -0.5)
        decay = jnp.where(diff >= 0, jnp.exp(lg * jnp.maximum(diff, 0.0)), 0.0)
        scores = _dot_nt(q, k) * decay
        st = st_ref[h]
        out = _dot(scores, v) + _dot(q, st) * jnp.exp(lg * (pos + 1.0))
        zeta = jnp.exp(lg * (n - 1.0 - pos))
        st_ref[h] = st * jnp.exp(lg * n) + _dot_tn(k * zeta, v)
        o_ref[0, 0, :, sl] = out.astype(o_ref.dtype)


def _retention(pc, log_gamma, cos2, sin2, ctx_len):
    bz, t, _ = pc.shape
    n_ctx = ctx_len // ODD_CHUNK
    n_chunks = t // ODD_CHUNK
    blk = lambda i, d: _seg_order(i, d, n_ctx, n_chunks)
    return pl.pallas_call(
        _retention_kernel,
        grid_spec=pltpu.PrefetchScalarGridSpec(
            num_scalar_prefetch=1,
            grid=(bz, 2, n_chunks),
            in_specs=[pl.BlockSpec((1, ODD_CHUNK, 3 * C_W), lambda b, d, i, lg: (b, blk(i, d), 0)),
                      pl.BlockSpec((ODD_CHUNK, C_HEAD_DIM), lambda b, d, i, lg: (blk(i, d), 0)),
                      pl.BlockSpec((ODD_CHUNK, C_HEAD_DIM), lambda b, d, i, lg: (blk(i, d), 0))],
            out_specs=pl.BlockSpec((1, 1, ODD_CHUNK, C_W), lambda b, d, i, lg: (d, b, blk(i, d), 0)),
            scratch_shapes=[pltpu.VMEM((C_HEADS, C_HEAD_DIM, C_HEAD_DIM), F32)]),
        out_shape=jax.ShapeDtypeStruct((2, bz, t, C_W), BF16),
        compiler_params=_params("parallel", "parallel", "arbitrary"),
        name="retention",
    )(log_gamma, pc, cos2, sin2)


def _mlstm_kernel(p_ref, bias_ref, o_ref, c_ref, n_ref, m_ref):
    d = pl.program_id(1)
    i = pl.program_id(2)

    @pl.when(i == 0)
    def _():
        c_ref[...] = jnp.zeros_like(c_ref)
        n_ref[...] = jnp.zeros_like(n_ref)
        m_ref[...] = jnp.full_like(m_ref, -jnp.inf)

    n = p_ref.shape[1]
    ti = lax.broadcasted_iota(jnp.int32, (n, n), 0)
    tj = lax.broadcasted_iota(jnp.int32, (n, n), 1)
    fwd = d == 0
    before_eq = jnp.where(fwd, ti - tj, tj - ti) >= 0
    gts = p_ref[0, :, 4 * D_W:] + bias_ref[...]
    gts = jnp.where(fwd, gts, pltpu.roll(gts, LANES - D_HEADS, 1))
    lf = jnp.minimum(gts, 0.0) - jnp.log(1.0 + jnp.exp(-jnp.abs(gts)))
    bcum = _dot_exact_lhs(before_eq.astype(F32), lf)
    btot = jnp.sum(lf, axis=0, keepdims=True)
    gts_t = gts.T
    bcum_t = bcum.T
    for h in range(D_HEADS):
        sl = slice(h * D_HEAD_DIM, (h + 1) * D_HEAD_DIM)
        q = p_ref[0, :, sl]
        k = p_ref[0, :, D_W + h * D_HEAD_DIM:D_W + (h + 1) * D_HEAD_DIM] * (D_HEAD_DIM ** -0.5)
        v = p_ref[0, :, 2 * D_W + h * D_HEAD_DIM:2 * D_W + (h + 1) * D_HEAD_DIM]
        ig_col = gts[:, h:h + 1]
        ig_row = gts_t[h:h + 1, :]
        b_col = bcum[:, 2 * D_HEADS + h:2 * D_HEADS + h + 1]
        b_row = bcum_t[2 * D_HEADS + h:2 * D_HEADS + h + 1, :]
        b_end = btot[:, 2 * D_HEADS + h:2 * D_HEADS + h + 1]
        m_prev = m_ref[h][0:1, 0:1]
        d_log = jnp.where(before_eq, b_col - b_row + ig_row, -jnp.inf)
        g_inter = b_col + m_prev
        m_t = jnp.maximum(g_inter, jnp.max(d_log, axis=1, keepdims=True))
        s_intra = _dot_nt(q, k) * jnp.exp(d_log - m_t)
        s_inter = jnp.exp(g_inter - m_t)
        num = _dot(s_intra, v) + _dot(q, c_ref[h]) * s_inter
        den = jnp.sum(s_intra, axis=1, keepdims=True) + jnp.sum(q * n_ref[h], axis=1, keepdims=True) * s_inter
        den = jnp.maximum(jnp.abs(den), jnp.exp(-m_t))
        o_ref[0, 0, :, sl] = (num / den).astype(o_ref.dtype)
        w_end = b_end - b_col + ig_col
        m_i = jnp.max(w_end, axis=0, keepdims=True)
        e_end = jnp.exp(w_end - m_i)
        ke = k * e_end
        m_new = jnp.maximum(b_end + m_prev, m_i)
        s_old = jnp.exp(b_end + m_prev - m_new)
        s_new = jnp.exp(m_i - m_new)
        c_ref[h] = c_ref[h] * s_old + _dot_tn(ke, v) * s_new
        n_ref[h] = n_ref[h] * s_old + jnp.sum(ke, axis=0, keepdims=True) * s_new
        m_ref[h] = jnp.broadcast_to(m_new, m_ref.shape[1:])


def _mlstm(pd, bias128, ctx_len):
    bz, t, w = pd.shape
    n_ctx = ctx_len // ODD_CHUNK
    n_chunks = t // ODD_CHUNK
    blk = lambda i, d: _seg_order(i, d, n_ctx, n_chunks)
    return pl.pallas_call(
        _mlstm_kernel,
        grid=(bz, 2, n_chunks),
        in_specs=[pl.BlockSpec((1, ODD_CHUNK, w), lambda b, d, i: (b, blk(i, d), 0)),
                  pl.BlockSpec((1, LANES), lambda b, d, i: (0, 0))],
        out_specs=pl.BlockSpec((1, 1, ODD_CHUNK, D_W), lambda b, d, i: (d, b, blk(i, d), 0)),
        out_shape=jax.ShapeDtypeStruct((2, bz, t, D_W), BF16),
        scratch_shapes=[pltpu.VMEM((D_HEADS, D_HEAD_DIM, D_HEAD_DIM), F32),
                        pltpu.VMEM((D_HEADS, 1, D_HEAD_DIM), F32),
                        pltpu.VMEM((D_HEADS, 8, LANES), F32)],
        compiler_params=_params("parallel", "parallel", "arbitrary"),
        name="mlstm",
    )(pd, bias128)


def _group_norm_lanes(x, n_groups, eps):
    gw = x.shape[1] // n_groups
    outs = []
    for g in range(n_groups):
        xg = x[:, g * gw:(g + 1) * gw]
        mu = jnp.mean(xg, axis=1, keepdims=True)
        xc = xg - mu
        var = jnp.mean(xc * xc, axis=1, keepdims=True)
        outs.append(xc * lax.rsqrt(var + eps))
    return jnp.concatenate(outs, axis=1)


def _post_mixer(ycat, x_ref, wout_ref, mods_ref, lng_ref, lnb_ref, rw_ref, rb_ref, x1_o, h2_o, topi_o, topw_o, rank_o, cnt_o, cnt_scr, alpha):
    y = jnp.dot(ycat.astype(BF16), wout_ref[...], preferred_element_type=F32)
    mods = mods_ref[0, 0]
    x1 = alpha * x_ref[0] + mods[2:3] * y
    mu = jnp.mean(x1, axis=1, keepdims=True)
    xc = x1 - mu
    var = jnp.mean(xc * xc, axis=1, keepdims=True)
    x1 = xc * lax.rsqrt(var + LN_EPS) * lng_ref[...] + lnb_ref[...]
    x1_o[0] = x1
    h2 = x1 * (1.0 + mods[4:5]) + mods[3:4]
    h2_o[0] = h2
    logits = _dot_x3(h2, rw_ref[...]) + rb_ref[...]
    lane = lax.broadcasted_iota(jnp.int32, logits.shape, 1)
    work = jnp.where(lane < N_EXPERTS, logits, -jnp.inf)
    top_i = jnp.zeros(logits.shape, jnp.int32)
    top_v = jnp.full(logits.shape, -jnp.inf, F32)
    top = None
    picks = []
    for kth in range(TOP_K):
        mk = jnp.max(work, axis=1, keepdims=True)
        idx = jnp.min(jnp.where(work == mk, lane, LANES), axis=1, keepdims=True)
        pick = lane == idx
        picks.append(pick)
        work = jnp.where(pick, -jnp.inf, work)
        top_i = jnp.where(lane == kth, idx, top_i)
        top_v = jnp.where(lane == kth, mk, top_v)
        if kth == 0:
            top = mk
    e = jnp.exp(top_v - top)
    topi_o[0] = top_i
    topw_o[0] = e / jnp.sum(e, axis=1, keepdims=True)

    @pl.when(jnp.logical_and(pl.program_id(0) == 0, pl.program_id(1) == 0))
    def _():
        cnt_scr[...] = jnp.zeros_like(cnt_scr)

    rows = logits.shape[0]
    chosen = jnp.zeros(logits.shape, F32)
    for pick in picks:
        chosen = jnp.where(pick, 1.0, chosen)
    ti = lax.broadcasted_iota(jnp.int32, (rows, rows), 0)
    tj = lax.broadcasted_iota(jnp.int32, (rows, rows), 1)
    earlier = jnp.dot((tj < ti).astype(BF16), chosen.astype(BF16), preferred_element_type=F32) + cnt_scr[...]
    rank = jnp.zeros(logits.shape, F32)
    for kth, pick in enumerate(picks):
        rank = jnp.where(lane == kth, jnp.sum(jnp.where(pick, earlier, 0.0), axis=1, keepdims=True), rank)
    rank_o[0] = rank.astype(jnp.int32)
    cnt_scr[...] = cnt_scr[...] + jnp.sum(chosen, axis=0, keepdims=True)
    cnt_o[...] = cnt_scr[...].astype(jnp.int32)


def _outproj_even_kernel(wkvf_ref, wkvb_ref, bonus_ref, g_ref, gng_ref, gnb_ref, hb_ref,
                         x_ref, wout_ref, mods_ref, lng_ref, lnb_ref, rw_ref, rb_ref,
                         x1_o, h2_o, topi_o, topw_o, rank_o, cnt_o, cnt_scr, *, alpha):
    wkv = wkvf_ref[0] + wkvb_ref[0]
    ya = ((_group_norm_lanes(wkv, A_HEADS, GN_EPS_RWKV) * gng_ref[...] + gnb_ref[...] + bonus_ref[0].astype(F32))
          * g_ref[0].astype(F32))
    yb = hb_ref[0, 0].astype(F32) + hb_ref[1, 0].astype(F32)
    ycat = jnp.concatenate([ya, yb], axis=1)
    _post_mixer(ycat, x_ref, wout_ref, mods_ref, lng_ref, lnb_ref, rw_ref, rb_ref, x1_o, h2_o, topi_o, topw_o, rank_o, cnt_o, cnt_scr, alpha)


def _outproj_odd_kernel(oc_ref, pc_ref, cg_ref, cb_ref, od_ref, pd_ref, dg_ref, db_ref,
                        x_ref, wout_ref, mods_ref, lng_ref, lnb_ref, rw_ref, rb_ref,
                        x1_o, h2_o, topi_o, topw_o, rank_o, cnt_o, cnt_scr, *, alpha):
    oc = oc_ref[0, 0].astype(F32) + oc_ref[1, 0].astype(F32)
    gc = pc_ref[0]
    ya = (_group_norm_lanes(oc, C_HEADS, LN_EPS) * cg_ref[...] + cb_ref[...]) * (gc * _sigmoid(gc))
    od = od_ref[0, 0].astype(F32) + od_ref[1, 0].astype(F32)
    yb = (_group_norm_lanes(od, D_HEADS, LN_EPS) * dg_ref[...] + db_ref[...]) * _sigmoid(pd_ref[0])
    ycat = jnp.concatenate([ya, yb], axis=1)
    _post_mixer(ycat, x_ref, wout_ref, mods_ref, lng_ref, lnb_ref, rw_ref, rb_ref, x1_o, h2_o, topi_o, topw_o, rank_o, cnt_o, cnt_scr, alpha)


def _outproj(kind, mixer_args, mixer_specs, x, wout, mods, ln_g, ln_b, rw, rb, n_ctx_tiles, alpha, first_tile):
    bz, t, d = x.shape
    t_out = t - first_tile * ROW_TILE
    seg = lambda i: (i + first_tile >= n_ctx_tiles).astype(jnp.int32)
    row = lambda a: a.reshape(1, -1)
    vec = lambda n: pl.BlockSpec((1, n), lambda b, i: (0, 0))
    kern = _outproj_even_kernel if kind == 'even' else _outproj_odd_kernel
    tok = lambda n: pl.BlockSpec((1, ROW_TILE, n), lambda b, i: (b, i, 0))
    return pl.pallas_call(
        functools.partial(kern, alpha=alpha),
        grid=(bz, t_out // ROW_TILE),
        in_specs=list(mixer_specs) + [
            pl.BlockSpec((1, ROW_TILE, d), lambda b, i: (b, i + first_tile, 0)),
            pl.BlockSpec((d, d), lambda b, i: (0, 0)),
            pl.BlockSpec((1, 1, 8, d), lambda b, i: (b, seg(i), 0, 0)),
            vec(d), vec(d),
            pl.BlockSpec((d, LANES), lambda b, i: (0, 0)),
            vec(LANES)],
        out_specs=[tok(d), tok(d), tok(LANES), tok(LANES), tok(LANES), vec(LANES)],
        out_shape=[jax.ShapeDtypeStruct((bz, t_out, d), F32),
                   jax.ShapeDtypeStruct((bz, t_out, d), F32),
                   jax.ShapeDtypeStruct((bz, t_out, LANES), jnp.int32),
                   jax.ShapeDtypeStruct((bz, t_out, LANES), F32),
                   jax.ShapeDtypeStruct((bz, t_out, LANES), jnp.int32),
                   jax.ShapeDtypeStruct((1, LANES), jnp.int32)],
        scratch_shapes=[pltpu.VMEM((1, LANES), F32)],
        compiler_params=_params("arbitrary", "arbitrary"),
        name="outproj_" + kind,
    )(*mixer_args, x, wout, mods, row(ln_g), row(ln_b), rw, rb)


def _route(top_i, rank, counts):
    experts = jnp.arange(N_EXPERTS, dtype=jnp.int32)
    e_flat = top_i[..., :TOP_K].reshape(-1)
    n_asg = e_flat.shape[0]
    cnt = counts[0, :N_EXPERTS]
    start = jnp.cumsum(cnt) - cnt
    dest = rank[..., :TOP_K].reshape(-1) + jnp.sum(jnp.where(e_flat[:, None] == experts, start, 0), axis=1)
    first_t = start // EXPERT_TILE
    last_t = (start + cnt - 1) // EXPERT_TILE
    n_t = jnp.where(cnt > 0, last_t - first_t + 1, 0)
    w_end = jnp.cumsum(n_t)
    n_work = w_end[-1]
    w = jnp.minimum(jnp.arange(n_asg // EXPERT_TILE + N_EXPERTS - 1, dtype=jnp.int32), n_work - 1)
    we = jnp.sum((w_end[None, :] <= w[:, None]).astype(jnp.int32), axis=1)
    pick = we[:, None] == experts
    sel = lambda v: jnp.sum(jnp.where(pick, v, 0), axis=1)
    wt = sel(first_t) + (w - sel(w_end - n_t))
    lo = sel(start)
    hi = lo + sel(cnt)
    slot = sel(jnp.cumsum((cnt > 0).astype(jnp.int32)) - 1) % 2
    w_next = sel(w_end)
    nxt = jnp.where(w_next < n_work, jnp.sum((w_end[None, :] <= w_next[:, None]).astype(jnp.int32), axis=1), -1)
    i32 = lambda v: v.astype(jnp.int32)
    return i32(dest), (i32(wt), i32(we), i32(lo), i32(hi), i32(n_work.reshape(1)), i32(slot), i32(nxt))


def _dispatch_kernel(dest_ref, h_ref, xs_hbm, stage, sems):
    i = pl.program_id(0)
    rows = h_ref.shape[0]
    base = i * rows * TOP_K

    def drain(s):
        for _ in range(TOP_K):
            pltpu.make_async_copy(stage.at[s], xs_hbm.at[pl.ds(0, rows)], sems.at[s]).wait()

    for s in range(2):
        @pl.when(i % 2 == s)
        def _(s=s):
            stage[s] = h_ref[...]

            def issue(t, carry):
                for k in range(TOP_K):
                    row = dest_ref[base + t * TOP_K + k]
                    pltpu.make_async_copy(stage.at[s, pl.ds(t, 1)], xs_hbm.at[pl.ds(row, 1)], sems.at[s]).start()
                return carry

            lax.fori_loop(0, rows, issue, 0)

            @pl.when(i > 0)
            def _():
                drain(1 - s)

            @pl.when(i == pl.num_programs(0) - 1)
            def _():
                drain(s)


def _dispatch(h2, dest):
    n, d = h2.shape
    return pl.pallas_call(
        _dispatch_kernel,
        grid_spec=pltpu.PrefetchScalarGridSpec(
            num_scalar_prefetch=1,
            grid=(n // ROW_TILE,),
            in_specs=[pl.BlockSpec((ROW_TILE, d), lambda i, dest: (i, 0))],
            out_specs=pl.BlockSpec(memory_space=pl.ANY),
            scratch_shapes=[pltpu.VMEM((2, ROW_TILE, d), F32), pltpu.SemaphoreType.DMA((2,))]),
        out_shape=jax.ShapeDtypeStruct((n * TOP_K, d), F32),
        compiler_params=_params("arbitrary"),
        name="moe_dispatch",
    )(dest, h2)


def _experts_kernel(wt_ref, we_ref, lo_ref, hi_ref, nw_ref, slot_ref, nxt_ref, xs_ref, w1_hbm, b1_ref, w2_hbm,
                    b2_ref, y_ref, w1f, w2f, w1b, w2b, sems, *, f_chunk, layer):
    w = pl.program_id(0)
    valid = w < nw_ref[0]
    prev = jnp.maximum(w - 1, 0)
    new_expert = jnp.logical_or(w == 0, we_ref[w] != we_ref[prev])
    new_tile = jnp.logical_or(w == 0, wt_ref[w] != wt_ref[prev])

    def weight_copies(e, s):
        return (pltpu.make_async_copy(w1_hbm.at[layer, e], w1f.at[s], sems.at[0, s]),
                pltpu.make_async_copy(w2_hbm.at[layer, e], w2f.at[s], sems.at[1, s]))

    @pl.when(w == 0)
    def _():
        for cp in weight_copies(we_ref[0], slot_ref[0]):
            cp.start()

    @pl.when(jnp.logical_and(valid, new_expert))
    def _():
        s = slot_ref[w]
        for cp in weight_copies(we_ref[w], s):
            cp.wait()
        w1b[...] = w1f[s].astype(BF16)
        w2b[...] = w2f[s].astype(BF16)

        @pl.when(nxt_ref[w] >= 0)
        def _():
            for cp in weight_copies(nxt_ref[w], 1 - s):
                cp.start()

    @pl.when(valid)
    def _():
        x = xs_ref[...].astype(BF16)
        acc = jnp.zeros(y_ref.shape, F32)
        for c in range(D_FF // f_chunk):
            c0 = c * f_chunk
            glu = jnp.dot(x, w1b[:, c0:c0 + f_chunk], preferred_element_type=F32) + b1_ref[0, :, c0:c0 + f_chunk]
            lin = (jnp.dot(x, w1b[:, D_FF + c0:D_FF + c0 + f_chunk], preferred_element_type=F32)
                   + b1_ref[0, :, D_FF + c0:D_FF + c0 + f_chunk])
            glu = jnp.minimum(glu, SWIGLU_LIMIT)
            lin = jnp.clip(lin, -SWIGLU_LIMIT, SWIGLU_LIMIT)
            act = (lin + 1.0) * glu * _sigmoid(SWIGLU_ALPHA * glu)
            acc += jnp.dot(act.astype(BF16), w2b[c0:c0 + f_chunk, :], preferred_element_type=F32)
        y = acc + b2_ref[0]
        rows = wt_ref[w] * EXPERT_TILE + lax.broadcasted_iota(jnp.int32, (EXPERT_TILE, 1), 0)
        mine = jnp.logical_and(rows >= lo_ref[w], rows < hi_ref[w])

        @pl.when(new_tile)
        def _():
            y_ref[...] = jnp.where(mine, y, 0.0)

        @pl.when(jnp.logical_not(new_tile))
        def _():
            y_ref[...] = jnp.where(mine, y, y_ref[...])


def _experts(xs, work, w1, b1, w2, b2, layer):
    n_asg, d = xs.shape
    n_l, n_e = w1.shape[:2]
    n_work_max = work[0].shape[0]
    return pl.pallas_call(
        functools.partial(_experts_kernel, f_chunk=512, layer=layer),
        grid_spec=pltpu.PrefetchScalarGridSpec(
            num_scalar_prefetch=7,
            grid=(n_work_max,),
            in_specs=[pl.BlockSpec((EXPERT_TILE, d), lambda w, wt, we, *_: (wt[w], 0)),
                      pl.BlockSpec(memory_space=pl.ANY),
                      pl.BlockSpec((None, 1, 1, 2 * D_FF), lambda w, wt, we, *_: (layer, we[w], 0, 0)),
                      pl.BlockSpec(memory_space=pl.ANY),
                      pl.BlockSpec((None, 1, 1, d), lambda w, wt, we, *_: (layer, we[w], 0, 0))],
            out_specs=pl.BlockSpec((EXPERT_TILE, d), lambda w, wt, we, *_: (wt[w], 0)),
            scratch_shapes=[pltpu.VMEM((2, d, 2 * D_FF), F32), pltpu.VMEM((2, D_FF, d), F32),
                            pltpu.VMEM((d, 2 * D_FF), BF16), pltpu.VMEM((D_FF, d), BF16),
                            pltpu.SemaphoreType.DMA((2, 2))]),
        out_shape=jax.ShapeDtypeStruct((n_asg, d), F32),
        compiler_params=_params("arbitrary"),
        name="moe_experts",
    )(*work, xs, w1, b1.reshape(n_l, n_e, 1, -1), w2, b2.reshape(n_l, n_e, 1, -1))


def _combine_kernel(dest_ref, y_hbm, topw_ref, x_ref, mods_ref, lng_ref, lnb_ref, o_ref, ybuf, sems, *, alpha):
    rows = x_ref.shape[1]
    tile = pl.program_id(0) * pl.num_programs(1) + pl.program_id(1)
    n_tiles = pl.num_programs(0) * pl.num_programs(1)

    def gather(which, s):
        base = which * rows * TOP_K

        def issue(t, carry):
            for k in range(TOP_K):
                row = dest_ref[base + t * TOP_K + k]
                pltpu.make_async_copy(y_hbm.at[pl.ds(row, 1)], ybuf.at[s, k, pl.ds(t, 1)], sems.at[s]).start()
            return carry

        lax.fori_loop(0, rows, issue, 0)

    @pl.when(tile == 0)
    def _():
        gather(0, 0)

    for s in range(2):
        @pl.when(tile % 2 == s)
        def _(s=s):
            @pl.when(tile + 1 < n_tiles)
            def _():
                gather(tile + 1, 1 - s)

            for k in range(TOP_K):
                pltpu.make_async_copy(y_hbm.at[pl.ds(0, rows)], ybuf.at[s, k], sems.at[s]).wait()
            tw = topw_ref[0]
            ffn = ybuf[s, 0] * tw[:, 0:1]
            for k in range(1, TOP_K):
                ffn = ffn + ybuf[s, k] * tw[:, k:k + 1]
            mods = mods_ref[0, 0]
            x2 = alpha * x_ref[0] + mods[5:6] * ffn
            mu = jnp.mean(x2, axis=1, keepdims=True)
            xc = x2 - mu
            var = jnp.mean(xc * xc, axis=1, keepdims=True)
            o_ref[0] = xc * lax.rsqrt(var + LN_EPS) * lng_ref[...] + lnb_ref[...]


def _combine(y_sorted, dest, topw, x1, mods, ln_g, ln_b, n_ctx_tiles, alpha, first_tile):
    bz, t, d = x1.shape
    seg = lambda i: (i + first_tile >= n_ctx_tiles).astype(jnp.int32)
    tok = lambda n: pl.BlockSpec((1, ROW_TILE, n), lambda b, i, dest: (b, i, 0))
    vec = pl.BlockSpec((1, d), lambda b, i, dest: (0, 0))
    return pl.pallas_call(
        functools.partial(_combine_kernel, alpha=alpha),
        grid_spec=pltpu.PrefetchScalarGridSpec(
            num_scalar_prefetch=1,
            grid=(bz, t // ROW_TILE),
            in_specs=[pl.BlockSpec(memory_space=pl.ANY), tok(LANES), tok(d),
                      pl.BlockSpec((1, 1, 8, d), lambda b, i, dest: (b, seg(i), 0, 0)), vec, vec],
            out_specs=tok(d),
            scratch_shapes=[pltpu.VMEM((2, TOP_K, ROW_TILE, d), F32), pltpu.SemaphoreType.DMA((2,))]),
        out_shape=jax.ShapeDtypeStruct((bz, t, d), F32),
        compiler_params=_params("arbitrary", "arbitrary"),
        name="moe_combine",
    )(dest, y_sorted, topw, x1, mods, ln_g.reshape(1, d), ln_b.reshape(1, d))


def _rope_tables(ctx_len, seq):
    idx = jnp.arange(seq)
    rowp = (idx // GRID_W).astype(F32)
    colp = (idx % GRID_W).astype(F32)
    n_freq = C_HEAD_DIM // 4
    inv = ROPE_BASE ** (-jnp.arange(n_freq, dtype=F32) / n_freq)
    ang = jnp.concatenate([rowp[:, None] * inv, colp[:, None] * inv], -1)
    cos, sin = jnp.cos(ang), jnp.sin(ang)
    cos2 = jnp.concatenate([cos, cos], -1)
    sin2 = jnp.concatenate([-sin, sin], -1)
    cos2 = jnp.concatenate([jnp.ones((ctx_len, C_HEAD_DIM), F32), cos2], 0)
    sin2 = jnp.concatenate([jnp.zeros((ctx_len, C_HEAD_DIM), F32), sin2], 0)
    return cos2, sin2


def kernel(x, c, ctx, c_ctx, w_mod, b_mod, ln_g, ln_b, even_w_in, even_w_out, a_mu, a_w0, a_wB, a_a0, a_aB, a_gB, a_kk, a_ka, a_rk, a_gn_g, a_gn_b, b_conv_w, b_conv_b, b_gate_w, b_gate_b, b_lam, odd_w_in, odd_w_out, c_log_gamma, c_gn_g, c_gn_b, d_ibias, d_fbias, d_gn_g, d_gn_b, router_w, router_b, exp_w1, exp_b1, exp_w2, exp_b2):
    bz, seq, d = x.shape
    ctx_len = ctx.shape[1]
    depth = w_mod.shape[0]
    assert ctx_len == ROW_TILE and seq % ROW_TILE == 0
    t = ctx_len + seq
    n_ctx_tiles = ctx_len // ROW_TILE
    alpha = (2 * depth) ** 0.25
    xs = jnp.concatenate([ctx, x], axis=1)
    c8 = jnp.concatenate([c, c_ctx[None], jnp.zeros((8 - bz - 1, d), F32)], 0)
    cos2, sin2 = _rope_tables(ctx_len, seq)
    row = lambda a: a.reshape(1, -1)
    vec = lambda n: pl.BlockSpec((1, n), lambda b, i: (0, 0))

    for layer in range(depth):
        first = n_ctx_tiles if layer == depth - 1 else 0
        tok = lambda n, first=first: pl.BlockSpec((1, ROW_TILE, n), lambda b, i: (b, i + first, 0))
        tok2 = lambda n, first=first: pl.BlockSpec((2, 1, ROW_TILE, n), lambda b, i: (0, b, i + first, 0))
        m = _mod_vectors(c8, w_mod, b_mod[layer], layer).reshape(8, N_MOD, d)
        m_lat = m[:bz]
        m_ctx = jnp.broadcast_to(m[bz][None], (bz, N_MOD, d))
        mods = jnp.stack([m_ctx, m_lat], axis=1)
        mods = jnp.concatenate([mods, jnp.zeros((bz, 2, 8 - N_MOD, d), F32)], axis=2)
        shift = mods[:, :, 0:1]
        scale = mods[:, :, 1:2]
        rw = jnp.pad(router_w[layer], ((0, 0), (0, LANES - N_EXPERTS)))
        rb = jnp.pad(router_b[layer], (0, LANES - N_EXPERTS)).reshape(1, LANES)
        li = layer // 2
        if layer % 2 == 0:
            pa, pb = _inproj(xs, shift, scale, even_w_in[li].astype(BF16), A_IN, n_ctx_tiles)
            r, v, kk, g, bonus, lw, kd, a = _rwkv_prep(pa, a_mu[li], a_w0[li], a_wB[li], a_a0[li], a_aB[li],
                                                       a_gB[li], a_kk[li], a_ka[li], a_rk[li].reshape(-1),
                                                       n_ctx_tiles)
            wkv_f, wkv_b = _rwkv_scan(r, lw, kd, v, kk, a, ctx_len)
            hb = _rglru(pb, b_conv_w[li], b_conv_b[li], b_gate_w[li], b_gate_b[li], b_lam[li], n_ctx_tiles)
            mixer_args = (wkv_f, wkv_b, bonus, g, row(a_gn_g[li]), row(a_gn_b[li]), hb)
            mixer_specs = (tok(A_W), tok(A_W), tok(A_W), tok(A_W), vec(A_W), vec(A_W), tok2(B_W))
            x1, h2, topi, topw, rank, counts = _outproj('even', mixer_args, mixer_specs, xs, even_w_out[li].astype(BF16), mods,
                                     ln_g[layer, 0], ln_b[layer, 0], rw, rb, n_ctx_tiles, alpha, first)
        else:
            w_in = jnp.pad(odd_w_in[li], ((0, 0), (0, D_IN_PAD - D_IN))).astype(BF16)
            pc, pd = _inproj(xs, shift, scale, w_in, C_IN, n_ctx_tiles)
            oc = _retention(pc, c_log_gamma[li], cos2, sin2, ctx_len)
            bias128 = jnp.concatenate([d_ibias[li].reshape(-1), d_fbias[li].reshape(-1),
                                       jnp.zeros((LANES - 4 * D_HEADS,), F32)]).reshape(1, LANES)
            od = _mlstm(pd, bias128, ctx_len)
            gate_c = pl.BlockSpec((1, ROW_TILE, C_W), lambda b, i, first=first: (b, i + first, 3))
            gate_d = pl.BlockSpec((1, ROW_TILE, D_W), lambda b, i, first=first: (b, i + first, 3))
            mixer_args = (oc, pc, row(c_gn_g[li]), row(c_gn_b[li]), od, pd, row(d_gn_g[li]), row(d_gn_b[li]))
            mixer_specs = (tok2(C_W), gate_c, vec(C_W), vec(C_W), tok2(D_W), gate_d, vec(D_W), vec(D_W))
            x1, h2, topi, topw, rank, counts = _outproj('odd', mixer_args, mixer_specs, xs, odd_w_out[li].astype(BF16), mods,
                                     ln_g[layer, 0], ln_b[layer, 0], rw, rb, n_ctx_tiles, alpha, first)
        dest, work = _route(topi, rank, counts)
        xs_sorted = _dispatch(h2.reshape(-1, d), dest)
        y_sorted = _experts(xs_sorted, work, exp_w1, exp_b1, exp_w2, exp_b2, layer)
        xs = _combine(y_sorted, dest, topw, x1, mods, ln_g[layer, 1], ln_b[layer, 1], n_ctx_tiles, alpha, first)
    return xs[:, xs.shape[1] - seq:]
```

```python
import functools
import math

import jax
import jax.numpy as jnp
from jax import lax
from jax.experimental import pallas as pl
from jax.experimental.pallas import tpu as pltpu

F32 = jnp.float32
BF16 = jnp.bfloat16

GRID_W = 64
A_HEADS = 8
A_HEAD_DIM = 64
A_W = 512
A_DECAY_RANK = 64
A_ICL_RANK = 64
A_GATE_RANK = 128
A_IN = 1920
A_DECAY_SCALE = math.exp(-0.5)
GN_EPS_RWKV = 64e-5
B_W = 512
B_BLOCKS = 8
B_C = 8.0
B_IN = 1024
C_HEADS = 4
C_HEAD_DIM = 128
C_W = 512
C_IN = 2048
ROPE_BASE = 10000.0
D_HEADS = 4
D_HEAD_DIM = 128
D_W = 512
D_IN = 2064
D_IN_PAD = 2176
N_EXPERTS = 32
TOP_K = 4
D_FF = 1024
SWIGLU_ALPHA = 1.702
SWIGLU_LIMIT = 7.0
LN_EPS = 1e-5
N_MOD = 6
LANES = 128

ROW_TILE = 256
RWKV_CHUNK = 64
RWKV_STEP_CHUNKS = 4
ODD_CHUNK = 256
EXPERT_TILE = 512
VMEM_LIMIT_BYTES = 56 * 1024 * 1024


def _params(*sem):
    return pltpu.CompilerParams(dimension_semantics=sem, vmem_limit_bytes=VMEM_LIMIT_BYTES)


def _dot(a, b):
    return jnp.dot(a.astype(BF16), b.astype(BF16), preferred_element_type=F32)


def _dot_nt(a, b):
    return lax.dot_general(a.astype(BF16), b.astype(BF16), (((1,), (1,)), ((), ())),
                           preferred_element_type=F32)


def _dot_tn(a, b):
    return lax.dot_general(a.astype(BF16), b.astype(BF16), (((0,), (0,)), ((), ())),
                           preferred_element_type=F32)


def _split(x):
    hi = x.astype(BF16)
    lo = (x - hi.astype(F32)).astype(BF16)
    return hi, lo


def _dot_x3(a, b):
    ah, al = _split(a)
    bh, bl = _split(b)
    out = jnp.dot(ah, bh, preferred_element_type=F32)
    out += jnp.dot(ah, bl, preferred_element_type=F32)
    out += jnp.dot(al, bh, preferred_element_type=F32)
    return out


def _dot_nt_x3(a, b):
    ah, al = _split(a)
    bh, bl = _split(b)
    dn = (((1,), (1,)), ((), ()))
    out = lax.dot_general(ah, bh, dn, preferred_element_type=F32)
    out += lax.dot_general(ah, bl, dn, preferred_element_type=F32)
    out += lax.dot_general(al, bh, dn, preferred_element_type=F32)
    return out


def _dot_exact_lhs(a, b):
    a = a.astype(BF16)
    b1 = b.astype(BF16)
    r1 = b - b1.astype(F32)
    b2 = r1.astype(BF16)
    b3 = (r1 - b2.astype(F32)).astype(BF16)
    out = jnp.dot(a, b1, preferred_element_type=F32)
    out += jnp.dot(a, b2, preferred_element_type=F32)
    out += jnp.dot(a, b3, preferred_element_type=F32)
    return out


def _sigmoid(x):
    return 1.0 / (1.0 + jnp.exp(-x))


def _softplus(x):
    return jnp.maximum(x, 0.0) + jnp.log(1.0 + jnp.exp(-jnp.abs(x)))


def _seg_order(i, d, n_ctx, n_tot):
    rev = jnp.where(i < n_ctx, n_ctx - 1 - i, n_ctx + n_tot - 1 - i)
    return jnp.where(d == 0, i, rev)


def _mod_kernel(c_ref, w_ref, b_ref, o_ref):
    c = c_ref[...]
    s = c * _sigmoid(c)
    o_ref[...] = _dot_x3(s, w_ref[...]) + b_ref[...]


def _mod_vectors(c8, w, b, layer):
    _, d, n = w.shape
    tn = n // 4
    return pl.pallas_call(
        _mod_kernel,
        grid=(n // tn,),
        in_specs=[pl.BlockSpec((8, d), lambda j: (0, 0)),
                  pl.BlockSpec((None, d, tn), lambda j: (layer, 0, j)),
                  pl.BlockSpec((1, tn), lambda j: (0, j))],
        out_specs=pl.BlockSpec((8, tn), lambda j: (0, j)),
        out_shape=jax.ShapeDtypeStruct((8, n), F32),
        compiler_params=_params("parallel"),
        name="mod_vectors",
    )(c8, w, b.reshape(1, n))


def _inproj_kernel(x_ref, sh_ref, sc_ref, w_ref, oa_ref, ob_ref, *, n_a):
    h = x_ref[0] * (1.0 + sc_ref[0, 0]) + sh_ref[0, 0]
    p = jnp.dot(h.astype(BF16), w_ref[...], preferred_element_type=F32)
    oa_ref[0] = p[:, :n_a]
    ob_ref[0] = p[:, n_a:]


def _inproj(x, shift, scale, w_bf16, n_a, n_ctx_tiles):
    bz, t, d = x.shape
    n = w_bf16.shape[1]
    n_b = n - n_a
    seg = lambda i: (i >= n_ctx_tiles).astype(jnp.int32)
    return pl.pallas_call(
        functools.partial(_inproj_kernel, n_a=n_a),
        grid=(bz, t // ROW_TILE),
        in_specs=[pl.BlockSpec((1, ROW_TILE, d), lambda b, i: (b, i, 0)),
                  pl.BlockSpec((1, 1, 1, d), lambda b, i: (b, seg(i), 0, 0)),
                  pl.BlockSpec((1, 1, 1, d), lambda b, i: (b, seg(i), 0, 0)),
                  pl.BlockSpec((d, n), lambda b, i: (0, 0))],
        out_specs=[pl.BlockSpec((1, ROW_TILE, n_a), lambda b, i: (b, i, 0)),
                   pl.BlockSpec((1, ROW_TILE, n_b), lambda b, i: (b, i, 0))],
        out_shape=[jax.ShapeDtypeStruct((bz, t, n_a), F32),
                   jax.ShapeDtypeStruct((bz, t, n_b), F32)],
        compiler_params=_params("parallel", "parallel"),
        name="inproj",
    )(x, shift, scale, w_bf16)


def _rwkv_prep_kernel(p_ref, up_ref, dn_ref, mu_ref, w0_ref, wb_ref, a0_ref, ab_ref, gb_ref,
                      kkw_ref, kaw_ref, rkw_ref, ones_ref,
                      r_o, v_o, kk_o, g_o, bonus_o, lw_o, kd_o, a_o, *, n_ctx_tiles, n_tiles):
    i = pl.program_id(1)
    p = p_ref[0]
    tr, w = p.shape
    row = lax.broadcasted_iota(jnp.int32, (tr, w), 0)
    lane = lax.broadcasted_iota(jnp.int32, (tr, w), 1)
    prev = pltpu.roll(p, 1, 0)
    nxt = pltpu.roll(p, tr - 1, 0)
    sh_ctx = jnp.where(lane % 2 == 0, jnp.where(row == 0, 0.0, prev), jnp.where(row == tr - 1, 0.0, nxt))
    col = row % GRID_W
    left = jnp.where(col == 0, 0.0, prev)
    right = jnp.where(col == GRID_W - 1, 0.0, nxt)
    up_halo = jnp.where(i > n_ctx_tiles, up_ref[0], 0.0)
    dn_halo = jnp.where(i < n_tiles - 1, dn_ref[0], 0.0)
    up = jnp.concatenate([up_halo, p[:tr - GRID_W]], axis=0)
    down = jnp.concatenate([p[GRID_W:], dn_halo], axis=0)
    slot = lane % 4
    sh_lat = jnp.where(slot == 0, left, jnp.where(slot == 1, right, jnp.where(slot == 2, up, down)))
    sh = jnp.where(i < n_ctx_tiles, sh_ctx, sh_lat)
    z = p + (sh - p) * mu_ref[...]

    r = z[:, 0:A_W]
    k = z[:, A_W:2 * A_W]
    v = z[:, 2 * A_W:3 * A_W]
    o = 3 * A_W
    wd = z[:, o:o + 2 * A_DECAY_RANK]
    o += 2 * A_DECAY_RANK
    ad = z[:, o:o + 2 * A_ICL_RANK]
    o += 2 * A_ICL_RANK
    zg = z[:, o:o + A_GATE_RANK]

    ones_bd = ones_ref[...]

    def head_sum(x):
        hi, lo = _split(x)
        return (jnp.dot(hi, ones_bd, preferred_element_type=F32)
                + jnp.dot(lo, ones_bd, preferred_element_type=F32))

    g_o[0] = _dot(_sigmoid(zg), gb_ref[...]).astype(BF16)
    kk = k * kkw_ref[...]
    kk = kk / jnp.maximum(jnp.sqrt(head_sum(kk * kk)), 1e-12)
    r_o[0] = r.astype(BF16)
    v_o[0] = v.astype(BF16)
    kk_o[0] = kk.astype(BF16)
    rr = r * rkw_ref[...]
    bsum = jnp.zeros_like(r)
    for d in range(2):
        wdd = jnp.tanh(wd[:, d * A_DECAY_RANK:(d + 1) * A_DECAY_RANK])
        lw = -A_DECAY_SCALE * _sigmoid(w0_ref[d] + _dot(wdd, wb_ref[d]))
        a = _sigmoid(a0_ref[d] + _dot(ad[:, d * A_ICL_RANK:(d + 1) * A_ICL_RANK], ab_ref[d]))
        kd = k * (1.0 + (a - 1.0) * kaw_ref[...])
        lw_o[d, 0] = lw
        kd_o[d, 0] = kd.astype(BF16)
        a_o[d, 0] = a.astype(BF16)
        bsum = bsum + head_sum(rr * kd)
    bonus_o[0] = (bsum * v).astype(BF16)


def _rwkv_prep(pa, mu, w0, wb, a0, ab, gb, kkw, kaw, rkw, n_ctx_tiles):
    bz, t, _ = pa.shape
    n_tiles = t // ROW_TILE
    hb = ROW_TILE // GRID_W
    n_hblk = t // GRID_W
    ones_bd = jnp.kron(jnp.eye(A_HEADS, dtype=F32), jnp.ones((A_HEAD_DIM, A_HEAD_DIM), F32)).astype(BF16)
    row = lambda a: a.reshape(1, -1)
    tok = pl.BlockSpec((1, ROW_TILE, A_W), lambda b, i: (b, i, 0))
    tok2 = pl.BlockSpec((2, 1, ROW_TILE, A_W), lambda b, i: (0, b, i, 0))
    full = lambda a: pl.BlockSpec(a.shape, lambda b, i: (0,) * a.ndim)
    args = (row(mu), w0.reshape(2, 1, A_W), wb.astype(BF16), a0.reshape(2, 1, A_W), ab.astype(BF16),
            gb.astype(BF16), row(kkw), row(kaw), row(rkw), ones_bd)
    s1 = jax.ShapeDtypeStruct((bz, t, A_W), BF16)
    s2 = jax.ShapeDtypeStruct((2, bz, t, A_W), BF16)
    s2f = jax.ShapeDtypeStruct((2, bz, t, A_W), F32)
    return pl.pallas_call(
        functools.partial(_rwkv_prep_kernel, n_ctx_tiles=n_ctx_tiles, n_tiles=n_tiles),
        grid=(bz, n_tiles),
        in_specs=[pl.BlockSpec((1, ROW_TILE, A_IN), lambda b, i: (b, i, 0)),
                  pl.BlockSpec((1, GRID_W, A_IN), lambda b, i: (b, jnp.maximum(i * hb - 1, 0), 0)),
                  pl.BlockSpec((1, GRID_W, A_IN), lambda b, i: (b, jnp.minimum((i + 1) * hb, n_hblk - 1), 0))]
                 + [full(a) for a in args],
        out_specs=[tok, tok, tok, tok, tok, tok2, tok2, tok2],
        out_shape=[s1, s1, s1, s1, s1, s2f, s2, s2],
        compiler_params=_params("parallel", "parallel"),
        name="rwkv_prep",
    )(pa, pa, pa, *args)


_BMM_NN = (((2,), (1,)), ((0,), (0,)))
_BMM_NT = (((2,), (2,)), ((0,), (0,)))
_BMM_TN = (((1,), (1,)), ((0,), (0,)))
RWKV_CHAINS = 2 * A_HEADS


def _bmm(a, b, dims):
    return lax.dot_general(a.astype(BF16), b.astype(BF16), dims, preferred_element_type=F32)


def _bmm_x3(a, b, dims):
    ah, al = _split(a)
    bh, bl = _split(b)
    rows = a.shape[1]
    p = lax.dot_general(jnp.concatenate([ah, al], axis=1), bh, dims, preferred_element_type=F32)
    return p[:, :rows] + p[:, rows:] + lax.dot_general(ah, bl, dims, preferred_element_type=F32)


def _split_heads(x):
    return jnp.stack([x[:, h * A_HEAD_DIM:(h + 1) * A_HEAD_DIM] for h in range(A_HEADS)], axis=0)


def _rwkv_scan_kernel(rf, lwf, kdf, vf, kkf, af, rb, lwb, kdb, vb, kkb, ab, of, ob, s_ref):
    i = pl.program_id(1)

    @pl.when(i == 0)
    def _():
        s_ref[...] = jnp.zeros_like(s_ref)

    n = RWKV_CHUNK
    n_sub = rf.shape[1] // n
    ti = lax.broadcasted_iota(jnp.int32, (n, n), 0)
    tj = lax.broadcasted_iota(jnp.int32, (n, n), 1)
    for c_idx in range(n_sub):
        rows_f = pl.ds(c_idx * n, n)
        rows_b = pl.ds((n_sub - 1 - c_idx) * n, n)
        _rwkv_chunk((rf, lwf, kdf, vf, kkf, af), (rb, lwb, kdb, vb, kkb, ab), of, ob, s_ref, rows_f, rows_b, ti, tj)


def _rwkv_chunk(fwd_refs, bwd_refs, of, ob, s_ref, rows_f, rows_b, ti, tj):
    n = RWKV_CHUNK

    def decayed(refs, rows, fwd):
        r_ref, lw_ref, kd_ref, v_ref, kk_ref, a_ref = refs
        lw = lw_ref[0, 0, rows, :]
        kd = kd_ref[0, 0, rows, :].astype(F32)
        kk = kk_ref[0, rows, :].astype(F32)
        b = kk * a_ref[0, 0, rows, :].astype(F32)
        incl = (tj <= ti) if fwd else (tj >= ti)
        c = _dot_exact_lhs(incl.astype(F32), lw)
        ctot = jnp.sum(lw, axis=0, keepdims=True)
        e_nc = jnp.exp(-c)
        e_tot = jnp.exp(ctot - c)
        return dict(rt=r_ref[0, rows, :].astype(F32) * jnp.exp(c), at=kk * jnp.exp(c - lw), kt=kd * e_nc,
                    bt=b * e_nc, kh=kd * e_tot, bh=b * e_tot, v=v_ref[0, rows, :].astype(F32), e=jnp.exp(ctot))

    pf = decayed(fwd_refs, rows_f, True)
    pb = decayed(bwd_refs, rows_b, False)

    def chains(name):
        return jnp.concatenate([_split_heads(pf[name]), _split_heads(pb[name])], axis=0)

    rt, at, kt, bt, kh, bh, v = (chains(x) for x in ("rt", "at", "kt", "bt", "kh", "bh", "v"))
    ar = jnp.concatenate([at, rt], axis=1)
    kb = jnp.concatenate([kt, bt], axis=1)
    m = _bmm(ar, kb, _BMM_NT)
    chain = lax.broadcasted_iota(jnp.int32, (RWKV_CHAINS, n, n), 0)
    lag = jnp.where(chain < A_HEADS, ti - tj, tj - ti)
    before = lag > 0
    before_eq = lag >= 0
    a_ak = jnp.where(before, m[:, :n, :n], 0.0)
    a_ab = jnp.where(before, m[:, :n, n:], 0.0)
    a_rk = jnp.where(before_eq, m[:, n:, :n], 0.0)
    a_rb = jnp.where(before_eq, m[:, n:, n:], 0.0)
    eye = (ti == tj).astype(F32)
    pw = -a_ab
    tinv = eye + pw
    for _ in range(int(math.log2(n)) - 1):
        pw = _bmm_x3(pw, pw, _BMM_NN)
        tinv = tinv + _bmm_x3(tinv, pw, _BMM_NN)
    sk = s_ref[...]
    a_s = _bmm(ar, sk, _BMM_NN)
    u = _bmm(tinv, a_s[:, :n] + _bmm(a_ak, v, _BMM_NN), _BMM_NN)
    vu = jnp.concatenate([v, -u], axis=1)
    out = a_s[:, n:] + _bmm(jnp.concatenate([a_rk, a_rb], axis=2), vu, _BMM_NN)
    e_rows = jnp.concatenate([_split_heads(pf["e"]), _split_heads(pb["e"])], axis=0)
    e_col = jnp.sum(eye * e_rows, axis=2, keepdims=True)
    s_ref[...] = sk * e_col + _bmm(jnp.concatenate([kh, bh], axis=1), vu, _BMM_TN)
    for h in range(A_HEADS):
        sl = slice(h * A_HEAD_DIM, (h + 1) * A_HEAD_DIM)
        of[0, rows_f, sl] = out[h]
        ob[0, rows_b, sl] = out[A_HEADS + h]


def _rwkv_scan(r, lw, kd, v, kk, a, ctx_len):
    bz, t, _ = r.shape
    blk = RWKV_CHUNK * RWKV_STEP_CHUNKS
    n_ctx = ctx_len // blk
    n_blocks = t // blk
    rev = lambda i: _seg_order(i, 1, n_ctx, n_blocks)
    tok_f = pl.BlockSpec((1, blk, A_W), lambda b, i: (b, i, 0))
    tok_b = pl.BlockSpec((1, blk, A_W), lambda b, i: (b, rev(i), 0))
    dir_f = pl.BlockSpec((1, 1, blk, A_W), lambda b, i: (0, b, i, 0))
    dir_b = pl.BlockSpec((1, 1, blk, A_W), lambda b, i: (1, b, rev(i), 0))
    shp = jax.ShapeDtypeStruct((bz, t, A_W), F32)
    return pl.pallas_call(
        _rwkv_scan_kernel,
        grid=(bz, n_blocks),
        in_specs=[tok_f, dir_f, dir_f, tok_f, tok_f, dir_f, tok_b, dir_b, dir_b, tok_b, tok_b, dir_b],
        out_specs=[tok_f, tok_b],
        out_shape=[shp, shp],
        scratch_shapes=[pltpu.VMEM((RWKV_CHAINS, A_HEAD_DIM, A_HEAD_DIM), F32)],
        compiler_params=_params("parallel", "arbitrary"),
        name="rwkv_scan",
    )(r, lw, kd, v, kk, a, r, lw, kd, v, kk, a)


def _rglru_kernel(p_ref, pv_ref, nx_ref, cw_ref, cb_ref, gw_ref, gbias_ref, lam_ref, o_ref,
                  a_scr, b_scr, h_scr, carry_ref, *, n_ctx_tiles, n_tiles):
    d = pl.program_id(1)
    i = pl.program_id(2)
    blk = _seg_order(i, d, n_ctx_tiles, n_tiles)

    @pl.when(i == 0)
    def _():
        carry_ref[...] = jnp.zeros_like(carry_ref)

    p = p_ref[0]
    tr = p.shape[0]
    x = p[:, :B_W]
    gate_in = p[:, B_W:]
    has_prev = jnp.logical_and(blk != 0, blk != n_ctx_tiles)
    has_next = jnp.logical_and(blk != n_ctx_tiles - 1, blk != n_tiles - 1)
    pv = jnp.where(has_prev, pv_ref[0][:, :B_W], 0.0)
    nx = jnp.where(has_next, nx_ref[0][:, :B_W], 0.0)
    row = lax.broadcasted_iota(jnp.int32, (tr, B_W), 0)
    pv_m1 = pv[7:8]
    pv_m2 = pv[6:7]
    nx_p1 = nx[0:1]
    x_m1 = jnp.where(row == 0, pv_m1, pltpu.roll(x, 1, 0))
    x_m2 = jnp.where(row == 0, pv_m2, jnp.where(row == 1, pv_m1, pltpu.roll(x, 2, 0)))
    x_p1 = jnp.where(row == tr - 1, nx_p1, pltpu.roll(x, tr - 1, 0))
    cw = cw_ref[...]
    u = cw[0:1] * x_m2 + cw[1:2] * x_m1 + cw[2:3] * x + cw[3:4] * x_p1 + cb_ref[...]
    gate = 0.5 * gate_in * (1.0 + jnp.tanh(math.sqrt(2.0 / math.pi) * (gate_in + 0.044715 * gate_in ** 3)))
    pre = _dot(u, gw_ref[0]) + gbias_ref[0]
    rec = _sigmoid(pre[:, :B_W])
    inp = _sigmoid(pre[:, B_W:])
    log_a = -B_C * rec * _softplus(-lam_ref[0])
    a_scr[...] = jnp.exp(log_a)
    b_scr[...] = jnp.sqrt(1.0 - jnp.exp(2.0 * log_a)) * inp * u

    n_grp = tr // 8
    g8 = lax.broadcasted_iota(jnp.int32, (8, B_W), 0)

    def group_scan(a, b, fwd):
        for s in (1, 2, 4):
            if fwd:
                ident = g8 < s
                a_p = pltpu.roll(a, s, 0)
                b_p = pltpu.roll(b, s, 0)
            else:
                ident = g8 >= 8 - s
                a_p = pltpu.roll(a, 8 - s, 0)
                b_p = pltpu.roll(b, 8 - s, 0)
            a_p = jnp.where(ident, 1.0, a_p)
            b_p = jnp.where(ident, 0.0, b_p)
            b = a * b_p + b
            a = a * a_p
        return a, b

    def run(fwd):
        def body(j, carry):
            g = j if fwd else n_grp - 1 - j
            off = pl.multiple_of(g * 8, 8)
            a, b = group_scan(a_scr[pl.ds(off, 8), :], b_scr[pl.ds(off, 8), :], fwd)
            h = b + a * carry
            h_scr[pl.ds(off, 8), :] = h
            return h[7:8] if fwd else h[0:1]
        carry_ref[...] = lax.fori_loop(0, n_grp, body, carry_ref[...], unroll=8)

    @pl.when(d == 0)
    def _():
        run(True)

    @pl.when(d == 1)
    def _():
        run(False)

    o_ref[0, 0] = (h_scr[...] * gate).astype(o_ref.dtype)


def _rglru(pb, conv_w, conv_b, gate_w, gate_b, lam, n_ctx_tiles):
    bz, t, _ = pb.shape
    n_tiles = t // ROW_TILE
    r8 = ROW_TILE // 8
    n8 = t // 8
    eye = jnp.eye(B_BLOCKS, dtype=F32)
    gw = jnp.einsum('dsgio,gh->dgisho', gate_w, eye).reshape(2, B_W, 2 * B_W).astype(BF16)
    gbias = gate_b.reshape(2, 1, 2 * B_W)
    blk = lambda i, d: _seg_order(i, d, n_ctx_tiles, n_tiles)
    return pl.pallas_call(
        functools.partial(_rglru_kernel, n_ctx_tiles=n_ctx_tiles, n_tiles=n_tiles),
        grid=(bz, 2, n_tiles),
        in_specs=[pl.BlockSpec((1, ROW_TILE, B_IN), lambda b, d, i: (b, blk(i, d), 0)),
                  pl.BlockSpec((1, 8, B_IN), lambda b, d, i: (b, jnp.maximum(blk(i, d) * r8 - 1, 0), 0)),
                  pl.BlockSpec((1, 8, B_IN), lambda b, d, i: (b, jnp.minimum((blk(i, d) + 1) * r8, n8 - 1), 0)),
                  pl.BlockSpec((4, B_W), lambda b, d, i: (0, 0)),
                  pl.BlockSpec((1, B_W), lambda b, d, i: (0, 0)),
                  pl.BlockSpec((1, B_W, 2 * B_W), lambda b, d, i: (d, 0, 0)),
                  pl.BlockSpec((1, 1, 2 * B_W), lambda b, d, i: (d, 0, 0)),
                  pl.BlockSpec((1, 1, B_W), lambda b, d, i: (d, 0, 0))],
        out_specs=pl.BlockSpec((1, 1, ROW_TILE, B_W), lambda b, d, i: (d, b, blk(i, d), 0)),
        out_shape=jax.ShapeDtypeStruct((2, bz, t, B_W), BF16),
        scratch_shapes=[pltpu.VMEM((ROW_TILE, B_W), F32), pltpu.VMEM((ROW_TILE, B_W), F32),
                        pltpu.VMEM((ROW_TILE, B_W), F32), pltpu.VMEM((1, B_W), F32)],
        compiler_params=_params("parallel", "parallel", "arbitrary"),
        name="rglru",
    )(pb, pb, pb, conv_w, conv_b.reshape(1, B_W), gw, gbias, lam.reshape(2, 1, B_W))


def _retention_kernel(lg_ref, p_ref, cos_ref, sin_ref, o_ref, st_ref):
    d = pl.program_id(1)
    i = pl.program_id(2)

    @pl.when(i == 0)
    def _():
        st_ref[...] = jnp.zeros_like(st_ref)

    n = p_ref.shape[1]
    cos2 = cos_ref[...]
    sin2 = sin_ref[...]
    ti = lax.broadcasted_iota(jnp.int32, (n, n), 0)
    tj = lax.broadcasted_iota(jnp.int32, (n, n), 1)
    fwd = d == 0
    diff = jnp.where(fwd, ti - tj, tj - ti).astype(F32)
    t1 = lax.broadcasted_iota(jnp.int32, (n, 1), 0)
    pos = jnp.where(fwd, t1, n - 1 - t1).astype(F32)
    for h in range(C_HEADS):
        lg = lg_ref[d, h] * jnp.ones((1, 1), F32)
        sl = slice(h * C_HEAD_DIM, (h + 1) * C_HEAD_DIM)
        q = p_ref[0, :, sl]
        k = p_ref[0, :, C_W + h * C_HEAD_DIM:C_W + (h + 1) * C_HEAD_DIM]
        v = p_ref[0, :, 2 * C_W + h * C_HEAD_DIM:2 * C_W + (h + 1) * C_HEAD_DIM]
        q = q * cos2 + pltpu.roll(q, C_HEAD_DIM // 2, 1) * sin2
        k = (k * cos2 + pltpu.roll(k, C_HEAD_DIM // 2, 1) * sin2) * (C_HEAD_DIM ** -0.5)
        decay = jnp.where(diff >= 0, jnp.exp(lg * jnp.maximum(diff, 0.0)), 0.0)
        scores = _dot_nt(q, k) * decay
        st = st_ref[h]
        out = _dot(scores, v) + _dot(q, st) * jnp.exp(lg * (pos + 1.0))
        zeta = jnp.exp(lg * (n - 1.0 - pos))
        st_ref[h] = st * jnp.exp(lg * n) + _dot_tn(k * zeta, v)
        o_ref[0, 0, :, sl] = out.astype(o_ref.dtype)


def _retention(pc, log_gamma, cos2, sin2, ctx_len):
    bz, t, _ = pc.shape
    n_ctx = ctx_len // ODD_CHUNK
    n_chunks = t // ODD_CHUNK
    blk = lambda i, d: _seg_order(i, d, n_ctx, n_chunks)
    return pl.pallas_call(
        _retention_kernel,
        grid_spec=pltpu.PrefetchScalarGridSpec(
            num_scalar_prefetch=1,
            grid=(bz, 2, n_chunks),
            in_specs=[pl.BlockSpec((1, ODD_CHUNK, 3 * C_W), lambda b, d, i, lg: (b, blk(i, d), 0)),
                      pl.BlockSpec((ODD_CHUNK, C_HEAD_DIM), lambda b, d, i, lg: (blk(i, d), 0)),
                      pl.BlockSpec((ODD_CHUNK, C_HEAD_DIM), lambda b, d, i, lg: (blk(i, d), 0))],
            out_specs=pl.BlockSpec((1, 1, ODD_CHUNK, C_W), lambda b, d, i, lg: (d, b, blk(i, d), 0)),
            scratch_shapes=[pltpu.VMEM((C_HEADS, C_HEAD_DIM, C_HEAD_DIM), F32)]),
        out_shape=jax.ShapeDtypeStruct((2, bz, t, C_W), BF16),
        compiler_params=_params("parallel", "parallel", "arbitrary"),
        name="retention",
    )(log_gamma, pc, cos2, sin2)


def _mlstm_kernel(p_ref, bias_ref, o_ref, c_ref, n_ref, m_ref):
    d = pl.program_id(1)
    i = pl.program_id(2)

    @pl.when(i == 0)
    def _():
        c_ref[...] = jnp.zeros_like(c_ref)
        n_ref[...] = jnp.zeros_like(n_ref)
        m_ref[...] = jnp.full_like(m_ref, -jnp.inf)

    n = p_ref.shape[1]
    ti = lax.broadcasted_iota(jnp.int32, (n, n), 0)
    tj = lax.broadcasted_iota(jnp.int32, (n, n), 1)
    fwd = d == 0
    before_eq = jnp.where(fwd, ti - tj, tj - ti) >= 0
    gts = p_ref[0, :, 4 * D_W:] + bias_ref[...]
    gts = jnp.where(fwd, gts, pltpu.roll(gts, LANES - D_HEADS, 1))
    lf = jnp.minimum(gts, 0.0) - jnp.log(1.0 + jnp.exp(-jnp.abs(gts)))
    bcum = _dot_exact_lhs(before_eq.astype(F32), lf)
    btot = jnp.sum(lf, axis=0, keepdims=True)
    gts_t = gts.T
    bcum_t = bcum.T
    for h in range(D_HEADS):
        sl = slice(h * D_HEAD_DIM, (h + 1) * D_HEAD_DIM)
        q = p_ref[0, :, sl]
        k = p_ref[0, :, D_W + h * D_HEAD_DIM:D_W + (h + 1) * D_HEAD_DIM] * (D_HEAD_DIM ** -0.5)
        v = p_ref[0, :, 2 * D_W + h * D_HEAD_DIM:2 * D_W + (h + 1) * D_HEAD_DIM]
        ig_col = gts[:, h:h + 1]
        ig_row = gts_t[h:h + 1, :]
        b_col = bcum[:, 2 * D_HEADS + h:2 * D_HEADS + h + 1]
        b_row = bcum_t[2 * D_HEADS + h:2 * D_HEADS + h + 1, :]
        b_end = btot[:, 2 * D_HEADS + h:2 * D_HEADS + h + 1]
        m_prev = m_ref[h][0:1, 0:1]
        d_log = jnp.where(before_eq, b_col - b_row + ig_row, -jnp.inf)
        g_inter = b_col + m_prev
        m_t = jnp.maximum(g_inter, jnp.max(d_log, axis=1, keepdims=True))
        s_intra = _dot_nt(q, k) * jnp.exp(d_log - m_t)
        s_inter = jnp.exp(g_inter - m_t)
        num = _dot(s_intra, v) + _dot(q, c_ref[h]) * s_inter
        den = jnp.sum(s_intra, axis=1, keepdims=True) + jnp.sum(q * n_ref[h], axis=1, keepdims=True) * s_inter
        den = jnp.maximum(jnp.abs(den), jnp.exp(-m_t))
        o_ref[0, 0, :, sl] = (num / den).astype(o_ref.dtype)
        w_end = b_end - b_col + ig_col
        m_i = jnp.max(w_end, axis=0, keepdims=True)
        e_end = jnp.exp(w_end - m_i)
        ke = k * e_end
        m_new = jnp.maximum(b_end + m_prev, m_i)
        s_old = jnp.exp(b_end + m_prev - m_new)
        s_new = jnp.exp(m_i - m_new)
        c_ref[h] = c_ref[h] * s_old + _dot_tn(ke, v) * s_new
        n_ref[h] = n_ref[h] * s_old + jnp.sum(ke, axis=0, keepdims=True) * s_new
        m_ref[h] = jnp.broadcast_to(m_new, m_ref.shape[1:])


def _mlstm(pd, bias128, ctx_len):
    bz, t, w = pd.shape
    n_ctx = ctx_len // ODD_CHUNK
    n_chunks = t // ODD_CHUNK
    blk = lambda i, d: _seg_order(i, d, n_ctx, n_chunks)
    return pl.pallas_call(
        _mlstm_kernel,
        grid=(bz, 2, n_chunks),
        in_specs=[pl.BlockSpec((1, ODD_CHUNK, w), lambda b, d, i: (b, blk(i, d), 0)),
                  pl.BlockSpec((1, LANES), lambda b, d, i: (0, 0))],
        out_specs=pl.BlockSpec((1, 1, ODD_CHUNK, D_W), lambda b, d, i: (d, b, blk(i, d), 0)),
        out_shape=jax.ShapeDtypeStruct((2, bz, t, D_W), BF16),
        scratch_shapes=[pltpu.VMEM((D_HEADS, D_HEAD_DIM, D_HEAD_DIM), F32),
                        pltpu.VMEM((D_HEADS, 1, D_HEAD_DIM), F32),
                        pltpu.VMEM((D_HEADS, 8, LANES), F32)],
        compiler_params=_params("parallel", "parallel", "arbitrary"),
        name="mlstm",
    )(pd, bias128)


def _group_norm_lanes(x, ones_bd, group, eps):
    def group_mean(v):
        hi, lo = _split(v)
        return (jnp.dot(hi, ones_bd, preferred_element_type=F32)
                + jnp.dot(lo, ones_bd, preferred_element_type=F32)) * (1.0 / group)

    xc = x - group_mean(x)
    return xc * lax.rsqrt(group_mean(xc * xc) + eps)


def _group_norm_aligned(x, n_groups, eps):
    gw = x.shape[1] // n_groups
    outs = []
    for g in range(n_groups):
        xg = x[:, g * gw:(g + 1) * gw]
        xc = xg - jnp.mean(xg, axis=1, keepdims=True)
        outs.append(xc * lax.rsqrt(jnp.mean(xc * xc, axis=1, keepdims=True) + eps))
    return jnp.concatenate(outs, axis=1)


def _group_ones(width, group):
    return jnp.kron(jnp.eye(width // group, dtype=F32), jnp.ones((group, group), F32)).astype(BF16)


def _post_mixer(ycat, x_ref, wout_ref, mods_ref, lng_ref, lnb_ref, rw_ref, rb_ref, x1_o, h2_o, topi_o, topw_o, rank_o, cnt_o, cnt_scr, alpha):
    y = jnp.dot(ycat.astype(BF16), wout_ref[...], preferred_element_type=F32)
    mods = mods_ref[0, 0]
    x1 = alpha * x_ref[0] + mods[2:3] * y
    mu = jnp.mean(x1, axis=1, keepdims=True)
    xc = x1 - mu
    var = jnp.mean(xc * xc, axis=1, keepdims=True)
    x1 = xc * lax.rsqrt(var + LN_EPS) * lng_ref[...] + lnb_ref[...]
    x1_o[0] = x1
    h2 = x1 * (1.0 + mods[4:5]) + mods[3:4]
    h2_o[0] = h2
    logits = _dot_x3(h2, rw_ref[...]) + rb_ref[...]
    lane = lax.broadcasted_iota(jnp.int32, logits.shape, 1)
    work = jnp.where(lane < N_EXPERTS, logits, -jnp.inf)
    top_i = jnp.zeros(logits.shape, jnp.int32)
    top_v = jnp.full(logits.shape, -jnp.inf, F32)
    top = None
    picks = []
    for kth in range(TOP_K):
        mk = jnp.max(work, axis=1, keepdims=True)
        idx = jnp.min(jnp.where(work == mk, lane, LANES), axis=1, keepdims=True)
        pick = lane == idx
        picks.append(pick)
        work = jnp.where(pick, -jnp.inf, work)
        top_i = jnp.where(lane == kth, idx, top_i)
        top_v = jnp.where(lane == kth, mk, top_v)
        if kth == 0:
            top = mk
    e = jnp.exp(top_v - top)
    topi_o[0] = top_i
    topw_o[0] = e / jnp.sum(e, axis=1, keepdims=True)

    @pl.when(jnp.logical_and(pl.program_id(0) == 0, pl.program_id(1) == 0))
    def _():
        cnt_scr[...] = jnp.zeros_like(cnt_scr)

    rows = logits.shape[0]
    chosen = jnp.zeros(logits.shape, F32)
    for pick in picks:
        chosen = jnp.where(pick, 1.0, chosen)
    ti = lax.broadcasted_iota(jnp.int32, (rows, rows), 0)
    tj = lax.broadcasted_iota(jnp.int32, (rows, rows), 1)
    earlier = jnp.dot((tj < ti).astype(BF16), chosen.astype(BF16), preferred_element_type=F32) + cnt_scr[...]
    rank = jnp.zeros(logits.shape, F32)
    for kth, pick in enumerate(picks):
        rank = jnp.where(lane == kth, jnp.sum(jnp.where(pick, earlier, 0.0), axis=1, keepdims=True), rank)
    rank_o[0] = rank.astype(jnp.int32)
    cnt_scr[...] = cnt_scr[...] + jnp.sum(chosen, axis=0, keepdims=True)
    cnt_o[...] = cnt_scr[...].astype(jnp.int32)


def _outproj_even_kernel(wkvf_ref, wkvb_ref, bonus_ref, g_ref, gng_ref, gnb_ref, hb_ref, ones_ref,
                         x_ref, wout_ref, mods_ref, lng_ref, lnb_ref, rw_ref, rb_ref,
                         x1_o, h2_o, topi_o, topw_o, rank_o, cnt_o, cnt_scr, *, alpha):
    wkv = wkvf_ref[0] + wkvb_ref[0]
    normed = _group_norm_lanes(wkv, ones_ref[...], A_HEAD_DIM, GN_EPS_RWKV)
    ya = (normed * gng_ref[...] + gnb_ref[...] + bonus_ref[0].astype(F32)) * g_ref[0].astype(F32)
    yb = hb_ref[0, 0].astype(F32) + hb_ref[1, 0].astype(F32)
    ycat = jnp.concatenate([ya, yb], axis=1)
    _post_mixer(ycat, x_ref, wout_ref, mods_ref, lng_ref, lnb_ref, rw_ref, rb_ref, x1_o, h2_o, topi_o, topw_o, rank_o, cnt_o, cnt_scr, alpha)


def _outproj_odd_kernel(oc_ref, pc_ref, cg_ref, cb_ref, od_ref, pd_ref, dg_ref, db_ref,
                        x_ref, wout_ref, mods_ref, lng_ref, lnb_ref, rw_ref, rb_ref,
                        x1_o, h2_o, topi_o, topw_o, rank_o, cnt_o, cnt_scr, *, alpha):
    oc = oc_ref[0, 0].astype(F32) + oc_ref[1, 0].astype(F32)
    gc = pc_ref[0]
    ya = (_group_norm_aligned(oc, C_HEADS, LN_EPS) * cg_ref[...] + cb_ref[...]) * (gc * _sigmoid(gc))
    od = od_ref[0, 0].astype(F32) + od_ref[1, 0].astype(F32)
    yb = (_group_norm_aligned(od, D_HEADS, LN_EPS) * dg_ref[...] + db_ref[...]) * _sigmoid(pd_ref[0])
    ycat = jnp.concatenate([ya, yb], axis=1)
    _post_mixer(ycat, x_ref, wout_ref, mods_ref, lng_ref, lnb_ref, rw_ref, rb_ref, x1_o, h2_o, topi_o, topw_o, rank_o, cnt_o, cnt_scr, alpha)


def _outproj(kind, mixer_args, mixer_specs, x, wout, mods, ln_g, ln_b, rw, rb, n_ctx_tiles, alpha, first_tile):
    bz, t, d = x.shape
    t_out = t - first_tile * ROW_TILE
    seg = lambda i: (i + first_tile >= n_ctx_tiles).astype(jnp.int32)
    row = lambda a: a.reshape(1, -1)
    vec = lambda n: pl.BlockSpec((1, n), lambda b, i: (0, 0))
    kern = _outproj_even_kernel if kind == 'even' else _outproj_odd_kernel
    tok = lambda n: pl.BlockSpec((1, ROW_TILE, n), lambda b, i: (b, i, 0))
    return pl.pallas_call(
        functools.partial(kern, alpha=alpha),
        grid=(bz, t_out // ROW_TILE),
        in_specs=list(mixer_specs) + [
            pl.BlockSpec((1, ROW_TILE, d), lambda b, i: (b, i + first_tile, 0)),
            pl.BlockSpec((d, d), lambda b, i: (0, 0)),
            pl.BlockSpec((1, 1, 8, d), lambda b, i: (b, seg(i), 0, 0)),
            vec(d), vec(d),
            pl.BlockSpec((d, LANES), lambda b, i: (0, 0)),
            vec(LANES)],
        out_specs=[tok(d), tok(d), tok(LANES), tok(LANES), tok(LANES), vec(LANES)],
        out_shape=[jax.ShapeDtypeStruct((bz, t_out, d), F32),
                   jax.ShapeDtypeStruct((bz, t_out, d), F32),
                   jax.ShapeDtypeStruct((bz, t_out, LANES), jnp.int32),
                   jax.ShapeDtypeStruct((bz, t_out, LANES), F32),
                   jax.ShapeDtypeStruct((bz, t_out, LANES), jnp.int32),
                   jax.ShapeDtypeStruct((1, LANES), jnp.int32)],
        scratch_shapes=[pltpu.VMEM((1, LANES), F32)],
        compiler_params=_params("arbitrary", "arbitrary"),
        name="outproj_" + kind,
    )(*mixer_args, x, wout, mods, row(ln_g), row(ln_b), rw, rb)


def _route(top_i, rank, counts):
    experts = jnp.arange(N_EXPERTS, dtype=jnp.int32)
    e_flat = top_i[..., :TOP_K].reshape(-1)
    n_asg = e_flat.shape[0]
    cnt = counts[0, :N_EXPERTS]
    start = jnp.cumsum(cnt) - cnt
    dest = rank[..., :TOP_K].reshape(-1) + jnp.sum(jnp.where(e_flat[:, None] == experts, start, 0), axis=1)
    first_t = start // EXPERT_TILE
    last_t = (start + cnt - 1) // EXPERT_TILE
    n_t = jnp.where(cnt > 0, last_t - first_t + 1, 0)
    w_end = jnp.cumsum(n_t)
    n_work = w_end[-1]
    w = jnp.minimum(jnp.arange(n_asg // EXPERT_TILE + N_EXPERTS - 1, dtype=jnp.int32), n_work - 1)
    we = jnp.sum((w_end[None, :] <= w[:, None]).astype(jnp.int32), axis=1)
    pick = we[:, None] == experts
    sel = lambda v: jnp.sum(jnp.where(pick, v, 0), axis=1)
    wt = sel(first_t) + (w - sel(w_end - n_t))
    lo = sel(start)
    hi = lo + sel(cnt)
    slot = sel(jnp.cumsum((cnt > 0).astype(jnp.int32)) - 1) % 2
    w_next = sel(w_end)
    nxt = jnp.where(w_next < n_work, jnp.sum((w_end[None, :] <= w_next[:, None]).astype(jnp.int32), axis=1), -1)
    i32 = lambda v: v.astype(jnp.int32)
    return i32(dest), (i32(wt), i32(we), i32(lo), i32(hi), i32(n_work.reshape(1)), i32(slot), i32(nxt))


def _dispatch_kernel(dest_ref, h_ref, xs_hbm, stage, sems):
    i = pl.program_id(0)
    rows = h_ref.shape[0]
    base = i * rows * TOP_K

    def drain(s):
        for _ in range(TOP_K):
            pltpu.make_async_copy(stage.at[s], xs_hbm.at[pl.ds(0, rows)], sems.at[s]).wait()

    for s in range(2):
        @pl.when(i % 2 == s)
        def _(s=s):
            stage[s] = h_ref[...]

            def issue(t, carry):
                for k in range(TOP_K):
                    row = dest_ref[base + t * TOP_K + k]
                    pltpu.make_async_copy(stage.at[s, pl.ds(t, 1)], xs_hbm.at[pl.ds(row, 1)], sems.at[s]).start()
                return carry

            lax.fori_loop(0, rows, issue, 0)

            @pl.when(i > 0)
            def _():
                drain(1 - s)

            @pl.when(i == pl.num_programs(0) - 1)
            def _():
                drain(s)


def _dispatch(h2, dest):
    n, d = h2.shape
    return pl.pallas_call(
        _dispatch_kernel,
        grid_spec=pltpu.PrefetchScalarGridSpec(
            num_scalar_prefetch=1,
            grid=(n // ROW_TILE,),
            in_specs=[pl.BlockSpec((ROW_TILE, d), lambda i, dest: (i, 0))],
            out_specs=pl.BlockSpec(memory_space=pl.ANY),
            scratch_shapes=[pltpu.VMEM((2, ROW_TILE, d), F32), pltpu.SemaphoreType.DMA((2,))]),
        out_shape=jax.ShapeDtypeStruct((n * TOP_K, d), F32),
        compiler_params=_params("arbitrary"),
        name="moe_dispatch",
    )(dest, h2)


def _experts_kernel(wt_ref, we_ref, lo_ref, hi_ref, nw_ref, slot_ref, nxt_ref, xs_ref, w1_hbm, b1_ref, w2_hbm,
                    b2_ref, y_ref, w1f, w2f, w1b, w2b, sems, *, f_chunk, layer):
    w = pl.program_id(0)
    valid = w < nw_ref[0]
    prev = jnp.maximum(w - 1, 0)
    new_expert = jnp.logical_or(w == 0, we_ref[w] != we_ref[prev])
    new_tile = jnp.logical_or(w == 0, wt_ref[w] != wt_ref[prev])

    def weight_copies(e, s):
        return (pltpu.make_async_copy(w1_hbm.at[layer, e], w1f.at[s], sems.at[0, s]),
                pltpu.make_async_copy(w2_hbm.at[layer, e], w2f.at[s], sems.at[1, s]))

    @pl.when(w == 0)
    def _():
        for cp in weight_copies(we_ref[0], slot_ref[0]):
            cp.start()

    @pl.when(jnp.logical_and(valid, new_expert))
    def _():
        s = slot_ref[w]
        for cp in weight_copies(we_ref[w], s):
            cp.wait()
        w1b[...] = w1f[s].astype(BF16)
        w2b[...] = w2f[s].astype(BF16)

        @pl.when(nxt_ref[w] >= 0)
        def _():
            for cp in weight_copies(nxt_ref[w], 1 - s):
                cp.start()

    @pl.when(valid)
    def _():
        x = xs_ref[...].astype(BF16)
        acc = jnp.zeros(y_ref.shape, F32)
        for c in range(D_FF // f_chunk):
            c0 = c * f_chunk
            glu = jnp.dot(x, w1b[:, c0:c0 + f_chunk], preferred_element_type=F32) + b1_ref[0, :, c0:c0 + f_chunk]
            lin = (jnp.dot(x, w1b[:, D_FF + c0:D_FF + c0 + f_chunk], preferred_element_type=F32)
                   + b1_ref[0, :, D_FF + c0:D_FF + c0 + f_chunk])
            glu = jnp.minimum(glu, SWIGLU_LIMIT)
            lin = jnp.clip(lin, -SWIGLU_LIMIT, SWIGLU_LIMIT)
            act = (lin + 1.0) * glu * _sigmoid(SWIGLU_ALPHA * glu)
            acc += jnp.dot(act.astype(BF16), w2b[c0:c0 + f_chunk, :], preferred_element_type=F32)
        y = acc + b2_ref[0]
        rows = wt_ref[w] * EXPERT_TILE + lax.broadcasted_iota(jnp.int32, (EXPERT_TILE, 1), 0)
        mine = jnp.logical_and(rows >= lo_ref[w], rows < hi_ref[w])

        @pl.when(new_tile)
        def _():
            y_ref[...] = jnp.where(mine, y, 0.0)

        @pl.when(jnp.logical_not(new_tile))
        def _():
            y_ref[...] = jnp.where(mine, y, y_ref[...])


def _experts(xs, work, w1, b1, w2, b2, layer):
    n_asg, d = xs.shape
    n_l, n_e = w1.shape[:2]
    n_work_max = work[0].shape[0]
    return pl.pallas_call(
        functools.partial(_experts_kernel, f_chunk=512, layer=layer),
        grid_spec=pltpu.PrefetchScalarGridSpec(
            num_scalar_prefetch=7,
            grid=(n_work_max,),
            in_specs=[pl.BlockSpec((EXPERT_TILE, d), lambda w, wt, we, *_: (wt[w], 0)),
                      pl.BlockSpec(memory_space=pl.ANY),
                      pl.BlockSpec((None, 1, 1, 2 * D_FF), lambda w, wt, we, *_: (layer, we[w], 0, 0)),
                      pl.BlockSpec(memory_space=pl.ANY),
                      pl.BlockSpec((None, 1, 1, d), lambda w, wt, we, *_: (layer, we[w], 0, 0))],
            out_specs=pl.BlockSpec((EXPERT_TILE, d), lambda w, wt, we, *_: (wt[w], 0)),
            scratch_shapes=[pltpu.VMEM((2, d, 2 * D_FF), F32), pltpu.VMEM((2, D_FF, d), F32),
                            pltpu.VMEM((d, 2 * D_FF), BF16), pltpu.VMEM((D_FF, d), BF16),
                            pltpu.SemaphoreType.DMA((2, 2))]),
        out_shape=jax.ShapeDtypeStruct((n_asg, d), F32),
        compiler_params=_params("arbitrary"),
        name="moe_experts",
    )(*work, xs, w1, b1.reshape(n_l, n_e, 1, -1), w2, b2.reshape(n_l, n_e, 1, -1))


def _combine_kernel(dest_ref, y_hbm, topw_ref, x_ref, mods_ref, lng_ref, lnb_ref, o_ref, ybuf, sems, *, alpha):
    rows = x_ref.shape[1]
    tile = pl.program_id(0) * pl.num_programs(1) + pl.program_id(1)
    n_tiles = pl.num_programs(0) * pl.num_programs(1)

    def gather(which, s):
        base = which * rows * TOP_K

        def issue(t, carry):
            for k in range(TOP_K):
                row = dest_ref[base + t * TOP_K + k]
                pltpu.make_async_copy(y_hbm.at[pl.ds(row, 1)], ybuf.at[s, k, pl.ds(t, 1)], sems.at[s]).start()
            return carry

        lax.fori_loop(0, rows, issue, 0)

    @pl.when(tile == 0)
    def _():
        gather(0, 0)

    for s in range(2):
        @pl.when(tile % 2 == s)
        def _(s=s):
            @pl.when(tile + 1 < n_tiles)
            def _():
                gather(tile + 1, 1 - s)

            for k in range(TOP_K):
                pltpu.make_async_copy(y_hbm.at[pl.ds(0, rows)], ybuf.at[s, k], sems.at[s]).wait()
            tw = topw_ref[0]
            ffn = ybuf[s, 0] * tw[:, 0:1]
            for k in range(1, TOP_K):
                ffn = ffn + ybuf[s, k] * tw[:, k:k + 1]
            mods = mods_ref[0, 0]
            x2 = alpha * x_ref[0] + mods[5:6] * ffn
            mu = jnp.mean(x2, axis=1, keepdims=True)
            xc = x2 - mu
            var = jnp.mean(xc * xc, axis=1, keepdims=True)
            o_ref[0] = xc * lax.rsqrt(var + LN_EPS) * lng_ref[...] + lnb_ref[...]


def _combine(y_sorted, dest, topw, x1, mods, ln_g, ln_b, n_ctx_tiles, alpha, first_tile):
    bz, t, d = x1.shape
    seg = lambda i: (i + first_tile >= n_ctx_tiles).astype(jnp.int32)
    tok = lambda n: pl.BlockSpec((1, ROW_TILE, n), lambda b, i, dest: (b, i, 0))
    vec = pl.BlockSpec((1, d), lambda b, i, dest: (0, 0))
    return pl.pallas_call(
        functools.partial(_combine_kernel, alpha=alpha),
        grid_spec=pltpu.PrefetchScalarGridSpec(
            num_scalar_prefetch=1,
            grid=(bz, t // ROW_TILE),
            in_specs=[pl.BlockSpec(memory_space=pl.ANY), tok(LANES), tok(d),
                      pl.BlockSpec((1, 1, 8, d), lambda b, i, dest: (b, seg(i), 0, 0)), vec, vec],
            out_specs=tok(d),
            scratch_shapes=[pltpu.VMEM((2, TOP_K, ROW_TILE, d), F32), pltpu.SemaphoreType.DMA((2,))]),
        out_shape=jax.ShapeDtypeStruct((bz, t, d), F32),
        compiler_params=_params("arbitrary", "arbitrary"),
        name="moe_combine",
    )(dest, y_sorted, topw, x1, mods, ln_g.reshape(1, d), ln_b.reshape(1, d))


def _rope_tables(ctx_len, seq):
    idx = jnp.arange(seq)
    rowp = (idx // GRID_W).astype(F32)
    colp = (idx % GRID_W).astype(F32)
    n_freq = C_HEAD_DIM // 4
    inv = ROPE_BASE ** (-jnp.arange(n_freq, dtype=F32) / n_freq)
    ang = jnp.concatenate([rowp[:, None] * inv, colp[:, None] * inv], -1)
    cos, sin = jnp.cos(ang), jnp.sin(ang)
    cos2 = jnp.concatenate([cos, cos], -1)
    sin2 = jnp.concatenate([-sin, sin], -1)
    cos2 = jnp.concatenate([jnp.ones((ctx_len, C_HEAD_DIM), F32), cos2], 0)
    sin2 = jnp.concatenate([jnp.zeros((ctx_len, C_HEAD_DIM), F32), sin2], 0)
    return cos2, sin2


def kernel(x, c, ctx, c_ctx, w_mod, b_mod, ln_g, ln_b, even_w_in, even_w_out, a_mu, a_w0, a_wB, a_a0, a_aB, a_gB, a_kk, a_ka, a_rk, a_gn_g, a_gn_b, b_conv_w, b_conv_b, b_gate_w, b_gate_b, b_lam, odd_w_in, odd_w_out, c_log_gamma, c_gn_g, c_gn_b, d_ibias, d_fbias, d_gn_g, d_gn_b, router_w, router_b, exp_w1, exp_b1, exp_w2, exp_b2):
    bz, seq, d = x.shape
    ctx_len = ctx.shape[1]
    depth = w_mod.shape[0]
    assert ctx_len == ROW_TILE and seq % ROW_TILE == 0
    t = ctx_len + seq
    n_ctx_tiles = ctx_len // ROW_TILE
    alpha = (2 * depth) ** 0.25
    xs = jnp.concatenate([ctx, x], axis=1)
    c8 = jnp.concatenate([c, c_ctx[None], jnp.zeros((8 - bz - 1, d), F32)], 0)
    cos2, sin2 = _rope_tables(ctx_len, seq)
    row = lambda a: a.reshape(1, -1)
    vec = lambda n: pl.BlockSpec((1, n), lambda b, i: (0, 0))

    for layer in range(depth):
        first = n_ctx_tiles if layer == depth - 1 else 0
        tok = lambda n, first=first: pl.BlockSpec((1, ROW_TILE, n), lambda b, i: (b, i + first, 0))
        tok2 = lambda n, first=first: pl.BlockSpec((2, 1, ROW_TILE, n), lambda b, i: (0, b, i + first, 0))
        m = _mod_vectors(c8, w_mod, b_mod[layer], layer).reshape(8, N_MOD, d)
        m_lat = m[:bz]
        m_ctx = jnp.broadcast_to(m[bz][None], (bz, N_MOD, d))
        mods = jnp.stack([m_ctx, m_lat], axis=1)
        mods = jnp.concatenate([mods, jnp.zeros((bz, 2, 8 - N_MOD, d), F32)], axis=2)
        shift = mods[:, :, 0:1]
        scale = mods[:, :, 1:2]
        rw = jnp.pad(router_w[layer], ((0, 0), (0, LANES - N_EXPERTS)))
        rb = jnp.pad(router_b[layer], (0, LANES - N_EXPERTS)).reshape(1, LANES)
        li = layer // 2
        if layer % 2 == 0:
            pa, pb = _inproj(xs, shift, scale, even_w_in[li].astype(BF16), A_IN, n_ctx_tiles)
            r, v, kk, g, bonus, lw, kd, a = _rwkv_prep(pa, a_mu[li], a_w0[li], a_wB[li], a_a0[li], a_aB[li],
                                                       a_gB[li], a_kk[li], a_ka[li], a_rk[li].reshape(-1),
                                                       n_ctx_tiles)
            wkv_f, wkv_b = _rwkv_scan(r, lw, kd, v, kk, a, ctx_len)
            hb = _rglru(pb, b_conv_w[li], b_conv_b[li], b_gate_w[li], b_gate_b[li], b_lam[li], n_ctx_tiles)
            mixer_args = (wkv_f, wkv_b, bonus, g, row(a_gn_g[li]), row(a_gn_b[li]), hb, _group_ones(A_W, A_HEAD_DIM))
            mixer_specs = (tok(A_W), tok(A_W), tok(A_W), tok(A_W), vec(A_W), vec(A_W), tok2(B_W),
                           pl.BlockSpec((A_W, A_W), lambda b, i: (0, 0)))
            x1, h2, topi, topw, rank, counts = _outproj('even', mixer_args, mixer_specs, xs, even_w_out[li].astype(BF16), mods,
                                     ln_g[layer, 0], ln_b[layer, 0], rw, rb, n_ctx_tiles, alpha, first)
        else:
            w_in = jnp.pad(odd_w_in[li], ((0, 0), (0, D_IN_PAD - D_IN))).astype(BF16)
            pc, pd = _inproj(xs, shift, scale, w_in, C_IN, n_ctx_tiles)
            oc = _retention(pc, c_log_gamma[li], cos2, sin2, ctx_len)
            bias128 = jnp.concatenate([d_ibias[li].reshape(-1), d_fbias[li].reshape(-1),
                                       jnp.zeros((LANES - 4 * D_HEADS,), F32)]).reshape(1, LANES)
            od = _mlstm(pd, bias128, ctx_len)
            gate_c = pl.BlockSpec((1, ROW_TILE, C_W), lambda b, i, first=first: (b, i + first, 3))
            gate_d = pl.BlockSpec((1, ROW_TILE, D_W), lambda b, i, first=first: (b, i + first, 3))
            mixer_args = (oc, pc, row(c_gn_g[li]), row(c_gn_b[li]), od, pd, row(d_gn_g[li]), row(d_gn_b[li]))
            mixer_specs = (tok2(C_W), gate_c, vec(C_W), vec(C_W), tok2(D_W), gate_d, vec(D_W), vec(D_W))
            x1, h2, topi, topw, rank, counts = _outproj('odd', mixer_args, mixer_specs, xs, odd_w_out[li].astype(BF16), mods,
                                     ln_g[layer, 0], ln_b[layer, 0], rw, rb, n_ctx_tiles, alpha, first)
        dest, work = _route(topi, rank, counts)
        xs_sorted = _dispatch(h2.reshape(-1, d), dest)
        y_sorted = _experts(xs_sorted, work, exp_w1, exp_b1, exp_w2, exp_b2, layer)
        xs = _combine(y_sorted, dest, topw, x1, mods, ln_g[layer, 1], ln_b[layer, 1], n_ctx_tiles, alpha, first)
    return xs[:, xs.shape[1] - seq:]
```

```python
import functools
import math

import jax
import jax.numpy as jnp
from jax import lax
from jax.experimental import pallas as pl
from jax.experimental.pallas import tpu as pltpu

F32 = jnp.float32
BF16 = jnp.bfloat16

GRID_W = 64
A_HEADS = 8
A_HEAD_DIM = 64
A_W = 512
A_DECAY_RANK = 64
A_ICL_RANK = 64
A_GATE_RANK = 128
A_IN = 1920
A_DECAY_SCALE = math.exp(-0.5)
GN_EPS_RWKV = 64e-5
B_W = 512
B_BLOCKS = 8
B_C = 8.0
B_IN = 1024
C_HEADS = 4
C_HEAD_DIM = 128
C_W = 512
C_IN = 2048
ROPE_BASE = 10000.0
D_HEADS = 4
D_HEAD_DIM = 128
D_W = 512
D_IN = 2064
D_IN_PAD = 2176
N_EXPERTS = 32
TOP_K = 4
D_FF = 1024
SWIGLU_ALPHA = 1.702
SWIGLU_LIMIT = 7.0
LN_EPS = 1e-5
N_MOD = 6
LANES = 128

ROW_TILE = 256
RWKV_CHUNK = 64
RWKV_STEP_CHUNKS = 4
ODD_CHUNK = 256
EXPERT_TILE = 512
VMEM_LIMIT_BYTES = 56 * 1024 * 1024


def _params(*sem):
    return pltpu.CompilerParams(dimension_semantics=sem, vmem_limit_bytes=VMEM_LIMIT_BYTES)


def _dot(a, b):
    return jnp.dot(a.astype(BF16), b.astype(BF16), preferred_element_type=F32)


def _dot_nt(a, b):
    return lax.dot_general(a.astype(BF16), b.astype(BF16), (((1,), (1,)), ((), ())),
                           preferred_element_type=F32)


def _dot_tn(a, b):
    return lax.dot_general(a.astype(BF16), b.astype(BF16), (((0,), (0,)), ((), ())),
                           preferred_element_type=F32)


def _split(x):
    hi = x.astype(BF16)
    lo = (x - hi.astype(F32)).astype(BF16)
    return hi, lo


def _dot_x3(a, b):
    ah, al = _split(a)
    bh, bl = _split(b)
    out = jnp.dot(ah, bh, preferred_element_type=F32)
    out += jnp.dot(ah, bl, preferred_element_type=F32)
    out += jnp.dot(al, bh, preferred_element_type=F32)
    return out


def _dot_nt_x3(a, b):
    ah, al = _split(a)
    bh, bl = _split(b)
    dn = (((1,), (1,)), ((), ()))
    out = lax.dot_general(ah, bh, dn, preferred_element_type=F32)
    out += lax.dot_general(ah, bl, dn, preferred_element_type=F32)
    out += lax.dot_general(al, bh, dn, preferred_element_type=F32)
    return out


def _dot_exact_lhs(a, b):
    a = a.astype(BF16)
    b1 = b.astype(BF16)
    r1 = b - b1.astype(F32)
    b2 = r1.astype(BF16)
    b3 = (r1 - b2.astype(F32)).astype(BF16)
    out = jnp.dot(a, b1, preferred_element_type=F32)
    out += jnp.dot(a, b2, preferred_element_type=F32)
    out += jnp.dot(a, b3, preferred_element_type=F32)
    return out


def _sigmoid(x):
    return 1.0 / (1.0 + jnp.exp(-x))


def _softplus(x):
    return jnp.maximum(x, 0.0) + jnp.log(1.0 + jnp.exp(-jnp.abs(x)))


def _seg_order(i, d, n_ctx, n_tot):
    rev = jnp.where(i < n_ctx, n_ctx - 1 - i, n_ctx + n_tot - 1 - i)
    return jnp.where(d == 0, i, rev)


def _mod_kernel(c_ref, w_ref, b_ref, o_ref):
    c = c_ref[...]
    s = c * _sigmoid(c)
    o_ref[...] = _dot_x3(s, w_ref[...]) + b_ref[...]


def _mod_vectors(c8, w, b, layer):
    _, d, n = w.shape
    tn = n // 4
    return pl.pallas_call(
        _mod_kernel,
        grid=(n // tn,),
        in_specs=[pl.BlockSpec((8, d), lambda j: (0, 0)),
                  pl.BlockSpec((None, d, tn), lambda j: (layer, 0, j)),
                  pl.BlockSpec((1, tn), lambda j: (0, j))],
        out_specs=pl.BlockSpec((8, tn), lambda j: (0, j)),
        out_shape=jax.ShapeDtypeStruct((8, n), F32),
        compiler_params=_params("parallel"),
        name="mod_vectors",
    )(c8, w, b.reshape(1, n))


def _inproj_kernel(x_ref, sh_ref, sc_ref, w_ref, oa_ref, ob_ref, *, n_a):
    h = x_ref[0] * (1.0 + sc_ref[0, 0]) + sh_ref[0, 0]
    p = jnp.dot(h.astype(BF16), w_ref[...], preferred_element_type=F32)
    oa_ref[0] = p[:, :n_a]
    ob_ref[0] = p[:, n_a:]


def _inproj(x, shift, scale, w_bf16, n_a, n_ctx_tiles):
    bz, t, d = x.shape
    n = w_bf16.shape[1]
    n_b = n - n_a
    seg = lambda i: (i >= n_ctx_tiles).astype(jnp.int32)
    return pl.pallas_call(
        functools.partial(_inproj_kernel, n_a=n_a),
        grid=(bz, t // ROW_TILE),
        in_specs=[pl.BlockSpec((1, ROW_TILE, d), lambda b, i: (b, i, 0)),
                  pl.BlockSpec((1, 1, 1, d), lambda b, i: (b, seg(i), 0, 0)),
                  pl.BlockSpec((1, 1, 1, d), lambda b, i: (b, seg(i), 0, 0)),
                  pl.BlockSpec((d, n), lambda b, i: (0, 0))],
        out_specs=[pl.BlockSpec((1, ROW_TILE, n_a), lambda b, i: (b, i, 0)),
                   pl.BlockSpec((1, ROW_TILE, n_b), lambda b, i: (b, i, 0))],
        out_shape=[jax.ShapeDtypeStruct((bz, t, n_a), F32),
                   jax.ShapeDtypeStruct((bz, t, n_b), F32)],
        compiler_params=_params("parallel", "parallel"),
        name="inproj",
    )(x, shift, scale, w_bf16)


def _rwkv_prep_kernel(p_ref, up_ref, dn_ref, mu_ref, w0_ref, wb_ref, a0_ref, ab_ref, gb_ref,
                      kkw_ref, kaw_ref, rkw_ref, ones_ref,
                      r_o, v_o, kk_o, g_o, bonus_o, lw_o, kd_o, a_o, *, n_ctx_tiles, n_tiles):
    i = pl.program_id(1)
    p = p_ref[0]
    tr, w = p.shape
    row = lax.broadcasted_iota(jnp.int32, (tr, w), 0)
    lane = lax.broadcasted_iota(jnp.int32, (tr, w), 1)
    prev = pltpu.roll(p, 1, 0)
    nxt = pltpu.roll(p, tr - 1, 0)
    sh_ctx = jnp.where(lane % 2 == 0, jnp.where(row == 0, 0.0, prev), jnp.where(row == tr - 1, 0.0, nxt))
    col = row % GRID_W
    left = jnp.where(col == 0, 0.0, prev)
    right = jnp.where(col == GRID_W - 1, 0.0, nxt)
    up_halo = jnp.where(i > n_ctx_tiles, up_ref[0], 0.0)
    dn_halo = jnp.where(i < n_tiles - 1, dn_ref[0], 0.0)
    up = jnp.concatenate([up_halo, p[:tr - GRID_W]], axis=0)
    down = jnp.concatenate([p[GRID_W:], dn_halo], axis=0)
    slot = lane % 4
    sh_lat = jnp.where(slot == 0, left, jnp.where(slot == 1, right, jnp.where(slot == 2, up, down)))
    sh = jnp.where(i < n_ctx_tiles, sh_ctx, sh_lat)
    z = p + (sh - p) * mu_ref[...]

    r = z[:, 0:A_W]
    k = z[:, A_W:2 * A_W]
    v = z[:, 2 * A_W:3 * A_W]
    o = 3 * A_W
    wd = z[:, o:o + 2 * A_DECAY_RANK]
    o += 2 * A_DECAY_RANK
    ad = z[:, o:o + 2 * A_ICL_RANK]
    o += 2 * A_ICL_RANK
    zg = z[:, o:o + A_GATE_RANK]

    ones_bd = ones_ref[...]

    def head_sum(x):
        hi, lo = _split(x)
        return (jnp.dot(hi, ones_bd, preferred_element_type=F32)
                + jnp.dot(lo, ones_bd, preferred_element_type=F32))

    g_o[0] = _dot(_sigmoid(zg), gb_ref[...]).astype(BF16)
    kk = k * kkw_ref[...]
    kk = kk / jnp.maximum(jnp.sqrt(head_sum(kk * kk)), 1e-12)
    r_o[0] = r.astype(BF16)
    v_o[0] = v.astype(BF16)
    kk_o[0] = kk.astype(BF16)
    rr = r * rkw_ref[...]
    bsum = jnp.zeros_like(r)
    for d in range(2):
        wdd = jnp.tanh(wd[:, d * A_DECAY_RANK:(d + 1) * A_DECAY_RANK])
        lw = -A_DECAY_SCALE * _sigmoid(w0_ref[d] + _dot(wdd, wb_ref[d]))
        a = _sigmoid(a0_ref[d] + _dot(ad[:, d * A_ICL_RANK:(d + 1) * A_ICL_RANK], ab_ref[d]))
        kd = k * (1.0 + (a - 1.0) * kaw_ref[...])
        lw_o[d, 0] = lw
        kd_o[d, 0] = kd.astype(BF16)
        a_o[d, 0] = a.astype(BF16)
        bsum = bsum + head_sum(rr * kd)
    bonus_o[0] = (bsum * v).astype(BF16)


def _rwkv_prep(pa, mu, w0, wb, a0, ab, gb, kkw, kaw, rkw, n_ctx_tiles):
    bz, t, _ = pa.shape
    n_tiles = t // ROW_TILE
    hb = ROW_TILE // GRID_W
    n_hblk = t // GRID_W
    ones_bd = jnp.kron(jnp.eye(A_HEADS, dtype=F32), jnp.ones((A_HEAD_DIM, A_HEAD_DIM), F32)).astype(BF16)
    row = lambda a: a.reshape(1, -1)
    tok = pl.BlockSpec((1, ROW_TILE, A_W), lambda b, i: (b, i, 0))
    tok2 = pl.BlockSpec((2, 1, ROW_TILE, A_W), lambda b, i: (0, b, i, 0))
    full = lambda a: pl.BlockSpec(a.shape, lambda b, i: (0,) * a.ndim)
    args = (row(mu), w0.reshape(2, 1, A_W), wb.astype(BF16), a0.reshape(2, 1, A_W), ab.astype(BF16),
            gb.astype(BF16), row(kkw), row(kaw), row(rkw), ones_bd)
    s1 = jax.ShapeDtypeStruct((bz, t, A_W), BF16)
    s2 = jax.ShapeDtypeStruct((2, bz, t, A_W), BF16)
    s2f = jax.ShapeDtypeStruct((2, bz, t, A_W), F32)
    return pl.pallas_call(
        functools.partial(_rwkv_prep_kernel, n_ctx_tiles=n_ctx_tiles, n_tiles=n_tiles),
        grid=(bz, n_tiles),
        in_specs=[pl.BlockSpec((1, ROW_TILE, A_IN), lambda b, i: (b, i, 0)),
                  pl.BlockSpec((1, GRID_W, A_IN), lambda b, i: (b, jnp.maximum(i * hb - 1, 0), 0)),
                  pl.BlockSpec((1, GRID_W, A_IN), lambda b, i: (b, jnp.minimum((i + 1) * hb, n_hblk - 1), 0))]
                 + [full(a) for a in args],
        out_specs=[tok, tok, tok, tok, tok, tok2, tok2, tok2],
        out_shape=[s1, s1, s1, s1, s1, s2f, s2, s2],
        compiler_params=_params("parallel", "parallel"),
        name="rwkv_prep",
    )(pa, pa, pa, *args)


_BMM_NN = (((2,), (1,)), ((0,), (0,)))
_BMM_NT = (((2,), (2,)), ((0,), (0,)))
_BMM_TN = (((1,), (1,)), ((0,), (0,)))
RWKV_CHAINS = 2 * A_HEADS


def _bmm(a, b, dims):
    return lax.dot_general(a.astype(BF16), b.astype(BF16), dims, preferred_element_type=F32)


def _bmm_x3(a, b, dims):
    ah, al = _split(a)
    bh, bl = _split(b)
    rows = a.shape[1]
    p = lax.dot_general(jnp.concatenate([ah, al], axis=1), bh, dims, preferred_element_type=F32)
    return p[:, :rows] + p[:, rows:] + lax.dot_general(ah, bl, dims, preferred_element_type=F32)


def _split_heads(x):
    return jnp.stack([x[:, h * A_HEAD_DIM:(h + 1) * A_HEAD_DIM] for h in range(A_HEADS)], axis=0)


def _rwkv_scan_kernel(rf, lwf, kdf, vf, kkf, af, rb, lwb, kdb, vb, kkb, ab, of, ob, s_ref):
    i = pl.program_id(1)

    @pl.when(i == 0)
    def _():
        s_ref[...] = jnp.zeros_like(s_ref)

    n = RWKV_CHUNK
    n_sub = rf.shape[1] // n
    ti = lax.broadcasted_iota(jnp.int32, (n, n), 0)
    tj = lax.broadcasted_iota(jnp.int32, (n, n), 1)
    for c_idx in range(n_sub):
        rows_f = pl.ds(c_idx * n, n)
        rows_b = pl.ds((n_sub - 1 - c_idx) * n, n)
        _rwkv_chunk((rf, lwf, kdf, vf, kkf, af), (rb, lwb, kdb, vb, kkb, ab), of, ob, s_ref, rows_f, rows_b, ti, tj)


def _rwkv_chunk(fwd_refs, bwd_refs, of, ob, s_ref, rows_f, rows_b, ti, tj):
    n = RWKV_CHUNK

    def decayed(refs, rows, fwd):
        r_ref, lw_ref, kd_ref, v_ref, kk_ref, a_ref = refs
        lw = lw_ref[0, 0, rows, :]
        kd = kd_ref[0, 0, rows, :].astype(F32)
        kk = kk_ref[0, rows, :].astype(F32)
        b = kk * a_ref[0, 0, rows, :].astype(F32)
        incl = (tj <= ti) if fwd else (tj >= ti)
        c = _dot_exact_lhs(incl.astype(F32), lw)
        ctot = jnp.sum(lw, axis=0, keepdims=True)
        e_nc = jnp.exp(-c)
        e_tot = jnp.exp(ctot - c)
        return dict(rt=r_ref[0, rows, :].astype(F32) * jnp.exp(c), at=kk * jnp.exp(c - lw), kt=kd * e_nc,
                    bt=b * e_nc, kh=kd * e_tot, bh=b * e_tot, v=v_ref[0, rows, :].astype(F32), e=jnp.exp(ctot))

    pf = decayed(fwd_refs, rows_f, True)
    pb = decayed(bwd_refs, rows_b, False)

    def chains(name):
        return jnp.concatenate([_split_heads(pf[name]), _split_heads(pb[name])], axis=0)

    rt, at, kt, bt, kh, bh, v = (chains(x) for x in ("rt", "at", "kt", "bt", "kh", "bh", "v"))
    ar = jnp.concatenate([at, rt], axis=1)
    kb = jnp.concatenate([kt, bt], axis=1)
    m = _bmm(ar, kb, _BMM_NT)
    chain = lax.broadcasted_iota(jnp.int32, (RWKV_CHAINS, n, n), 0)
    lag = jnp.where(chain < A_HEADS, ti - tj, tj - ti)
    before = lag > 0
    before_eq = lag >= 0
    a_ak = jnp.where(before, m[:, :n, :n], 0.0)
    a_ab = jnp.where(before, m[:, :n, n:], 0.0)
    a_rk = jnp.where(before_eq, m[:, n:, :n], 0.0)
    a_rb = jnp.where(before_eq, m[:, n:, n:], 0.0)
    eye = (ti == tj).astype(F32)
    pw = -a_ab
    tinv = eye + pw
    for _ in range(int(math.log2(n)) - 1):
        pw = _bmm_x3(pw, pw, _BMM_NN)
        tinv = tinv + _bmm_x3(tinv, pw, _BMM_NN)
    sk = s_ref[...]
    a_s = _bmm(ar, sk, _BMM_NN)
    u = _bmm(tinv, a_s[:, :n] + _bmm(a_ak, v, _BMM_NN), _BMM_NN)
    vu = jnp.concatenate([v, -u], axis=1)
    out = a_s[:, n:] + _bmm(jnp.concatenate([a_rk, a_rb], axis=2), vu, _BMM_NN)
    e_rows = jnp.concatenate([_split_heads(pf["e"]), _split_heads(pb["e"])], axis=0)
    e_col = jnp.sum(eye * e_rows, axis=2, keepdims=True)
    s_ref[...] = sk * e_col + _bmm(jnp.concatenate([kh, bh], axis=1), vu, _BMM_TN)
    for h in range(A_HEADS):
        sl = slice(h * A_HEAD_DIM, (h + 1) * A_HEAD_DIM)
        of[0, rows_f, sl] = out[h]
        ob[0, rows_b, sl] = out[A_HEADS + h]


def _rwkv_scan(r, lw, kd, v, kk, a, ctx_len):
    bz, t, _ = r.shape
    blk = RWKV_CHUNK * RWKV_STEP_CHUNKS
    n_ctx = ctx_len // blk
    n_blocks = t // blk
    rev = lambda i: _seg_order(i, 1, n_ctx, n_blocks)
    tok_f = pl.BlockSpec((1, blk, A_W), lambda b, i: (b, i, 0))
    tok_b = pl.BlockSpec((1, blk, A_W), lambda b, i: (b, rev(i), 0))
    dir_f = pl.BlockSpec((1, 1, blk, A_W), lambda b, i: (0, b, i, 0))
    dir_b = pl.BlockSpec((1, 1, blk, A_W), lambda b, i: (1, b, rev(i), 0))
    shp = jax.ShapeDtypeStruct((bz, t, A_W), F32)
    return pl.pallas_call(
        _rwkv_scan_kernel,
        grid=(bz, n_blocks),
        in_specs=[tok_f, dir_f, dir_f, tok_f, tok_f, dir_f, tok_b, dir_b, dir_b, tok_b, tok_b, dir_b],
        out_specs=[tok_f, tok_b],
        out_shape=[shp, shp],
        scratch_shapes=[pltpu.VMEM((RWKV_CHAINS, A_HEAD_DIM, A_HEAD_DIM), F32)],
        compiler_params=_params("parallel", "arbitrary"),
        name="rwkv_scan",
    )(r, lw, kd, v, kk, a, r, lw, kd, v, kk, a)


def _rglru_kernel(p_ref, pv_ref, nx_ref, cw_ref, cb_ref, gw_ref, gbias_ref, lam_ref, o_ref,
                  a_scr, b_scr, h_scr, carry_ref, *, n_ctx_tiles, n_tiles):
    d = pl.program_id(1)
    i = pl.program_id(2)
    blk = _seg_order(i, d, n_ctx_tiles, n_tiles)

    @pl.when(i == 0)
    def _():
        carry_ref[...] = jnp.zeros_like(carry_ref)

    p = p_ref[0]
    tr = p.shape[0]
    x = p[:, :B_W]
    gate_in = p[:, B_W:]
    has_prev = jnp.logical_and(blk != 0, blk != n_ctx_tiles)
    has_next = jnp.logical_and(blk != n_ctx_tiles - 1, blk != n_tiles - 1)
    pv = jnp.where(has_prev, pv_ref[0][:, :B_W], 0.0)
    nx = jnp.where(has_next, nx_ref[0][:, :B_W], 0.0)
    row = lax.broadcasted_iota(jnp.int32, (tr, B_W), 0)
    pv_m1 = pv[7:8]
    pv_m2 = pv[6:7]
    nx_p1 = nx[0:1]
    x_m1 = jnp.where(row == 0, pv_m1, pltpu.roll(x, 1, 0))
    x_m2 = jnp.where(row == 0, pv_m2, jnp.where(row == 1, pv_m1, pltpu.roll(x, 2, 0)))
    x_p1 = jnp.where(row == tr - 1, nx_p1, pltpu.roll(x, tr - 1, 0))
    cw = cw_ref[...]
    u = cw[0:1] * x_m2 + cw[1:2] * x_m1 + cw[2:3] * x + cw[3:4] * x_p1 + cb_ref[...]
    gate = 0.5 * gate_in * (1.0 + jnp.tanh(math.sqrt(2.0 / math.pi) * (gate_in + 0.044715 * gate_in ** 3)))
    pre = _dot(u, gw_ref[0]) + gbias_ref[0]
    rec = _sigmoid(pre[:, :B_W])
    inp = _sigmoid(pre[:, B_W:])
    log_a = -B_C * rec * _softplus(-lam_ref[0])
    a_scr[...] = jnp.exp(log_a)
    b_scr[...] = jnp.sqrt(1.0 - jnp.exp(2.0 * log_a)) * inp * u

    n_grp = tr // 8
    g8 = lax.broadcasted_iota(jnp.int32, (8, B_W), 0)

    def group_scan(a, b, fwd):
        for s in (1, 2, 4):
            if fwd:
                ident = g8 < s
                a_p = pltpu.roll(a, s, 0)
                b_p = pltpu.roll(b, s, 0)
            else:
                ident = g8 >= 8 - s
                a_p = pltpu.roll(a, 8 - s, 0)
                b_p = pltpu.roll(b, 8 - s, 0)
            a_p = jnp.where(ident, 1.0, a_p)
            b_p = jnp.where(ident, 0.0, b_p)
            b = a * b_p + b
            a = a * a_p
        return a, b

    def run(fwd):
        def body(j, carry):
            g = j if fwd else n_grp - 1 - j
            off = pl.multiple_of(g * 8, 8)
            a, b = group_scan(a_scr[pl.ds(off, 8), :], b_scr[pl.ds(off, 8), :], fwd)
            h = b + a * carry
            h_scr[pl.ds(off, 8), :] = h
            return h[7:8] if fwd else h[0:1]
        carry_ref[...] = lax.fori_loop(0, n_grp, body, carry_ref[...], unroll=8)

    @pl.when(d == 0)
    def _():
        run(True)

    @pl.when(d == 1)
    def _():
        run(False)

    o_ref[0, 0] = (h_scr[...] * gate).astype(o_ref.dtype)


def _rglru(pb, conv_w, conv_b, gate_w, gate_b, lam, n_ctx_tiles):
    bz, t, _ = pb.shape
    n_tiles = t // ROW_TILE
    r8 = ROW_TILE // 8
    n8 = t // 8
    eye = jnp.eye(B_BLOCKS, dtype=F32)
    gw = jnp.einsum('dsgio,gh->dgisho', gate_w, eye).reshape(2, B_W, 2 * B_W).astype(BF16)
    gbias = gate_b.reshape(2, 1, 2 * B_W)
    blk = lambda i, d: _seg_order(i, d, n_ctx_tiles, n_tiles)
    return pl.pallas_call(
        functools.partial(_rglru_kernel, n_ctx_tiles=n_ctx_tiles, n_tiles=n_tiles),
        grid=(bz, 2, n_tiles),
        in_specs=[pl.BlockSpec((1, ROW_TILE, B_IN), lambda b, d, i: (b, blk(i, d), 0)),
                  pl.BlockSpec((1, 8, B_IN), lambda b, d, i: (b, jnp.maximum(blk(i, d) * r8 - 1, 0), 0)),
                  pl.BlockSpec((1, 8, B_IN), lambda b, d, i: (b, jnp.minimum((blk(i, d) + 1) * r8, n8 - 1), 0)),
                  pl.BlockSpec((4, B_W), lambda b, d, i: (0, 0)),
                  pl.BlockSpec((1, B_W), lambda b, d, i: (0, 0)),
                  pl.BlockSpec((1, B_W, 2 * B_W), lambda b, d, i: (d, 0, 0)),
                  pl.BlockSpec((1, 1, 2 * B_W), lambda b, d, i: (d, 0, 0)),
                  pl.BlockSpec((1, 1, B_W), lambda b, d, i: (d, 0, 0))],
        out_specs=pl.BlockSpec((1, 1, ROW_TILE, B_W), lambda b, d, i: (d, b, blk(i, d), 0)),
        out_shape=jax.ShapeDtypeStruct((2, bz, t, B_W), BF16),
        scratch_shapes=[pltpu.VMEM((ROW_TILE, B_W), F32), pltpu.VMEM((ROW_TILE, B_W), F32),
                        pltpu.VMEM((ROW_TILE, B_W), F32), pltpu.VMEM((1, B_W), F32)],
        compiler_params=_params("parallel", "parallel", "arbitrary"),
        name="rglru",
    )(pb, pb, pb, conv_w, conv_b.reshape(1, B_W), gw, gbias, lam.reshape(2, 1, B_W))


def _retention_kernel(lg_ref, p_ref, cos_ref, sin_ref, o_ref, st_ref):
    d = pl.program_id(1)
    i = pl.program_id(2)

    @pl.when(i == 0)
    def _():
        st_ref[...] = jnp.zeros_like(st_ref)

    n = p_ref.shape[1]
    cos2 = cos_ref[...]
    sin2 = sin_ref[...]
    ti = lax.broadcasted_iota(jnp.int32, (n, n), 0)
    tj = lax.broadcasted_iota(jnp.int32, (n, n), 1)
    fwd = d == 0
    diff = jnp.where(fwd, ti - tj, tj - ti).astype(F32)
    t1 = lax.broadcasted_iota(jnp.int32, (n, 1), 0)
    pos = jnp.where(fwd, t1, n - 1 - t1).astype(F32)
    for h in range(C_HEADS):
        lg = lg_ref[d, h] * jnp.ones((1, 1), F32)
        sl = slice(h * C_HEAD_DIM, (h + 1) * C_HEAD_DIM)
        q = p_ref[0, :, sl]
        k = p_ref[0, :, C_W + h * C_HEAD_DIM:C_W + (h + 1) * C_HEAD_DIM]
        v = p_ref[0, :, 2 * C_W + h * C_HEAD_DIM:2 * C_W + (h + 1) * C_HEAD_DIM]
        q = q * cos2 + pltpu.roll(q, C_HEAD_DIM // 2, 1) * sin2
        k = (k * cos2 + pltpu.roll(k, C_HEAD_DIM // 2, 1) * sin2) * (C_HEAD_DIM ** -0.5)
        decay = jnp.where(diff >= 0, jnp.exp(lg * jnp.maximum(diff, 0.0)), 0.0)
        scores = _dot_nt(q, k) * decay
        st = st_ref[h]
        out = _dot(scores, v) + _dot(q, st) * jnp.exp(lg * (pos + 1.0))
        zeta = jnp.exp(lg * (n - 1.0 - pos))
        st_ref[h] = st * jnp.exp(lg * n) + _dot_tn(k * zeta, v)
        o_ref[0, 0, :, sl] = out.astype(o_ref.dtype)


def _retention(pc, log_gamma, cos2, sin2, ctx_len):
    bz, t, _ = pc.shape
    n_ctx = ctx_len // ODD_CHUNK
    n_chunks = t // ODD_CHUNK
    blk = lambda i, d: _seg_order(i, d, n_ctx, n_chunks)
    return pl.pallas_call(
        _retention_kernel,
        grid_spec=pltpu.PrefetchScalarGridSpec(
            num_scalar_prefetch=1,
            grid=(bz, 2, n_chunks),
            in_specs=[pl.BlockSpec((1, ODD_CHUNK, 3 * C_W), lambda b, d, i, lg: (b, blk(i, d), 0)),
                      pl.BlockSpec((ODD_CHUNK, C_HEAD_DIM), lambda b, d, i, lg: (blk(i, d), 0)),
                      pl.BlockSpec((ODD_CHUNK, C_HEAD_DIM), lambda b, d, i, lg: (blk(i, d), 0))],
            out_specs=pl.BlockSpec((1, 1, ODD_CHUNK, C_W), lambda b, d, i, lg: (d, b, blk(i, d), 0)),
            scratch_shapes=[pltpu.VMEM((C_HEADS, C_HEAD_DIM, C_HEAD_DIM), F32)]),
        out_shape=jax.ShapeDtypeStruct((2, bz, t, C_W), BF16),
        compiler_params=_params("parallel", "parallel", "arbitrary"),
        name="retention",
    )(log_gamma, pc, cos2, sin2)


def _mlstm_kernel(p_ref, bias_ref, o_ref, c_ref, n_ref, m_ref):
    d = pl.program_id(1)
    i = pl.program_id(2)

    @pl.when(i == 0)
    def _():
        c_ref[...] = jnp.zeros_like(c_ref)
        n_ref[...] = jnp.zeros_like(n_ref)
        m_ref[...] = jnp.full_like(m_ref, -jnp.inf)

    n = p_ref.shape[1]
    ti = lax.broadcasted_iota(jnp.int32, (n, n), 0)
    tj = lax.broadcasted_iota(jnp.int32, (n, n), 1)
    fwd = d == 0
    before_eq = jnp.where(fwd, ti - tj, tj - ti) >= 0
    gts = p_ref[0, :, 4 * D_W:] + bias_ref[...]
    gts = jnp.where(fwd, gts, pltpu.roll(gts, LANES - D_HEADS, 1))
    lf = jnp.minimum(gts, 0.0) - jnp.log(1.0 + jnp.exp(-jnp.abs(gts)))
    bcum = _dot_exact_lhs(before_eq.astype(F32), lf)
    btot = jnp.sum(lf, axis=0, keepdims=True)
    gts_t = gts.T
    bcum_t = bcum.T
    for h in range(D_HEADS):
        sl = slice(h * D_HEAD_DIM, (h + 1) * D_HEAD_DIM)
        q = p_ref[0, :, sl]
        k = p_ref[0, :, D_W + h * D_HEAD_DIM:D_W + (h + 1) * D_HEAD_DIM] * (D_HEAD_DIM ** -0.5)
        v = p_ref[0, :, 2 * D_W + h * D_HEAD_DIM:2 * D_W + (h + 1) * D_HEAD_DIM]
        ig_col = gts[:, h:h + 1]
        ig_row = gts_t[h:h + 1, :]
        b_col = bcum[:, 2 * D_HEADS + h:2 * D_HEADS + h + 1]
        b_row = bcum_t[2 * D_HEADS + h:2 * D_HEADS + h + 1, :]
        b_end = btot[:, 2 * D_HEADS + h:2 * D_HEADS + h + 1]
        m_prev = m_ref[h][0:1, 0:1]
        d_log = jnp.where(before_eq, b_col - b_row + ig_row, -jnp.inf)
        g_inter = b_col + m_prev
        m_t = jnp.maximum(g_inter, jnp.max(d_log, axis=1, keepdims=True))
        s_intra = _dot_nt(q, k) * jnp.exp(d_log - m_t)
        s_inter = jnp.exp(g_inter - m_t)
        num = _dot(s_intra, v) + _dot(q, c_ref[h]) * s_inter
        den = jnp.sum(s_intra, axis=1, keepdims=True) + jnp.sum(q * n_ref[h], axis=1, keepdims=True) * s_inter
        den = jnp.maximum(jnp.abs(den), jnp.exp(-m_t))
        o_ref[0, 0, :, sl] = (num / den).astype(o_ref.dtype)
        w_end = b_end - b_col + ig_col
        m_i = jnp.max(w_end, axis=0, keepdims=True)
        e_end = jnp.exp(w_end - m_i)
        ke = k * e_end
        m_new = jnp.maximum(b_end + m_prev, m_i)
        s_old = jnp.exp(b_end + m_prev - m_new)
        s_new = jnp.exp(m_i - m_new)
        c_ref[h] = c_ref[h] * s_old + _dot_tn(ke, v) * s_new
        n_ref[h] = n_ref[h] * s_old + jnp.sum(ke, axis=0, keepdims=True) * s_new
        m_ref[h] = jnp.broadcast_to(m_new, m_ref.shape[1:])


def _mlstm(pd, bias128, ctx_len):
    bz, t, w = pd.shape
    n_ctx = ctx_len // ODD_CHUNK
    n_chunks = t // ODD_CHUNK
    blk = lambda i, d: _seg_order(i, d, n_ctx, n_chunks)
    return pl.pallas_call(
        _mlstm_kernel,
        grid=(bz, 2, n_chunks),
        in_specs=[pl.BlockSpec((1, ODD_CHUNK, w), lambda b, d, i: (b, blk(i, d), 0)),
                  pl.BlockSpec((1, LANES), lambda b, d, i: (0, 0))],
        out_specs=pl.BlockSpec((1, 1, ODD_CHUNK, D_W), lambda b, d, i: (d, b, blk(i, d), 0)),
        out_shape=jax.ShapeDtypeStruct((2, bz, t, D_W), BF16),
        scratch_shapes=[pltpu.VMEM((D_HEADS, D_HEAD_DIM, D_HEAD_DIM), F32),
                        pltpu.VMEM((D_HEADS, 1, D_HEAD_DIM), F32),
                        pltpu.VMEM((D_HEADS, 8, LANES), F32)],
        compiler_params=_params("parallel", "parallel", "arbitrary"),
        name="mlstm",
    )(pd, bias128)


def _group_norm_lanes(x, ones_bd, group, eps):
    def group_mean(v):
        hi, lo = _split(v)
        return (jnp.dot(hi, ones_bd, preferred_element_type=F32)
                + jnp.dot(lo, ones_bd, preferred_element_type=F32)) * (1.0 / group)

    xc = x - group_mean(x)
    return xc * lax.rsqrt(group_mean(xc * xc) + eps)


def _group_norm_aligned(x, n_groups, eps):
    gw = x.shape[1] // n_groups
    outs = []
    for g in range(n_groups):
        xg = x[:, g * gw:(g + 1) * gw]
        xc = xg - jnp.mean(xg, axis=1, keepdims=True)
        outs.append(xc * lax.rsqrt(jnp.mean(xc * xc, axis=1, keepdims=True) + eps))
    return jnp.concatenate(outs, axis=1)


def _group_ones(width, group):
    return jnp.kron(jnp.eye(width // group, dtype=F32), jnp.ones((group, group), F32)).astype(BF16)


def _post_mixer(ycat, x_ref, wout_ref, mods_ref, lng_ref, lnb_ref, rw_ref, rb_ref, x1_o, h2_o, topi_o, topw_o, rank_o, cnt_o, cnt_scr, alpha):
    y = jnp.dot(ycat.astype(BF16), wout_ref[...], preferred_element_type=F32)
    mods = mods_ref[0, 0]
    x1 = alpha * x_ref[0] + mods[2:3] * y
    mu = jnp.mean(x1, axis=1, keepdims=True)
    xc = x1 - mu
    var = jnp.mean(xc * xc, axis=1, keepdims=True)
    x1 = xc * lax.rsqrt(var + LN_EPS) * lng_ref[...] + lnb_ref[...]
    x1_o[0] = x1
    h2 = x1 * (1.0 + mods[4:5]) + mods[3:4]
    h2_o[0] = h2
    logits = _dot_x3(h2, rw_ref[...]) + rb_ref[...]
    lane = lax.broadcasted_iota(jnp.int32, logits.shape, 1)
    work = jnp.where(lane < N_EXPERTS, logits, -jnp.inf)
    top_i = jnp.zeros(logits.shape, jnp.int32)
    top_v = jnp.full(logits.shape, -jnp.inf, F32)
    top = None
    picks = []
    for kth in range(TOP_K):
        mk = jnp.max(work, axis=1, keepdims=True)
        idx = jnp.min(jnp.where(work == mk, lane, LANES), axis=1, keepdims=True)
        pick = lane == idx
        picks.append(pick)
        work = jnp.where(pick, -jnp.inf, work)
        top_i = jnp.where(lane == kth, idx, top_i)
        top_v = jnp.where(lane == kth, mk, top_v)
        if kth == 0:
            top = mk
    e = jnp.exp(top_v - top)
    topi_o[0] = top_i
    topw_o[0] = e / jnp.sum(e, axis=1, keepdims=True)

    @pl.when(jnp.logical_and(pl.program_id(0) == 0, pl.program_id(1) == 0))
    def _():
        cnt_scr[...] = jnp.zeros_like(cnt_scr)

    rows = logits.shape[0]
    chosen = jnp.zeros(logits.shape, F32)
    for pick in picks:
        chosen = jnp.where(pick, 1.0, chosen)
    ti = lax.broadcasted_iota(jnp.int32, (rows, rows), 0)
    tj = lax.broadcasted_iota(jnp.int32, (rows, rows), 1)
    earlier = jnp.dot((tj < ti).astype(BF16), chosen.astype(BF16), preferred_element_type=F32) + cnt_scr[...]
    rank = jnp.zeros(logits.shape, F32)
    for kth, pick in enumerate(picks):
        rank = jnp.where(lane == kth, jnp.sum(jnp.where(pick, earlier, 0.0), axis=1, keepdims=True), rank)
    rank_o[0] = rank.astype(jnp.int32)
    cnt_scr[...] = cnt_scr[...] + jnp.sum(chosen, axis=0, keepdims=True)
    cnt_o[...] = cnt_scr[...].astype(jnp.int32)


def _outproj_even_kernel(wkvf_ref, wkvb_ref, bonus_ref, g_ref, gng_ref, gnb_ref, hb_ref, ones_ref,
                         x_ref, wout_ref, mods_ref, lng_ref, lnb_ref, rw_ref, rb_ref,
                         x1_o, h2_o, topi_o, topw_o, rank_o, cnt_o, cnt_scr, *, alpha):
    wkv = wkvf_ref[0] + wkvb_ref[0]
    normed = _group_norm_lanes(wkv, ones_ref[...], A_HEAD_DIM, GN_EPS_RWKV)
    ya = (normed * gng_ref[...] + gnb_ref[...] + bonus_ref[0].astype(F32)) * g_ref[0].astype(F32)
    yb = hb_ref[0, 0].astype(F32) + hb_ref[1, 0].astype(F32)
    ycat = jnp.concatenate([ya, yb], axis=1)
    _post_mixer(ycat, x_ref, wout_ref, mods_ref, lng_ref, lnb_ref, rw_ref, rb_ref, x1_o, h2_o, topi_o, topw_o, rank_o, cnt_o, cnt_scr, alpha)


def _outproj_odd_kernel(oc_ref, pc_ref, cg_ref, cb_ref, od_ref, pd_ref, dg_ref, db_ref,
                        x_ref, wout_ref, mods_ref, lng_ref, lnb_ref, rw_ref, rb_ref,
                        x1_o, h2_o, topi_o, topw_o, rank_o, cnt_o, cnt_scr, *, alpha):
    oc = oc_ref[0, 0].astype(F32) + oc_ref[1, 0].astype(F32)
    gc = pc_ref[0]
    ya = (_group_norm_aligned(oc, C_HEADS, LN_EPS) * cg_ref[...] + cb_ref[...]) * (gc * _sigmoid(gc))
    od = od_ref[0, 0].astype(F32) + od_ref[1, 0].astype(F32)
    yb = (_group_norm_aligned(od, D_HEADS, LN_EPS) * dg_ref[...] + db_ref[...]) * _sigmoid(pd_ref[0])
    ycat = jnp.concatenate([ya, yb], axis=1)
    _post_mixer(ycat, x_ref, wout_ref, mods_ref, lng_ref, lnb_ref, rw_ref, rb_ref, x1_o, h2_o, topi_o, topw_o, rank_o, cnt_o, cnt_scr, alpha)


def _outproj(kind, mixer_args, mixer_specs, x, wout, mods, ln_g, ln_b, rw, rb, n_ctx_tiles, alpha, first_tile):
    bz, t, d = x.shape
    t_out = t - first_tile * ROW_TILE
    seg = lambda i: (i + first_tile >= n_ctx_tiles).astype(jnp.int32)
    row = lambda a: a.reshape(1, -1)
    vec = lambda n: pl.BlockSpec((1, n), lambda b, i: (0, 0))
    kern = _outproj_even_kernel if kind == 'even' else _outproj_odd_kernel
    tok = lambda n: pl.BlockSpec((1, ROW_TILE, n), lambda b, i: (b, i, 0))
    return pl.pallas_call(
        functools.partial(kern, alpha=alpha),
        grid=(bz, t_out // ROW_TILE),
        in_specs=list(mixer_specs) + [
            pl.BlockSpec((1, ROW_TILE, d), lambda b, i: (b, i + first_tile, 0)),
            pl.BlockSpec((d, d), lambda b, i: (0, 0)),
            pl.BlockSpec((1, 1, 8, d), lambda b, i: (b, seg(i), 0, 0)),
            vec(d), vec(d),
            pl.BlockSpec((d, LANES), lambda b, i: (0, 0)),
            vec(LANES)],
        out_specs=[tok(d), tok(d), tok(LANES), tok(LANES), tok(LANES), vec(LANES)],
        out_shape=[jax.ShapeDtypeStruct((bz, t_out, d), F32),
                   jax.ShapeDtypeStruct((bz, t_out, d), F32),
                   jax.ShapeDtypeStruct((bz, t_out, LANES), jnp.int32),
                   jax.ShapeDtypeStruct((bz, t_out, LANES), F32),
                   jax.ShapeDtypeStruct((bz, t_out, LANES), jnp.int32),
                   jax.ShapeDtypeStruct((1, LANES), jnp.int32)],
        scratch_shapes=[pltpu.VMEM((1, LANES), F32)],
        compiler_params=_params("arbitrary", "arbitrary"),
        name="outproj_" + kind,
    )(*mixer_args, x, wout, mods, row(ln_g), row(ln_b), rw, rb)


def _route(top_i, rank, counts):
    experts = jnp.arange(N_EXPERTS, dtype=jnp.int32)
    e_flat = top_i[..., :TOP_K].reshape(-1)
    n_asg = e_flat.shape[0]
    cnt = counts[0, :N_EXPERTS]
    start = jnp.cumsum(cnt) - cnt
    dest = rank[..., :TOP_K].reshape(-1) + jnp.sum(jnp.where(e_flat[:, None] == experts, start, 0), axis=1)
    first_t = start // EXPERT_TILE
    last_t = (start + cnt - 1) // EXPERT_TILE
    n_t = jnp.where(cnt > 0, last_t - first_t + 1, 0)
    w_end = jnp.cumsum(n_t)
    n_work = w_end[-1]
    w = jnp.minimum(jnp.arange(n_asg // EXPERT_TILE + N_EXPERTS - 1, dtype=jnp.int32), n_work - 1)
    we = jnp.sum((w_end[None, :] <= w[:, None]).astype(jnp.int32), axis=1)
    pick = we[:, None] == experts
    sel = lambda v: jnp.sum(jnp.where(pick, v, 0), axis=1)
    wt = sel(first_t) + (w - sel(w_end - n_t))
    lo = sel(start)
    hi = lo + sel(cnt)
    slot = sel(jnp.cumsum((cnt > 0).astype(jnp.int32)) - 1) % 2
    w_next = sel(w_end)
    nxt = jnp.where(w_next < n_work, jnp.sum((w_end[None, :] <= w_next[:, None]).astype(jnp.int32), axis=1), -1)
    i32 = lambda v: v.astype(jnp.int32)
    return i32(dest), (i32(wt), i32(we), i32(lo), i32(hi), i32(n_work.reshape(1)), i32(slot), i32(nxt))


def _dispatch_kernel(dest_ref, h_ref, xs_hbm, stage, sems):
    i = pl.program_id(0)
    rows = h_ref.shape[0]
    base = i * rows * TOP_K

    def drain(s):
        for _ in range(TOP_K):
            pltpu.make_async_copy(stage.at[s], xs_hbm.at[pl.ds(0, rows)], sems.at[s]).wait()

    for s in range(2):
        @pl.when(i % 2 == s)
        def _(s=s):
            stage[s] = h_ref[...]

            def issue(t, carry):
                for k in range(TOP_K):
                    row = dest_ref[base + t * TOP_K + k]
                    pltpu.make_async_copy(stage.at[s, pl.ds(t, 1)], xs_hbm.at[pl.ds(row, 1)],
                                          sems.at[s]).start(priority=k % 2)
                return carry

            lax.fori_loop(0, rows, issue, 0)

            @pl.when(i > 0)
            def _():
                drain(1 - s)

            @pl.when(i == pl.num_programs(0) - 1)
            def _():
                drain(s)


def _dispatch(h2, dest):
    n, d = h2.shape
    return pl.pallas_call(
        _dispatch_kernel,
        grid_spec=pltpu.PrefetchScalarGridSpec(
            num_scalar_prefetch=1,
            grid=(n // ROW_TILE,),
            in_specs=[pl.BlockSpec((ROW_TILE, d), lambda i, dest: (i, 0))],
            out_specs=pl.BlockSpec(memory_space=pl.ANY),
            scratch_shapes=[pltpu.VMEM((2, ROW_TILE, d), F32), pltpu.SemaphoreType.DMA((2,))]),
        out_shape=jax.ShapeDtypeStruct((n * TOP_K, d), F32),
        compiler_params=_params("arbitrary"),
        name="moe_dispatch",
    )(dest, h2)


def _experts_kernel(wt_ref, we_ref, lo_ref, hi_ref, nw_ref, slot_ref, nxt_ref, xs_ref, w1_hbm, b1_ref, w2_hbm,
                    b2_ref, y_ref, w1f, w2f, w1b, w2b, sems, *, f_chunk, layer):
    w = pl.program_id(0)
    valid = w < nw_ref[0]
    prev = jnp.maximum(w - 1, 0)
    new_expert = jnp.logical_or(w == 0, we_ref[w] != we_ref[prev])
    new_tile = jnp.logical_or(w == 0, wt_ref[w] != wt_ref[prev])

    def weight_copies(e, s):
        return (pltpu.make_async_copy(w1_hbm.at[layer, e], w1f.at[s], sems.at[0, s]),
                pltpu.make_async_copy(w2_hbm.at[layer, e], w2f.at[s], sems.at[1, s]))

    @pl.when(w == 0)
    def _():
        for cp in weight_copies(we_ref[0], slot_ref[0]):
            cp.start()

    @pl.when(jnp.logical_and(valid, new_expert))
    def _():
        s = slot_ref[w]
        for cp in weight_copies(we_ref[w], s):
            cp.wait()
        w1b[...] = w1f[s].astype(BF16)
        w2b[...] = w2f[s].astype(BF16)

        @pl.when(nxt_ref[w] >= 0)
        def _():
            for cp in weight_copies(nxt_ref[w], 1 - s):
                cp.start()

    @pl.when(valid)
    def _():
        x = xs_ref[...].astype(BF16)
        acc = jnp.zeros(y_ref.shape, F32)
        for c in range(D_FF // f_chunk):
            c0 = c * f_chunk
            glu = jnp.dot(x, w1b[:, c0:c0 + f_chunk], preferred_element_type=F32) + b1_ref[0, :, c0:c0 + f_chunk]
            lin = (jnp.dot(x, w1b[:, D_FF + c0:D_FF + c0 + f_chunk], preferred_element_type=F32)
                   + b1_ref[0, :, D_FF + c0:D_FF + c0 + f_chunk])
            glu = jnp.minimum(glu, SWIGLU_LIMIT)
            lin = jnp.clip(lin, -SWIGLU_LIMIT, SWIGLU_LIMIT)
            act = (lin + 1.0) * glu * _sigmoid(SWIGLU_ALPHA * glu)
            acc += jnp.dot(act.astype(BF16), w2b[c0:c0 + f_chunk, :], preferred_element_type=F32)
        y = acc + b2_ref[0]
        rows = wt_ref[w] * EXPERT_TILE + lax.broadcasted_iota(jnp.int32, (EXPERT_TILE, 1), 0)
        mine = jnp.logical_and(rows >= lo_ref[w], rows < hi_ref[w])

        @pl.when(new_tile)
        def _():
            y_ref[...] = jnp.where(mine, y, 0.0)

        @pl.when(jnp.logical_not(new_tile))
        def _():
            y_ref[...] = jnp.where(mine, y, y_ref[...])


def _experts(xs, work, w1, b1, w2, b2, layer):
    n_asg, d = xs.shape
    n_l, n_e = w1.shape[:2]
    n_work_max = work[0].shape[0]
    return pl.pallas_call(
        functools.partial(_experts_kernel, f_chunk=512, layer=layer),
        grid_spec=pltpu.PrefetchScalarGridSpec(
            num_scalar_prefetch=7,
            grid=(n_work_max,),
            in_specs=[pl.BlockSpec((EXPERT_TILE, d), lambda w, wt, we, *_: (wt[w], 0)),
                      pl.BlockSpec(memory_space=pl.ANY),
                      pl.BlockSpec((None, 1, 1, 2 * D_FF), lambda w, wt, we, *_: (layer, we[w], 0, 0)),
                      pl.BlockSpec(memory_space=pl.ANY),
                      pl.BlockSpec((None, 1, 1, d), lambda w, wt, we, *_: (layer, we[w], 0, 0))],
            out_specs=pl.BlockSpec((EXPERT_TILE, d), lambda w, wt, we, *_: (wt[w], 0)),
            scratch_shapes=[pltpu.VMEM((2, d, 2 * D_FF), F32), pltpu.VMEM((2, D_FF, d), F32),
                            pltpu.VMEM((d, 2 * D_FF), BF16), pltpu.VMEM((D_FF, d), BF16),
                            pltpu.SemaphoreType.DMA((2, 2))]),
        out_shape=jax.ShapeDtypeStruct((n_asg, d), F32),
        compiler_params=_params("arbitrary"),
        name="moe_experts",
    )(*work, xs, w1, b1.reshape(n_l, n_e, 1, -1), w2, b2.reshape(n_l, n_e, 1, -1))


def _combine_kernel(dest_ref, y_hbm, topw_ref, x_ref, mods_ref, lng_ref, lnb_ref, o_ref, ybuf, sems, *, alpha):
    rows = x_ref.shape[1]
    tile = pl.program_id(0) * pl.num_programs(1) + pl.program_id(1)
    n_tiles = pl.num_programs(0) * pl.num_programs(1)

    def gather(which, s):
        base = which * rows * TOP_K

        def issue(t, carry):
            for k in range(TOP_K):
                row = dest_ref[base + t * TOP_K + k]
                pltpu.make_async_copy(y_hbm.at[pl.ds(row, 1)], ybuf.at[s, k, pl.ds(t, 1)],
                                      sems.at[s]).start(priority=k % 2)
            return carry

        lax.fori_loop(0, rows, issue, 0)

    @pl.when(tile == 0)
    def _():
        gather(0, 0)

    for s in range(2):
        @pl.when(tile % 2 == s)
        def _(s=s):
            @pl.when(tile + 1 < n_tiles)
            def _():
                gather(tile + 1, 1 - s)

            for k in range(TOP_K):
                pltpu.make_async_copy(y_hbm.at[pl.ds(0, rows)], ybuf.at[s, k], sems.at[s]).wait()
            tw = topw_ref[0]
            ffn = ybuf[s, 0] * tw[:, 0:1]
            for k in range(1, TOP_K):
                ffn = ffn + ybuf[s, k] * tw[:, k:k + 1]
            mods = mods_ref[0, 0]
            x2 = alpha * x_ref[0] + mods[5:6] * ffn
            mu = jnp.mean(x2, axis=1, keepdims=True)
            xc = x2 - mu
            var = jnp.mean(xc * xc, axis=1, keepdims=True)
            o_ref[0] = xc * lax.rsqrt(var + LN_EPS) * lng_ref[...] + lnb_ref[...]


def _combine(y_sorted, dest, topw, x1, mods, ln_g, ln_b, n_ctx_tiles, alpha, first_tile):
    bz, t, d = x1.shape
    seg = lambda i: (i + first_tile >= n_ctx_tiles).astype(jnp.int32)
    tok = lambda n: pl.BlockSpec((1, ROW_TILE, n), lambda b, i, dest: (b, i, 0))
    vec = pl.BlockSpec((1, d), lambda b, i, dest: (0, 0))
    return pl.pallas_call(
        functools.partial(_combine_kernel, alpha=alpha),
        grid_spec=pltpu.PrefetchScalarGridSpec(
            num_scalar_prefetch=1,
            grid=(bz, t // ROW_TILE),
            in_specs=[pl.BlockSpec(memory_space=pl.ANY), tok(LANES), tok(d),
                      pl.BlockSpec((1, 1, 8, d), lambda b, i, dest: (b, seg(i), 0, 0)), vec, vec],
            out_specs=tok(d),
            scratch_shapes=[pltpu.VMEM((2, TOP_K, ROW_TILE, d), F32), pltpu.SemaphoreType.DMA((2,))]),
        out_shape=jax.ShapeDtypeStruct((bz, t, d), F32),
        compiler_params=_params("arbitrary", "arbitrary"),
        name="moe_combine",
    )(dest, y_sorted, topw, x1, mods, ln_g.reshape(1, d), ln_b.reshape(1, d))


def _rope_tables(ctx_len, seq):
    idx = jnp.arange(seq)
    rowp = (idx // GRID_W).astype(F32)
    colp = (idx % GRID_W).astype(F32)
    n_freq = C_HEAD_DIM // 4
    inv = ROPE_BASE ** (-jnp.arange(n_freq, dtype=F32) / n_freq)
    ang = jnp.concatenate([rowp[:, None] * inv, colp[:, None] * inv], -1)
    cos, sin = jnp.cos(ang), jnp.sin(ang)
    cos2 = jnp.concatenate([cos, cos], -1)
    sin2 = jnp.concatenate([-sin, sin], -1)
    cos2 = jnp.concatenate([jnp.ones((ctx_len, C_HEAD_DIM), F32), cos2], 0)
    sin2 = jnp.concatenate([jnp.zeros((ctx_len, C_HEAD_DIM), F32), sin2], 0)
    return cos2, sin2


def kernel(x, c, ctx, c_ctx, w_mod, b_mod, ln_g, ln_b, even_w_in, even_w_out, a_mu, a_w0, a_wB, a_a0, a_aB, a_gB, a_kk, a_ka, a_rk, a_gn_g, a_gn_b, b_conv_w, b_conv_b, b_gate_w, b_gate_b, b_lam, odd_w_in, odd_w_out, c_log_gamma, c_gn_g, c_gn_b, d_ibias, d_fbias, d_gn_g, d_gn_b, router_w, router_b, exp_w1, exp_b1, exp_w2, exp_b2):
    bz, seq, d = x.shape
    ctx_len = ctx.shape[1]
    depth = w_mod.shape[0]
    assert ctx_len == ROW_TILE and seq % ROW_TILE == 0
    t = ctx_len + seq
    n_ctx_tiles = ctx_len // ROW_TILE
    alpha = (2 * depth) ** 0.25
    xs = jnp.concatenate([ctx, x], axis=1)
    c8 = jnp.concatenate([c, c_ctx[None], jnp.zeros((8 - bz - 1, d), F32)], 0)
    cos2, sin2 = _rope_tables(ctx_len, seq)
    row = lambda a: a.reshape(1, -1)
    vec = lambda n: pl.BlockSpec((1, n), lambda b, i: (0, 0))

    for layer in range(depth):
        first = n_ctx_tiles if layer == depth - 1 else 0
        tok = lambda n, first=first: pl.BlockSpec((1, ROW_TILE, n), lambda b, i: (b, i + first, 0))
        tok2 = lambda n, first=first: pl.BlockSpec((2, 1, ROW_TILE, n), lambda b, i: (0, b, i + first, 0))
        m = _mod_vectors(c8, w_mod, b_mod[layer], layer).reshape(8, N_MOD, d)
        m_lat = m[:bz]
        m_ctx = jnp.broadcast_to(m[bz][None], (bz, N_MOD, d))
        mods = jnp.stack([m_ctx, m_lat], axis=1)
        mods = jnp.concatenate([mods, jnp.zeros((bz, 2, 8 - N_MOD, d), F32)], axis=2)
        shift = mods[:, :, 0:1]
        scale = mods[:, :, 1:2]
        rw = jnp.pad(router_w[layer], ((0, 0), (0, LANES - N_EXPERTS)))
        rb = jnp.pad(router_b[layer], (0, LANES - N_EXPERTS)).reshape(1, LANES)
        li = layer // 2
        if layer % 2 == 0:
            pa, pb = _inproj(xs, shift, scale, even_w_in[li].astype(BF16), A_IN, n_ctx_tiles)
            r, v, kk, g, bonus, lw, kd, a = _rwkv_prep(pa, a_mu[li], a_w0[li], a_wB[li], a_a0[li], a_aB[li],
                                                       a_gB[li], a_kk[li], a_ka[li], a_rk[li].reshape(-1),
                                                       n_ctx_tiles)
            wkv_f, wkv_b = _rwkv_scan(r, lw, kd, v, kk, a, ctx_len)
            hb = _rglru(pb, b_conv_w[li], b_conv_b[li], b_gate_w[li], b_gate_b[li], b_lam[li], n_ctx_tiles)
            mixer_args = (wkv_f, wkv_b, bonus, g, row(a_gn_g[li]), row(a_gn_b[li]), hb, _group_ones(A_W, A_HEAD_DIM))
            mixer_specs = (tok(A_W), tok(A_W), tok(A_W), tok(A_W), vec(A_W), vec(A_W), tok2(B_W),
                           pl.BlockSpec((A_W, A_W), lambda b, i: (0, 0)))
            x1, h2, topi, topw, rank, counts = _outproj('even', mixer_args, mixer_specs, xs, even_w_out[li].astype(BF16), mods,
                                     ln_g[layer, 0], ln_b[layer, 0], rw, rb, n_ctx_tiles, alpha, first)
        else:
            w_in = jnp.pad(odd_w_in[li], ((0, 0), (0, D_IN_PAD - D_IN))).astype(BF16)
            pc, pd = _inproj(xs, shift, scale, w_in, C_IN, n_ctx_tiles)
            oc = _retention(pc, c_log_gamma[li], cos2, sin2, ctx_len)
            bias128 = jnp.concatenate([d_ibias[li].reshape(-1), d_fbias[li].reshape(-1),
                                       jnp.zeros((LANES - 4 * D_HEADS,), F32)]).reshape(1, LANES)
            od = _mlstm(pd, bias128, ctx_len)
            gate_c = pl.BlockSpec((1, ROW_TILE, C_W), lambda b, i, first=first: (b, i + first, 3))
            gate_d = pl.BlockSpec((1, ROW_TILE, D_W), lambda b, i, first=first: (b, i + first, 3))
            mixer_args = (oc, pc, row(c_gn_g[li]), row(c_gn_b[li]), od, pd, row(d_gn_g[li]), row(d_gn_b[li]))
            mixer_specs = (tok2(C_W), gate_c, vec(C_W), vec(C_W), tok2(D_W), gate_d, vec(D_W), vec(D_W))
            x1, h2, topi, topw, rank, counts = _outproj('odd', mixer_args, mixer_specs, xs, odd_w_out[li].astype(BF16), mods,
                                     ln_g[layer, 0], ln_b[layer, 0], rw, rb, n_ctx_tiles, alpha, first)
        dest, work = _route(topi, rank, counts)
        xs_sorted = _dispatch(h2.reshape(-1, d), dest)
        y_sorted = _experts(xs_sorted, work, exp_w1, exp_b1, exp_w2, exp_b2, layer)
        xs = _combine(y_sorted, dest, topw, x1, mods, ln_g[layer, 1], ln_b[layer, 1], n_ctx_tiles, alpha, first)
    return xs[:, xs.shape[1] - seq:]
```
